```python
import math
import jax
import jax.numpy as jnp
from jax import lax
import numpy as np

D_MODEL = 1024
BATCH = 32
SEQ = 2048
DEPTH = 1
DEC_BATCH = 128
DEC_SEQ = 8
PAST_LEN = 8192
PAGE_SIZE = 128

M_HEADS = 4
M_DV = 256
M_DQK = 128
M_WIDTH = M_HEADS * M_DV
M_CONV = 4
M_CHUNK = 64
N_HEADS = 16
N_KV = 2
N_HD = 64
N_HPG = N_HEADS // N_KV
N_WIDTH = N_HEADS * N_HD
KV_W = N_KV * N_HD
CMP_LEN = 32
CMP_STRIDE = 16
SLC_BLOCK = 64
N_SELECT = 16
WINDOW = 512
NSA_Q_BLOCK = 16
FORCE_BONUS = 1000.0
N_EXPERTS = 32
TOP_K = 4
D_EXPERT = 1024
SWIGLU_LIMIT = 7.0
SWIGLU_ALPHA = 1.702
MOE_BLOCK = 128
EPS = 1e-6
TINY = 1e-30

OFF_MX = 0
OFF_MV = OFF_MX + M_WIDTH
OFF_MO = OFF_MV + M_WIDTH
OFF_MI = OFF_MO + M_WIDTH
OFF_MF = OFF_MI + M_HEADS
OFF_NQ = OFF_MF + M_HEADS
OFF_NKV = OFF_NQ + N_WIDTH
OFF_NG = OFF_NKV + 6 * KV_W
OFF_BG = OFF_NG + 3 * N_HEADS
N_IN = OFF_BG + 2 * D_MODEL

kernel_name = 'mlstm_nsa_moe_hybrid_step'


def rmsnorm(x, g):
    xf = x.astype(jnp.float32)
    y = xf * lax.rsqrt(jnp.mean(xf * xf, axis=-1, keepdims=True) + EPS)
    return (y * g).astype(x.dtype)


def masked_softmax(s, mask, axes):
    s = jnp.where(mask, s, -jnp.inf)
    mx = jnp.max(s, axis=axes, keepdims=True)
    mx = jnp.where(jnp.isfinite(mx), mx, 0.0)
    p = jnp.exp(s - mx)
    return p / jnp.maximum(jnp.sum(p, axis=axes, keepdims=True), TINY)


def project_in(xn, w_in, b_in):
    z = xn @ w_in + b_in
    B, T = z.shape[:2]
    q = z[..., OFF_NQ:OFF_NKV].reshape(B, T, N_HEADS, N_HD)
    kv = z[..., OFF_NKV:OFF_NG].reshape(B, T, 6, N_KV, N_HD)
    gates = jax.nn.sigmoid(z[..., OFF_NG:OFF_BG].astype(jnp.float32)).reshape(B, T, N_HEADS, 3)
    return z, q, kv, gates


def causal_conv(x, buf, w, b):
    T = x.shape[1]
    xp = jnp.concatenate([buf, x], axis=1)
    y = sum(xp[:, j:j + T] * w[j] for j in range(M_CONV)) + b
    return y, xp[:, -(M_CONV - 1):]


def mlstm_scan(q, k, v, ig, lf, C0, n0, m0):
    B, T, H, _ = q.shape
    L = math.gcd(T, M_CHUNK)
    nc = T // L

    def to_chunks(a):
        return a.reshape((B, nc, L) + a.shape[2:]).swapaxes(0, 1)

    causal = jnp.tril(jnp.ones((L, L), dtype=bool))

    def step(carry, xs):
        C, n, m = carry
        qc, kc, vc, ic, fc = xs
        b = jnp.cumsum(fc, axis=1)
        a = b + m[:, None, :]
        dmat = b[:, :, None, :] - b[:, None, :, :] + ic[:, None, :, :]
        dmat = jnp.where(causal[None, :, :, None], dmat, -jnp.inf)
        mt = jnp.maximum(a, jnp.max(dmat, axis=2))
        w_intra = jnp.exp(dmat - mt[:, :, None, :])
        w_inter = jnp.exp(a - mt)
        qk = jnp.einsum('bthd,bshd->btsh', qc, kc) * w_intra
        num = jnp.einsum('btsh,bshv->bthv', qk, vc) + w_inter[..., None] * jnp.einsum('bthd,bhdv->bthv', qc, C)
        den = jnp.sum(qk, axis=2) + w_inter * jnp.einsum('bthd,bhd->bth', qc, n)
        h = num / jnp.maximum(jnp.abs(den), jnp.exp(-mt))[..., None]
        m_new = mt[:, -1]
        w_end = jnp.exp(dmat[:, -1] - m_new[:, None, :])
        decay = jnp.exp(a[:, -1] - m_new)
        C_new = decay[..., None, None] * C + jnp.einsum('bsh,bshd,bshv->bhdv', w_end, kc, vc)
        n_new = decay[..., None] * n + jnp.einsum('bsh,bshd->bhd', w_end, kc)
        return (C_new, n_new, m_new), h

    xs = (to_chunks(q), to_chunks(k), to_chunks(v), to_chunks(ig), to_chunks(lf))
    (C, n, m), h = lax.scan(step, (C0, n0, m0), xs)
    return h.swapaxes(0, 1).reshape(B, T, H, M_DV), C, n, m


def mlstm_branch(z, conv_buf, C0, n0, m0, w_conv, b_conv, w_mq, w_mk, g_mnorm):
    f32 = jnp.float32
    B, T = z.shape[:2]
    xm = z[..., OFF_MX:OFF_MV]
    v = z[..., OFF_MV:OFF_MO].reshape(B, T, M_HEADS, M_DV).astype(f32)
    o = jax.nn.sigmoid(z[..., OFF_MO:OFF_MI].astype(f32))
    ig = z[..., OFF_MI:OFF_MF].astype(f32)
    lf = jax.nn.log_sigmoid(z[..., OFF_MF:OFF_NQ].astype(f32))
    xc, new_buf = causal_conv(xm, conv_buf.astype(xm.dtype), w_conv, b_conv)
    xc = jax.nn.silu(xc.astype(f32)).reshape(B, T, M_HEADS, M_DV)
    q = jnp.einsum('bthc,hcd->bthd', xc, w_mq.astype(f32))
    k = jnp.einsum('bthc,hcd->bthd', xc, w_mk.astype(f32)) * (M_DQK ** -0.5)
    h, C, n, m = mlstm_scan(q, k, v, ig, lf, C0.astype(f32), n0.astype(f32), m0.astype(f32))
    hn = h * lax.rsqrt(jnp.mean(h * h, axis=-1, keepdims=True) + EPS) * g_mnorm
    return hn.reshape(B, T, M_WIDTH) * o, new_buf, C, n, m


def cmp_segments(rows, w):
    B, L = rows.shape[:2]
    R = CMP_LEN // CMP_STRIDE
    nseg = L // CMP_STRIDE
    seg = rows[:, :nseg * CMP_STRIDE].reshape(B, nseg, CMP_STRIDE, N_KV, N_HD)
    return jnp.einsum('bnsgd,rsde->rbnge', seg, w.reshape(R, CMP_STRIDE, N_HD, N_HD))


def cmp_blocks(segs, b):
    R = CMP_LEN // CMP_STRIDE
    nc = segs.shape[2] - R + 1
    return sum(segs[r, :, r:r + nc] for r in range(R)) + b


def cmp_to_slc_matrix(nc, ns):
    c0 = np.arange(nc)[:, None] * CMP_STRIDE
    s0 = np.arange(ns)[None, :] * SLC_BLOCK
    ov = np.clip(np.minimum(c0 + CMP_LEN, s0 + SLC_BLOCK) - np.maximum(c0, s0), 0, None)
    return jnp.asarray(ov / CMP_LEN, dtype=jnp.float32)


def to_blocks(a):
    B, L = a.shape[:2]
    ns = -(-L // SLC_BLOCK)
    a = jnp.pad(a, [(0, 0), (0, ns * SLC_BLOCK - L)] + [(0, 0)] * (a.ndim - 2))
    return a.reshape((B, ns, SLC_BLOCK) + a.shape[2:])


def nsa_attend(q, qpos, ck, cv, ns, gather_sel, wk, wv, wpos, gates):
    f32 = jnp.float32
    B, T = q.shape[:2]
    nc = ck.shape[1]
    qg = q.reshape(B, T, N_KV, N_HPG, N_HD).astype(f32) * (N_HD ** -0.5)
    c_end = jnp.arange(nc) * CMP_STRIDE + (CMP_LEN - 1)
    cmask = (c_end[None, :] <= qpos[:, None])[None, :, None, None, :]
    p_cmp = masked_softmax(jnp.einsum('btghd,bcgd->btghc', qg, ck.astype(f32)), cmask, (-1,))
    o_cmp = jnp.einsum('btghc,bcgd->btghd', p_cmp, cv.astype(f32))
    imp = jnp.einsum('btgc,cn->btgn', jnp.sum(p_cmp, axis=3), cmp_to_slc_matrix(nc, ns))
    blk = jnp.arange(ns)[None, :]
    cur = (qpos // SLC_BLOCK)[:, None]
    allowed = blk <= cur
    forced = (blk == 0) | (blk == cur) | (blk == cur - 1)
    score = jnp.where(allowed[None, :, None, :], imp + FORCE_BONUS * forced[None, :, None, :], -jnp.inf)
    top_s, idx = lax.top_k(score, min(N_SELECT, ns))
    idx = idx.transpose(0, 2, 1, 3)
    ok = jnp.isfinite(top_s).transpose(0, 2, 1, 3)
    ksel, vsel = gather_sel(idx)
    kpos = idx[..., None] * SLC_BLOCK + jnp.arange(SLC_BLOCK)
    smask = ((kpos <= qpos[None, None, :, None, None]) & ok[..., None])[:, :, :, None]
    s = jnp.einsum('btghd,bgtnkd->bgthnk', qg, ksel.astype(f32))
    p_slc = masked_softmax(s, smask, (-2, -1))
    o_slc = jnp.einsum('bgthnk,bgtnkd->btghd', p_slc, vsel.astype(f32))
    dpos = qpos[:, None] - wpos[None, :]
    wmask = ((wpos[None, :] >= 0) & (dpos >= 0) & (dpos < WINDOW))[None, :, None, None, :]
    p_win = masked_softmax(jnp.einsum('btghd,bwgd->btghw', qg, wk.astype(f32)), wmask, (-1,))
    o_win = jnp.einsum('btghw,bwgd->btghd', p_win, wv.astype(f32))
    g = gates.reshape(B, T, N_KV, N_HPG, 3)
    o = g[..., 0:1] * o_cmp + g[..., 1:2] * o_slc + g[..., 2:3] * o_win
    return o.reshape(B, T, N_WIDTH)


def nsa_prompt(q, kv, gates, w_cmp, b_cmp):
    B, S = q.shape[:2]
    ck = cmp_blocks(cmp_segments(kv[:, :, 0], w_cmp[0]), b_cmp[0])
    cv = cmp_blocks(cmp_segments(kv[:, :, 1], w_cmp[1]), b_cmp[1])
    blocks = to_blocks(kv[:, :, 2:4])
    b_ix = jnp.arange(B)[:, None, None, None]
    g_ix = jnp.arange(N_KV)[None, :, None, None]

    def gather_sel(idx):
        return blocks[b_ix, idx, :, 0, g_ix], blocks[b_ix, idx, :, 1, g_ix]

    pad = ((0, 0), (WINDOW, 0), (0, 0), (0, 0))
    wk = jnp.pad(kv[:, :, 4], pad)
    wv = jnp.pad(kv[:, :, 5], pad)
    qb = math.gcd(S, NSA_Q_BLOCK)

    def one(i):
        start = i * qb
        return nsa_attend(lax.dynamic_slice_in_dim(q, start, qb, 1), start + jnp.arange(qb), ck, cv,
                          blocks.shape[1], gather_sel,
                          lax.dynamic_slice_in_dim(wk, start, WINDOW + qb, 1),
                          lax.dynamic_slice_in_dim(wv, start, WINDOW + qb, 1),
                          start - WINDOW + jnp.arange(WINDOW + qb),
                          lax.dynamic_slice_in_dim(gates, start, qb, 1))

    o = lax.map(one, jnp.arange(S // qb))
    return o.swapaxes(0, 1).reshape(B, S, N_WIDTH)


def nsa_sample(q, kv, gates, cache_kv, layer, page_table, win_buf, w_cmp, b_cmp):
    B, T = q.shape[:2]
    past_len = page_table.shape[1] * PAGE_SIZE
    b_ix = jnp.arange(B)[:, None, None, None]
    g_ix = jnp.arange(N_KV)[None, :, None, None]

    def cmp_branch(c):
        past = cache_kv[page_table, :, layer, c].reshape(B, past_len, N_KV, N_HD)
        segs = jnp.concatenate([cmp_segments(past, w_cmp[c]), cmp_segments(kv[:, :, c], w_cmp[c])], axis=2)
        return cmp_blocks(segs, b_cmp[c])

    ck, cv = cmp_branch(0), cmp_branch(1)
    new_blocks = to_blocks(kv[:, :, 2:4])
    nnb = new_blocks.shape[1]
    npb = past_len // SLC_BLOCK
    bpp = PAGE_SIZE // SLC_BLOCK

    def gather_sel(idx):
        in_past = (idx < npb)[..., None, None]
        jc = jnp.minimum(idx, npb - 1)
        page = page_table[b_ix, jc // bpp]
        row = ((jc % bpp) * SLC_BLOCK)[..., None] + jnp.arange(SLC_BLOCK)
        loc = jnp.clip(idx - npb, 0, nnb - 1)

        def pick(c):
            past = cache_kv[page[..., None], row, layer, 2 + c, g_ix[..., None]]
            new = new_blocks[b_ix, loc, :, c, g_ix]
            return jnp.where(in_past, past, new.astype(past.dtype))

        return pick(0), pick(1)

    wb = win_buf.shape[1]
    win_all = jnp.concatenate([win_buf.astype(kv.dtype), kv[:, :, 4:]], axis=1)
    qpos = past_len + jnp.arange(T)
    wpos = past_len - wb + jnp.arange(wb + T)
    o = nsa_attend(q, qpos, ck, cv, npb + nnb, gather_sel, win_all[:, :, 0], win_all[:, :, 1], wpos, gates)
    return o, win_all[:, T:]


def merge(z, h_m, h_n, w_bm, w_bn, w_out):
    g = jax.nn.sigmoid(z[..., OFF_BG:N_IN].astype(jnp.float32))
    return (g[..., :D_MODEL] * (h_m @ w_bm) + g[..., D_MODEL:] * (h_n @ w_bn)) @ w_out


def expert_ffn(xe, wgu, bgu, wd, bd):
    gu = xe @ wgu + bgu
    gate = jnp.minimum(gu[..., :D_EXPERT], SWIGLU_LIMIT)
    up = jnp.clip(gu[..., D_EXPERT:], -SWIGLU_LIMIT, SWIGLU_LIMIT)
    glu = gate * jax.nn.sigmoid(gate * SWIGLU_ALPHA)
    return ((up + 1.0) * glu) @ wd + bd


def moe(x, w_router, b_router, w_gu, b_gu, w_dn, b_dn):
    B, T, D = x.shape
    N = B * T
    xf = x.reshape(N, D)
    logits = (xf @ w_router + b_router).astype(jnp.float32)
    top_v, top_e = lax.top_k(logits, TOP_K)
    top_w = jax.nn.softmax(top_v, axis=-1)
    nk = N * TOP_K
    flat_e = top_e.reshape(-1)
    order = jnp.argsort(flat_e)
    sorted_e = flat_e[order]
    counts = jnp.bincount(flat_e, length=N_EXPERTS)
    padded = (counts + MOE_BLOCK - 1) // MOE_BLOCK * MOE_BLOCK
    pad_end = jnp.cumsum(padded)
    pad_start = pad_end - padded
    grp_start = jnp.cumsum(counts) - counts
    dest = pad_start[sorted_e] + jnp.arange(nk) - grp_start[sorted_e]
    n_blocks = -(-(nk + N_EXPERTS * (MOE_BLOCK - 1)) // MOE_BLOCK)
    slots = n_blocks * MOE_BLOCK
    slot_tok = jnp.full((slots,), N, jnp.int32).at[dest].set((order // TOP_K).astype(jnp.int32))
    slot_w = jnp.zeros((slots,), jnp.float32).at[dest].set(top_w.reshape(-1)[order])
    blk_e = jnp.minimum(jnp.searchsorted(pad_end, jnp.arange(n_blocks) * MOE_BLOCK, side='right'), N_EXPERTS - 1)
    x_pad = jnp.concatenate([xf, jnp.zeros((1, D), xf.dtype)], axis=0)

    def run(args):
        tok, e = args
        return expert_ffn(x_pad[tok], w_gu[e], b_gu[e], w_dn[e], b_dn[e])

    y_slots = lax.map(run, (slot_tok.reshape(n_blocks, MOE_BLOCK), blk_e))
    y_slots = y_slots.reshape(slots, D) * slot_w[:, None].astype(y_slots.dtype)
    y = jax.ops.segment_sum(y_slots, slot_tok, num_segments=N + 1)[:N]
    return y.reshape(B, T, D)


def layer_prompt(x, lw):
    (n1, w_in, b_in, w_conv, b_conv, w_mq, w_mk, g_mnorm, w_cmp, b_cmp,
     w_bm, w_bn, w_out, n2, w_r, b_r, w_gu, b_gu, w_dn, b_dn) = lw
    B, S, _ = x.shape
    f32 = jnp.float32
    z, q, kv, gates = project_in(rmsnorm(x, n1), w_in, b_in)
    h_m, conv_new, C, n, m = mlstm_branch(
        z, jnp.zeros((B, M_CONV - 1, M_WIDTH), z.dtype), jnp.zeros((B, M_HEADS, M_DQK, M_DV), f32),
        jnp.zeros((B, M_HEADS, M_DQK), f32), jnp.zeros((B, M_HEADS), f32), w_conv, b_conv, w_mq, w_mk, g_mnorm)
    h_n = nsa_prompt(q, kv, gates, w_cmp, b_cmp)
    x = x + merge(z, h_m, h_n, w_bm, w_bn, w_out)
    x = x + moe(rmsnorm(x, n2), w_r, b_r, w_gu, b_gu, w_dn, b_dn)
    return x, (kv[:, :, :4], kv[:, -min(WINDOW, S):, 4:], conv_new, C, n, m)


def layer_sample(x, cache_kv, layer, page_table, win_buf, conv_buf, C0, n0, m0, lw):
    (n1, w_in, b_in, w_conv, b_conv, w_mq, w_mk, g_mnorm, w_cmp, b_cmp,
     w_bm, w_bn, w_out, n2, w_r, b_r, w_gu, b_gu, w_dn, b_dn) = lw
    z, q, kv, gates = project_in(rmsnorm(x, n1), w_in, b_in)
    h_m, conv_new, C, n, m = mlstm_branch(z, conv_buf, C0, n0, m0, w_conv, b_conv, w_mq, w_mk, g_mnorm)
    h_n, win_new = nsa_sample(q, kv, gates, cache_kv, layer, page_table, win_buf, w_cmp, b_cmp)
    x = x + merge(z, h_m, h_n, w_bm, w_bn, w_out)
    x = x + moe(rmsnorm(x, n2), w_r, b_r, w_gu, b_gu, w_dn, b_dn)
    return x, (kv[:, :, :4], win_new, conv_new, C, n, m)


def setup_inputs(seed: int = 0) -> dict:
    key = jax.random.key(seed)
    keys = iter(jax.random.split(key, 48))
    f32 = jnp.float32

    def nrm(shape, scale):
        return jax.random.normal(next(keys), shape, f32) * scale

    def gain(shape):
        return 1.0 + nrm(shape, 0.02)

    n_pages = PAST_LEN // PAGE_SIZE
    n_pool = (5 * DEC_BATCH * n_pages + 3) // 4
    wb = min(WINDOW, PAST_LEN)
    x_prompt = nrm((BATCH, SEQ, D_MODEL), 1.0)
    x_sample = nrm((DEC_BATCH, DEC_SEQ, D_MODEL), 1.0)
    cache_kv = nrm((n_pool, PAGE_SIZE, DEPTH, 4, N_KV, N_HD), 1.0)
    cache_win_kv = nrm((DEPTH, DEC_BATCH, wb, 2, N_KV, N_HD), 1.0)
    state_conv = nrm((DEPTH, DEC_BATCH, M_CONV - 1, M_WIDTH), 1.0)
    state_C = nrm((DEPTH, DEC_BATCH, M_HEADS, M_DQK, M_DV), 0.1)
    state_n = nrm((DEPTH, DEC_BATCH, M_HEADS, M_DQK), 0.1)
    state_m = nrm((DEPTH, DEC_BATCH, M_HEADS), 1.0)
    perm = jax.random.permutation(next(keys), n_pool)
    page_table = perm[:DEC_BATCH * n_pages].reshape(DEC_BATCH, n_pages).astype(jnp.int32)
    b_in = nrm((DEPTH, N_IN), 0.02)
    b_in = b_in.at[:, OFF_MF:OFF_NQ].add(jax.random.uniform(next(keys), (DEPTH, M_HEADS), f32, 3.0, 6.0))
    return {
        'x_prompt': x_prompt,
        'x_sample': x_sample,
        'cache_kv': cache_kv,
        'cache_win_kv': cache_win_kv,
        'state_conv': state_conv,
        'state_C': state_C,
        'state_n': state_n,
        'state_m': state_m,
        'page_table': page_table,
        'norm1_g': gain((DEPTH, D_MODEL)),
        'w_in': nrm((DEPTH, D_MODEL, N_IN), D_MODEL ** -0.5),
        'b_in': b_in,
        'w_conv': nrm((DEPTH, M_CONV, M_WIDTH), M_CONV ** -0.5),
        'b_conv': nrm((DEPTH, M_WIDTH), 0.02),
        'w_mq': nrm((DEPTH, M_HEADS, M_DV, M_DQK), M_DV ** -0.5),
        'w_mk': nrm((DEPTH, M_HEADS, M_DV, M_DQK), M_DV ** -0.5),
        'g_mnorm': gain((DEPTH, M_HEADS, M_DV)),
        'w_cmp': nrm((DEPTH, 2, CMP_LEN, N_HD, N_HD), (CMP_LEN * N_HD) ** -0.5),
        'b_cmp': nrm((DEPTH, 2, N_HD), 0.02),
        'w_branch_m': nrm((DEPTH, M_WIDTH, D_MODEL), M_WIDTH ** -0.5),
        'w_branch_n': nrm((DEPTH, N_WIDTH, D_MODEL), N_WIDTH ** -0.5),
        'w_out': nrm((DEPTH, D_MODEL, D_MODEL), D_MODEL ** -0.5),
        'norm2_g': gain((DEPTH, D_MODEL)),
        'w_router': nrm((DEPTH, D_MODEL, N_EXPERTS), D_MODEL ** -0.5),
        'b_router': nrm((DEPTH, N_EXPERTS), 0.01),
        'w_gu': nrm((DEPTH, N_EXPERTS, D_MODEL, 2 * D_EXPERT), D_MODEL ** -0.5),
        'b_gu': nrm((DEPTH, N_EXPERTS, 2 * D_EXPERT), 0.02),
        'w_dn': nrm((DEPTH, N_EXPERTS, D_EXPERT, D_MODEL), D_EXPERT ** -0.5),
        'b_dn': nrm((DEPTH, N_EXPERTS, D_MODEL), 0.02),
        'normf_g': gain((D_MODEL,)),
    }


def reference(x_prompt, x_sample, cache_kv, cache_win_kv, state_conv, state_C, state_n, state_m, page_table,
              norm1_g, w_in, b_in, w_conv, b_conv, w_mq, w_mk, g_mnorm, w_cmp, b_cmp,
              w_branch_m, w_branch_n, w_out, norm2_g, w_router, b_router, w_gu, b_gu, w_dn, b_dn, normf_g):
    xp, xs = x_prompt, x_sample
    st_p, st_s = [], []
    for l in range(DEPTH):
        lw = (norm1_g[l], w_in[l], b_in[l], w_conv[l], b_conv[l], w_mq[l], w_mk[l], g_mnorm[l], w_cmp[l], b_cmp[l],
              w_branch_m[l], w_branch_n[l], w_out[l], norm2_g[l], w_router[l], b_router[l], w_gu[l], b_gu[l],
              w_dn[l], b_dn[l])
        xp, sp = layer_prompt(xp, lw)
        xs, ss = layer_sample(xs, cache_kv, l, page_table, cache_win_kv[l], state_conv[l], state_C[l],
                              state_n[l], state_m[l], lw)
        st_p.append(sp)
        st_s.append(ss)
    y_prompt = rmsnorm(xp, normf_g)
    y_sample = rmsnorm(xs, normf_g)
    new_kv_p = jnp.stack([s[0] for s in st_p], axis=2)
    new_win_p = jnp.stack([s[1] for s in st_p])
    new_conv_p = jnp.stack([s[2] for s in st_p])
    new_C_p = jnp.stack([s[3] for s in st_p])
    new_n_p = jnp.stack([s[4] for s in st_p])
    new_m_p = jnp.stack([s[5] for s in st_p])
    new_kv_s = jnp.stack([s[0] for s in st_s], axis=2)
    new_win_s = jnp.stack([s[1] for s in st_s])
    new_conv_s = jnp.stack([s[2] for s in st_s])
    new_C_s = jnp.stack([s[3] for s in st_s])
    new_n_s = jnp.stack([s[4] for s in st_s])
    new_m_s = jnp.stack([s[5] for s in st_s])
    return (y_prompt, y_sample, new_kv_p, new_win_p, new_conv_p, new_C_p, new_n_p, new_m_p,
            new_kv_s, new_win_s, new_conv_s, new_C_s, new_n_s, new_m_s)
```

```python
import functools
import math

import jax
import jax.numpy as jnp
import numpy as np
from jax import lax
from jax.experimental import pallas as pl
from jax.experimental.pallas import tpu as pltpu

F32 = jnp.float32
BF16 = jnp.bfloat16

M_HEADS = 4
M_DV = 256
M_DQK = 128
M_WIDTH = M_HEADS * M_DV
M_CONV = 4
N_HEADS = 16
N_KV = 2
N_HD = 64
N_HPG = N_HEADS // N_KV
N_WIDTH = N_HEADS * N_HD
KV_W = N_KV * N_HD
CMP_LEN = 32
CMP_STRIDE = 16
SLC_BLOCK = 64
N_SELECT = 16
WINDOW = 512
FORCE_BONUS = 1000.0
N_EXPERTS = 32
TOP_K = 4
SWIGLU_LIMIT = 7.0
SWIGLU_ALPHA = 1.702
EPS = 1e-6
TINY = 1e-30
NEG_BIG = -1e30

LANES = 128
PK_MX = 0
PK_MV = 1024
PK_MO = 2048
PK_NQ = 3072
PK_NKV = 4096
PK_SMALL = PK_NKV + 6 * KV_W
PK_BG = 5120
PK_TOTAL = 7168
SM_IG = 0
SM_LF = M_HEADS
SM_NG = 2 * M_HEADS

VMEM_LIMIT = 56 * 1024 * 1024


def _cparams(sem):
    return pltpu.CompilerParams(dimension_semantics=sem, vmem_limit_bytes=VMEM_LIMIT)


def _log_sigmoid(x):
    return jnp.minimum(x, 0.0) - jnp.log1p(jnp.exp(-jnp.abs(x)))


def _dot(a, b):
    return jnp.dot(a, b, preferred_element_type=F32)


def _dot_nt(a, b):
    return lax.dot_general(a, b, (((1,), (1,)), ((), ())), preferred_element_type=F32)


def _dot_tn(a, b):
    return lax.dot_general(a, b, (((0,), (0,)), ((), ())), preferred_element_type=F32)


def _norm_matmul_kernel(x_ref, g_ref, w_ref, b_ref, o_ref, xn_ref):
    @pl.when(pl.program_id(1) == 0)
    def _():
        x = x_ref[...]
        ms = jnp.mean(x * x, axis=-1, keepdims=True)
        xn_ref[...] = (x * lax.rsqrt(ms + EPS) * g_ref[...]).astype(BF16)

    o_ref[...] = _dot(xn_ref[...], w_ref[...]) + b_ref[...]


def norm_matmul(x, g, w, b, tm, tn):
    n, d = x.shape
    nc = w.shape[1]
    return pl.pallas_call(
        _norm_matmul_kernel,
        grid=(n // tm, nc // tn),
        in_specs=[
            pl.BlockSpec((tm, d), lambda i, j: (i, 0)),
            pl.BlockSpec((1, d), lambda i, j: (0, 0)),
            pl.BlockSpec((d, tn), lambda i, j: (0, j)),
            pl.BlockSpec((1, tn), lambda i, j: (0, j)),
        ],
        out_specs=pl.BlockSpec((tm, tn), lambda i, j: (i, j)),
        out_shape=jax.ShapeDtypeStruct((n, nc), F32),
        scratch_shapes=[pltpu.VMEM((tm, d), BF16)],
        compiler_params=_cparams(("parallel", "arbitrary")),
        name="norm_matmul",
    )(x, g, w, b)


def _shift_rows(x, tail, d):
    rows = x.shape[0]
    xd = pltpu.roll(x, d, 0)
    td = pltpu.roll(tail, d, 0)
    head = jnp.where(lax.broadcasted_iota(jnp.int32, td.shape, 0) < d, td, xd[:8])
    if rows == 8:
        return head
    return jnp.concatenate([head, xd[8:]], axis=0)


def _cumsum_rows(x):
    n = x.shape[0]
    idx = lax.broadcasted_iota(jnp.int32, x.shape, 0)
    k = 1
    while k < n:
        x = x + jnp.where(idx >= k, pltpu.roll(x, k, 0), 0.0)
        k *= 2
    return x


def _cumsum_lanes(x, n):
    idx = lax.broadcasted_iota(jnp.int32, x.shape, 1)
    k = 1
    while k < n:
        x = x + jnp.where(idx >= k, pltpu.roll(x, k, 1), 0.0)
        k *= 2
    return x


def _mlstm_kernel(xm_ref, v_ref, o_ref, gc_ref, gr_ref, cbuf_ref, c0_ref, n0_ref, m0_ref,
                  wconv_ref, bconv_ref, wq_ref, wk_ref, gn_ref,
                  h_ref, cout_ref, nout_ref, mout_ref,
                  c_sc, n_sc, m_sc, tail_sc, *, chunk):
    L = chunk
    c = pl.program_id(1)

    @pl.when(c == 0)
    def _init():
        c_sc[...] = c0_ref[0]
        n_sc[...] = n0_ref[0]
        m_sc[...] = m0_ref[0]
        tail_sc[...] = cbuf_ref[0]

    x = xm_ref[...]
    tail = tail_sc[...]
    wc = wconv_ref[...]
    xc = x * wc[M_CONV - 1:M_CONV] + bconv_ref[...]
    for d in range(1, M_CONV):
        xc = xc + _shift_rows(x, tail, d) * wc[M_CONV - 1 - d:M_CONV - d]
    tail_sc[...] = x[L - 8:]
    xc = xc * jax.nn.sigmoid(xc)
    xcb = xc.astype(BF16)

    gc = gc_ref[...]
    gr = gr_ref[0]
    b_col = _cumsum_rows(_log_sigmoid(gc))
    b_row = _cumsum_lanes(_log_sigmoid(gr), L)
    causal = (lax.broadcasted_iota(jnp.int32, (L, L), 1) <= lax.broadcasted_iota(jnp.int32, (L, L), 0))

    vb = v_ref[...].astype(BF16)
    og = jax.nn.sigmoid(o_ref[...])
    gn = gn_ref[...]
    m_all = m_sc[...]
    scale = M_DQK ** -0.5
    m_new_list = []
    for h in range(M_HEADS):
        hs = slice(h * M_DV, (h + 1) * M_DV)
        q = _dot(xcb[:, hs], wq_ref[h])
        k = _dot(xcb[:, hs], wk_ref[h]) * scale
        qb = q.astype(BF16)
        m_prev = m_all[:, h:h + 1]
        bc = b_col[:, SM_LF + h:SM_LF + h + 1]
        igc = gc[:, SM_IG + h:SM_IG + h + 1]
        br = b_row[SM_LF + h:SM_LF + h + 1, :L]
        igr = gr[SM_IG + h:SM_IG + h + 1, :L]
        a = bc + m_prev
        dmat = jnp.where(causal, bc - br + igr, -jnp.inf)
        mt = jnp.maximum(a, jnp.max(dmat, axis=1, keepdims=True))
        w_intra = jnp.exp(dmat - mt)
        w_inter = jnp.exp(a - mt)
        qk = _dot_nt(qb, k.astype(BF16)) * w_intra
        cmat = c_sc[h]
        nrow = n_sc[h:h + 1, :]
        vh = vb[:, hs]
        num = _dot(qk.astype(BF16), vh) + w_inter * _dot(qb, cmat.astype(BF16))
        den = jnp.sum(qk, axis=1, keepdims=True) + w_inter * jnp.sum(q * nrow, axis=1, keepdims=True)
        hh = num / jnp.maximum(jnp.abs(den), jnp.exp(-mt))
        m_new = mt[L - 1:L, :]
        w_end = jnp.exp(bc[L - 1:L, :] - bc + igc - m_new)
        decay = jnp.exp(a[L - 1:L, :] - m_new)
        kw = k * w_end
        c_sc[h] = decay * cmat + _dot_tn(kw.astype(BF16), vh)
        n_sc[h:h + 1, :] = decay * nrow + jnp.sum(kw, axis=0, keepdims=True)
        m_new_list.append(m_new)
        hn = hh * lax.rsqrt(jnp.mean(hh * hh, axis=-1, keepdims=True) + EPS) * gn[:, hs]
        h_ref[0, :, hs] = (hn * og[:, hs]).astype(h_ref.dtype)
    m_sc[...] = jnp.concatenate(m_new_list, axis=1)

    @pl.when(c == pl.num_programs(1) - 1)
    def _fin():
        cout_ref[0] = c_sc[...]
        nout_ref[0] = n_sc[...]
        mout_ref[0] = m_sc[...]


def mlstm(z, g_rows, row0, bsz, tlen, chunk, conv_buf8, c0, n0, m0, w_conv, b_conv, wq, wk, gn):
    L = chunk
    nc = tlen // L
    rb0 = row0 // L
    lr = g_rows.shape[2]
    grl = L if nc > 1 else lr

    def zspec(col):
        return pl.BlockSpec((L, 1024), lambda b, c: (rb0 + b * nc + c, col))

    kern = functools.partial(_mlstm_kernel, chunk=L)
    return pl.pallas_call(
        kern,
        grid=(bsz, nc),
        in_specs=[
            zspec(PK_MX // 1024), zspec(PK_MV // 1024), zspec(PK_MO // 1024),
            pl.BlockSpec((L, LANES), lambda b, c: (rb0 + b * nc + c, PK_SMALL // LANES)),
            pl.BlockSpec((1, 8, grl), lambda b, c: (b, 0, c)),
            pl.BlockSpec((1, 8, M_WIDTH), lambda b, c: (b, 0, 0)),
            pl.BlockSpec((1, M_HEADS, M_DQK, M_DV), lambda b, c: (b, 0, 0, 0)),
            pl.BlockSpec((1, M_HEADS, M_DQK), lambda b, c: (b, 0, 0)),
            pl.BlockSpec((1, 1, M_HEADS), lambda b, c: (b, 0, 0)),
            pl.BlockSpec((M_CONV, M_WIDTH), lambda b, c: (0, 0)),
            pl.BlockSpec((1, M_WIDTH), lambda b, c: (0, 0)),
            pl.BlockSpec((M_HEADS, M_DV, M_DQK), lambda b, c: (0, 0, 0)),
            pl.BlockSpec((M_HEADS, M_DV, M_DQK), lambda b, c: (0, 0, 0)),
            pl.BlockSpec((1, M_WIDTH), lambda b, c: (0, 0)),
        ],
        out_specs=[
            pl.BlockSpec((1, L, M_WIDTH), lambda b, c: (b, c, 0)),
            pl.BlockSpec((1, M_HEADS, M_DQK, M_DV), lambda b, c: (b, 0, 0, 0)),
            pl.BlockSpec((1, M_HEADS, M_DQK), lambda b, c: (b, 0, 0)),
            pl.BlockSpec((1, 1, M_HEADS), lambda b, c: (b, 0, 0)),
        ],
        out_shape=[
            jax.ShapeDtypeStruct((bsz, tlen, M_WIDTH), BF16),
            jax.ShapeDtypeStruct((bsz, M_HEADS, M_DQK, M_DV), F32),
            jax.ShapeDtypeStruct((bsz, M_HEADS, M_DQK), F32),
            jax.ShapeDtypeStruct((bsz, 1, M_HEADS), F32),
        ],
        scratch_shapes=[
            pltpu.VMEM((M_HEADS, M_DQK, M_DV), F32),
            pltpu.VMEM((M_HEADS, M_DQK), F32),
            pltpu.VMEM((1, M_HEADS), F32),
            pltpu.VMEM((8, M_WIDTH), F32),
        ],
        compiler_params=_cparams(("parallel", "arbitrary")),
        name="mlstm",
    )(z, z, z, z, g_rows, conv_buf8, c0, n0, m0, w_conv, b_conv, wq, wk, gn)


OFF_MX = 0
OFF_MV = OFF_MX + M_WIDTH
OFF_MO = OFF_MV + M_WIDTH
OFF_MI = OFF_MO + M_WIDTH
OFF_MF = OFF_MI + M_HEADS
OFF_NQ = OFF_MF + M_HEADS
OFF_NKV = OFF_NQ + N_WIDTH
OFF_NG = OFF_NKV + 6 * KV_W
OFF_BG = OFF_NG + 3 * N_HEADS


def _pack_cols(a):
    lead = a.shape[:-1]
    d_model = a.shape[-1] - OFF_BG
    small = jnp.concatenate([a[..., OFF_MI:OFF_NQ], a[..., OFF_NG:OFF_BG]], axis=-1)
    small = jnp.concatenate([small, jnp.zeros(lead + (LANES - small.shape[-1],), a.dtype)], axis=-1)
    pad = jnp.zeros(lead + (PK_BG - PK_SMALL - LANES,), a.dtype)
    assert d_model == PK_TOTAL - PK_BG
    return jnp.concatenate([a[..., OFF_MX:OFF_MI], a[..., OFF_NQ:OFF_NKV], a[..., OFF_NKV:OFF_NG],
                            small, pad, a[..., OFF_BG:]], axis=-1)


def _dup_halves(x, lane):
    r = pltpu.roll(x, N_HD, 1)
    lo = lane < N_HD
    return jnp.where(lo, x, r), jnp.where(lo, r, x)


def _masked_softmax_rows(s, mask):
    s = jnp.where(mask, s, -jnp.inf)
    mx = jnp.max(s, axis=1, keepdims=True)
    mx = jnp.where(mx > -jnp.inf, mx, 0.0)
    p = jnp.exp(s - mx)
    return p / jnp.maximum(jnp.sum(p, axis=1, keepdims=True), TINY)


def _flash_pass(qe, qo, k_ref, v_ref, g, j_lo, j_hi, tk, bias_fn):
    rows = qe.shape[0]

    def body(j, carry):
        k0 = pl.multiple_of(j * tk, tk)
        kt = k_ref[g, pl.ds(k0, tk), :]
        vt = v_ref[g, pl.ds(k0, tk), :]
        bias = bias_fn(k0)

        def upd(qx, m, l, a):
            s = _dot_nt(qx, kt) + bias
            mn = jnp.maximum(m, jnp.max(s, axis=1, keepdims=True))
            alpha = jnp.exp(m - mn)
            p = jnp.exp(s - mn)
            l = alpha * l + jnp.sum(p, axis=1, keepdims=True)
            a = alpha * a + _dot(p.astype(BF16), vt)
            return mn, l, a

        me, le, ae, mo, lo_, ao = carry
        me, le, ae = upd(qe, me, le, ae)
        mo, lo_, ao = upd(qo, mo, lo_, ao)
        return me, le, ae, mo, lo_, ao

    neg = jnp.full((rows, 1), -jnp.inf, F32)
    zl = jnp.zeros((rows, 1), F32)
    za = jnp.zeros((rows, LANES), F32)
    me, le, ae, mo, lo_, ao = lax.fori_loop(j_lo, j_hi, body, (neg, zl, za, neg, zl, za))
    lane = lax.broadcasted_iota(jnp.int32, (rows, LANES), 1)
    return jnp.where(lane < N_HD, ae / le, ao / lo_)


def _select_blocks(imp, qpos, ns):
    blk = lax.broadcasted_iota(jnp.int32, imp.shape, 1)
    cur = qpos // SLC_BLOCK
    allowed = blk <= cur
    forced = (blk == 0) | (blk == cur) | (blk == cur - 1)
    score = jnp.where(allowed, imp + jnp.where(forced, FORCE_BONUS, 0.0), -jnp.inf)
    cnt = jnp.zeros(imp.shape, F32)
    for i in range(ns):
        si = score[:, i:i + 1]
        ahead = (si > score) | ((si == score) & (blk > i))
        cnt = cnt + jnp.where(ahead, 1.0, 0.0)
    return jnp.where((cnt < float(N_SELECT)) & allowed, 1.0, 0.0)


def _stack_q(q_ref, g, rows):
    qs = jnp.concatenate([q_ref[:, (4 * g + p) * LANES:(4 * g + p + 1) * LANES] for p in range(N_HPG // 2)], axis=0)
    qs = qs * (N_HD ** -0.5)
    lane = lax.broadcasted_iota(jnp.int32, qs.shape, 1)
    qe = jnp.where(lane < N_HD, qs, 0.0).astype(BF16)
    qo = jnp.where(lane < N_HD, 0.0, qs).astype(BF16)
    return qe, qo


def _gate_pair(gs, g, p, j, lane):
    he = g * N_HPG + 2 * p
    ce = SM_NG + 3 * he + j
    co = ce + 3
    return jnp.where(lane < N_HD, gs[:, ce:ce + 1], gs[:, co:co + 1])


def _nsa_prompt_kernel(q_ref, kv_ref, wcmp_ref, bcmp_ref, mmat_ref, emat_ref, o_ref,
                       ck_sc, cv_sc, sk_sc, sv_sc, wk_sc, wv_sc, stage_sc, *, seq, tq, tk):
    i = pl.program_id(1)
    nseg = seq // CMP_STRIDE
    ns = seq // SLC_BLOCK
    npair = N_HPG // 2

    @pl.when(i == 0)
    def _prep():
        lane = lax.broadcasted_iota(jnp.int32, (nseg, LANES), 1)
        for c, dst in ((0, ck_sc), (1, cv_sc)):
            acc = jnp.zeros((nseg, 2 * LANES), F32)
            stage_sc[...] = kv_ref[:, c * LANES:(c + 1) * LANES]
            for s in range(CMP_STRIDE):
                xs = stage_sc[pl.ds(s, nseg, stride=CMP_STRIDE), :]
                acc = acc + _dot(xs.astype(BF16), wcmp_ref[c, s])
            blocks = acc[:, :LANES] + pltpu.roll(acc[:, LANES:], nseg - 1, 0) + bcmp_ref[c:c + 1, :]
            d0, d1 = _dup_halves(blocks, lane)
            dst[0] = d0.astype(BF16)
            dst[1] = d1.astype(BF16)
        lane_s = lax.broadcasted_iota(jnp.int32, (seq, LANES), 1)
        for off, dst in ((2, sk_sc), (3, sv_sc), (4, wk_sc), (5, wv_sc)):
            d0, d1 = _dup_halves(kv_ref[:, off * LANES:(off + 1) * LANES], lane_s)
            dst[0] = d0.astype(BF16)
            dst[1] = d1.astype(BF16)

    t0 = i * tq
    qpos = t0 + lax.broadcasted_iota(jnp.int32, (tq, 1), 0)
    lane = lax.broadcasted_iota(jnp.int32, (tq, LANES), 1)
    gs = jax.nn.sigmoid(kv_ref[pl.ds(pl.multiple_of(t0, tq), tq), 6 * LANES:7 * LANES])
    c_end = lax.broadcasted_iota(jnp.int32, (tq, nseg), 1) * CMP_STRIDE + (CMP_LEN - 1)
    cmask = jnp.concatenate([c_end <= qpos] * npair, axis=0)

    for g in range(N_KV):
        qe, qo = _stack_q(q_ref, g, tq)
        ckg = ck_sc[g]
        cvg = cv_sc[g]
        p_e = _masked_softmax_rows(_dot_nt(qe, ckg), cmask)
        p_o = _masked_softmax_rows(_dot_nt(qo, ckg), cmask)
        lane4 = lax.broadcasted_iota(jnp.int32, (npair * tq, LANES), 1)
        o_cmp = jnp.where(lane4 < N_HD, _dot(p_e.astype(BF16), cvg), _dot(p_o.astype(BF16), cvg))
        psum = p_e + p_o
        psum = sum(psum[p * tq:(p + 1) * tq] for p in range(npair))
        imp = jnp.dot(psum, mmat_ref[...], precision=lax.Precision.HIGHEST, preferred_element_type=F32)
        sel = _select_blocks(imp, qpos, ns)
        selb = ((sel - 1.0) * (-NEG_BIG)).astype(BF16)

        def slc_bias(k0):
            bias = _dot(selb, emat_ref[:, pl.ds(k0, tk)])
            kpos = k0 + lax.broadcasted_iota(jnp.int32, (tq, tk), 1)
            bias = jnp.where(kpos <= qpos, bias, NEG_BIG)
            return jnp.concatenate([bias] * npair, axis=0)

        o_slc = _flash_pass(qe, qo, sk_sc, sv_sc, g, 0, (t0 + tq + tk - 1) // tk, tk, slc_bias)

        def win_bias(k0):
            kpos = k0 + lax.broadcasted_iota(jnp.int32, (tq, tq), 1)
            dpos = qpos - kpos
            bias = jnp.where((dpos >= 0) & (dpos < WINDOW), 0.0, NEG_BIG)
            return jnp.concatenate([bias] * npair, axis=0)

        o_win = _flash_pass(qe, qo, wk_sc, wv_sc, g, jnp.maximum(i - WINDOW // tq, 0), i + 1, tq, win_bias)

        for p in range(npair):
            rs = slice(p * tq, (p + 1) * tq)
            o = (_gate_pair(gs, g, p, 0, lane) * o_cmp[rs] + _gate_pair(gs, g, p, 1, lane) * o_slc[rs]
                 + _gate_pair(gs, g, p, 2, lane) * o_win[rs])
            c0 = (g * npair + p) * LANES
            o_ref[:, c0:c0 + LANES] = o.astype(o_ref.dtype)


def _cmp_to_slc_matrix(nc, ns, rows, cols):
    c0 = np.arange(nc)[:, None] * CMP_STRIDE
    s0 = np.arange(ns)[None, :] * SLC_BLOCK
    ov = np.clip(np.minimum(c0 + CMP_LEN, s0 + SLC_BLOCK) - np.maximum(c0, s0), 0, None) / CMP_LEN
    out = np.zeros((rows, cols), np.float32)
    out[:nc, :ns] = ov
    return jnp.asarray(out)


def _block_expand_matrix(nkeys):
    e = (np.arange(nkeys)[None, :] // SLC_BLOCK) == np.arange(LANES)[:, None]
    return jnp.asarray(e.astype(np.float32), dtype=BF16)


def _pack_cmp_weights(w_cmp, b_cmp):
    r = CMP_LEN // CMP_STRIDE
    w = w_cmp.reshape(2, r, CMP_STRIDE, N_HD, N_HD)
    eye = jnp.eye(N_KV, dtype=w.dtype)
    bd = jnp.einsum('gh,crsde->crsgdhe', eye, w).reshape(2, r, CMP_STRIDE, KV_W, KV_W)
    wp = jnp.concatenate([bd[:, k] for k in range(r)], axis=-1)
    bp = jnp.concatenate([b_cmp] * N_KV, axis=-1)
    return wp.astype(BF16), bp


def nsa_prompt(z, bsz, seq, wcmp_p, bcmp_p, tq=128, tk=256):
    nseg = seq // CMP_STRIDE
    ns = seq // SLC_BLOCK
    nq = seq // tq
    tk = min(tk, seq)
    mmat = _cmp_to_slc_matrix(nseg - 1, ns, nseg, LANES)
    emat = _block_expand_matrix(seq)
    kern = functools.partial(_nsa_prompt_kernel, seq=seq, tq=tq, tk=tk)
    return pl.pallas_call(
        kern,
        grid=(bsz, nq),
        in_specs=[
            pl.BlockSpec((tq, 1024), lambda b, i: (b * nq + i, PK_NQ // 1024)),
            pl.BlockSpec((seq, 1024), lambda b, i: (b, PK_NKV // 1024)),
            pl.BlockSpec((2, CMP_STRIDE, KV_W, 2 * KV_W), lambda b, i: (0, 0, 0, 0)),
            pl.BlockSpec((2, KV_W), lambda b, i: (0, 0)),
            pl.BlockSpec((nseg, LANES), lambda b, i: (0, 0)),
            pl.BlockSpec((LANES, seq), lambda b, i: (0, 0)),
        ],
        out_specs=pl.BlockSpec((tq, N_WIDTH), lambda b, i: (b * nq + i, 0)),
        out_shape=jax.ShapeDtypeStruct((bsz * seq, N_WIDTH), BF16),
        scratch_shapes=[
            pltpu.VMEM((N_KV, nseg, LANES), BF16), pltpu.VMEM((N_KV, nseg, LANES), BF16),
            pltpu.VMEM((N_KV, seq, LANES), BF16), pltpu.VMEM((N_KV, seq, LANES), BF16),
            pltpu.VMEM((N_KV, seq, LANES), BF16), pltpu.VMEM((N_KV, seq, LANES), BF16),
            pltpu.VMEM((seq, LANES), F32),
        ],
        compiler_params=_cparams(("parallel", "arbitrary")),
        name="nsa_prompt",
    )(z, z, wcmp_p, bcmp_p, mmat, emat)


def _merge_kernel(x_ref, hm_ref, hn_ref, bgm_ref, bgn_ref, wbm_ref, wbn_ref, wout_ref, n2_ref, wr_ref, br_ref,
                  h_ref, xn2_ref, logit_ref):
    t = (jax.nn.sigmoid(bgm_ref[...]) * _dot(hm_ref[...], wbm_ref[...])
         + jax.nn.sigmoid(bgn_ref[...]) * _dot(hn_ref[...], wbn_ref[...]))
    h = x_ref[...] + _dot(t.astype(BF16), wout_ref[...])
    h_ref[...] = h
    xn2 = (h * lax.rsqrt(jnp.mean(h * h, axis=-1, keepdims=True) + EPS) * n2_ref[...]).astype(BF16)
    xn2_ref[...] = xn2
    logit_ref[...] = _dot(xn2, wr_ref[...]) + br_ref[...]


def merge(x, hm, hn, z, wbm, wbn, wout, n2, wr, br, tm):
    n, d = x.shape
    row = lambda i: (i, 0)
    fixed = lambda i: (0, 0)
    return pl.pallas_call(
        _merge_kernel,
        grid=(n // tm,),
        in_specs=[
            pl.BlockSpec((tm, d), row), pl.BlockSpec((tm, d), row), pl.BlockSpec((tm, d), row),
            pl.BlockSpec((tm, d), lambda i: (i, PK_BG // 1024)),
            pl.BlockSpec((tm, d), lambda i: (i, PK_BG // 1024 + 1)),
            pl.BlockSpec((d, d), fixed), pl.BlockSpec((d, d), fixed), pl.BlockSpec((d, d), fixed),
            pl.BlockSpec((1, d), fixed), pl.BlockSpec((d, LANES), fixed), pl.BlockSpec((1, LANES), fixed),
        ],
        out_specs=[pl.BlockSpec((tm, d), row), pl.BlockSpec((tm, d), row), pl.BlockSpec((tm, LANES), row)],
        out_shape=[jax.ShapeDtypeStruct((n, d), F32), jax.ShapeDtypeStruct((n, d), BF16),
                   jax.ShapeDtypeStruct((n, LANES), F32)],
        compiler_params=_cparams(("parallel",)),
        name="merge",
    )(x, hm, hn, z, z, wbm, wbn, wout, n2, wr, br)


def _expert_kernel(te_ref, nu_ref, x_ref, wgu_ref, bgu_ref, wdn_ref, bdn_ref, y_ref):
    @pl.when(pl.program_id(0) < nu_ref[0])
    def _():
        de = wdn_ref.shape[1]
        gu = _dot(x_ref[...], wgu_ref[0]) + bgu_ref[0]
        gate = jnp.minimum(gu[:, :de], SWIGLU_LIMIT)
        up = jnp.clip(gu[:, de:], -SWIGLU_LIMIT, SWIGLU_LIMIT)
        glu = gate * jax.nn.sigmoid(gate * SWIGLU_ALPHA)
        act = ((up + 1.0) * glu).astype(BF16)
        y_ref[...] = _dot(act, wdn_ref[0]) + bdn_ref[0]


def expert_ffn(x_sorted, tile_e, n_used, wgu, bgu, wdn, bdn, tm):
    slots, d = x_sorted.shape
    de = wdn.shape[1]
    n_tiles = slots // tm
    rows = lambda t, te, nu: (jnp.minimum(t, nu[0] - 1), 0)
    per_e = lambda t, te, nu: (te[t], 0, 0)
    grid_spec = pltpu.PrefetchScalarGridSpec(
        num_scalar_prefetch=2,
        grid=(n_tiles,),
        in_specs=[
            pl.BlockSpec((tm, d), rows),
            pl.BlockSpec((1, d, 2 * de), per_e), pl.BlockSpec((1, 1, 2 * de), per_e),
            pl.BlockSpec((1, de, d), per_e), pl.BlockSpec((1, 1, d), per_e),
        ],
        out_specs=pl.BlockSpec((tm, d), rows),
    )
    return pl.pallas_call(
        _expert_kernel,
        grid_spec=grid_spec,
        out_shape=jax.ShapeDtypeStruct((slots, d), F32),
        compiler_params=_cparams(("arbitrary",)),
        name="expert_ffn",
    )(tile_e, n_used, x_sorted, wgu, bgu, wdn, bdn)


def _combine_kernel(h_ref, y_ref, w_ref, g_ref, o_ref):
    d = h_ref.shape[1]
    w = w_ref[...]
    acc = h_ref[...]
    moe = w[:, 0:1] * y_ref[:, 0:d]
    for k in range(1, TOP_K):
        moe = moe + w[:, k:k + 1] * y_ref[:, k * d:(k + 1) * d]
    acc = acc + moe
    o_ref[...] = acc * lax.rsqrt(jnp.mean(acc * acc, axis=-1, keepdims=True) + EPS) * g_ref[...]


def combine(h, y4, w4, g, tm):
    n, d = h.shape
    return pl.pallas_call(
        _combine_kernel,
        grid=(n // tm,),
        in_specs=[
            pl.BlockSpec((tm, d), lambda i: (i, 0)),
            pl.BlockSpec((tm, TOP_K * d), lambda i: (i, 0)),
            pl.BlockSpec((tm, TOP_K), lambda i: (i, 0)),
            pl.BlockSpec((1, d), lambda i: (0, 0)),
        ],
        out_specs=pl.BlockSpec((tm, d), lambda i: (i, 0)),
        out_shape=jax.ShapeDtypeStruct((n, d), F32),
        compiler_params=_cparams(("parallel",)),
        name="combine",
    )(h, y4, w4, g)


def _route(logits, tm):
    n = logits.shape[0]
    top_v, top_e = lax.top_k(logits, TOP_K)
    top_w = jax.nn.softmax(top_v, axis=-1)
    nk = n * TOP_K
    flat_e = top_e.reshape(-1)
    order = jnp.argsort(flat_e)
    sorted_e = flat_e[order]
    counts = jnp.bincount(flat_e, length=N_EXPERTS)
    padded = (counts + tm - 1) // tm * tm
    pad_end = jnp.cumsum(padded)
    pad_start = pad_end - padded
    grp_start = jnp.cumsum(counts) - counts
    dest = (pad_start[sorted_e] + jnp.arange(nk) - grp_start[sorted_e]).astype(jnp.int32)
    n_tiles = -(-(nk + N_EXPERTS * (tm - 1)) // tm)
    slot_tok = jnp.zeros((n_tiles * tm,), jnp.int32).at[dest].set((order // TOP_K).astype(jnp.int32))
    inv_slot = jnp.zeros((nk,), jnp.int32).at[order].set(dest)
    tile_e = jnp.minimum(jnp.searchsorted(pad_end, jnp.arange(n_tiles) * tm, side='right'),
                         N_EXPERTS - 1).astype(jnp.int32)
    n_used = (pad_end[-1:] // tm).astype(jnp.int32)
    return top_w, slot_tok, inv_slot.reshape(n, TOP_K), tile_e, n_used


def _msoftmax(s, mask, axes):
    s = jnp.where(mask, s, -jnp.inf)
    mx = jnp.max(s, axis=axes, keepdims=True)
    mx = jnp.where(jnp.isfinite(mx), mx, 0.0)
    p = jnp.exp(s - mx)
    return p / jnp.maximum(jnp.sum(p, axis=axes, keepdims=True), TINY)


def _cmp_segments(rows, w):
    b, l = rows.shape[:2]
    r = CMP_LEN // CMP_STRIDE
    nseg = l // CMP_STRIDE
    seg = rows[:, :nseg * CMP_STRIDE].reshape(b, nseg, CMP_STRIDE, N_KV, N_HD)
    return jnp.einsum('bnsgd,rsde->rbnge', seg, w.reshape(r, CMP_STRIDE, N_HD, N_HD))


def _nsa_sample_jax(q, kv, gates, cache_kv, layer, page_table, win_buf, w_cmp, b_cmp):
    bsz, t = q.shape[:2]
    page = cache_kv.shape[1]
    past_len = page_table.shape[1] * page
    b_ix = jnp.arange(bsz)[:, None, None, None]
    g_ix = jnp.arange(N_KV)[None, :, None, None]
    r = CMP_LEN // CMP_STRIDE

    def cmp_branch(c):
        past = cache_kv[page_table, :, layer, c].reshape(bsz, past_len, N_KV, N_HD)
        segs = jnp.concatenate([_cmp_segments(past, w_cmp[c]), _cmp_segments(kv[:, :, c], w_cmp[c])], axis=2)
        nc = segs.shape[2] - r + 1
        return sum(segs[k, :, k:k + nc] for k in range(r)) + b_cmp[c]

    ck, cv = cmp_branch(0), cmp_branch(1)
    nnb = -(-t // SLC_BLOCK)
    newb = jnp.pad(kv[:, :, 2:4], [(0, 0), (0, nnb * SLC_BLOCK - t), (0, 0), (0, 0), (0, 0)])
    newb = newb.reshape(bsz, nnb, SLC_BLOCK, 2, N_KV, N_HD)
    npb = past_len // SLC_BLOCK
    bpp = page // SLC_BLOCK
    ns = npb + nnb
    nc = ck.shape[1]
    qpos = past_len + jnp.arange(t)
    qg = q.reshape(bsz, t, N_KV, N_HPG, N_HD) * (N_HD ** -0.5)
    c_end = jnp.arange(nc) * CMP_STRIDE + (CMP_LEN - 1)
    cmask = (c_end[None, :] <= qpos[:, None])[None, :, None, None, :]
    p_cmp = _msoftmax(jnp.einsum('btghd,bcgd->btghc', qg, ck), cmask, (-1,))
    o_cmp = jnp.einsum('btghc,bcgd->btghd', p_cmp, cv)
    imp = jnp.einsum('btgc,cn->btgn', jnp.sum(p_cmp, axis=3), _cmp_to_slc_matrix(nc, ns, nc, ns),
                     precision=lax.Precision.HIGHEST)
    blk = jnp.arange(ns)[None, :]
    cur = (qpos // SLC_BLOCK)[:, None]
    allowed = blk <= cur
    forced = (blk == 0) | (blk == cur) | (blk == cur - 1)
    score = jnp.where(allowed[None, :, None, :], imp + FORCE_BONUS * forced[None, :, None, :], -jnp.inf)
    top_s, idx = lax.top_k(score, min(N_SELECT, ns))
    idx = idx.transpose(0, 2, 1, 3)
    ok = jnp.isfinite(top_s).transpose(0, 2, 1, 3)
    in_past = (idx < npb)[..., None, None]
    jc = jnp.minimum(idx, npb - 1)
    pg = page_table[b_ix, jc // bpp]
    row = ((jc % bpp) * SLC_BLOCK)[..., None] + jnp.arange(SLC_BLOCK)
    loc = jnp.clip(idx - npb, 0, nnb - 1)

    def pick(c):
        past = cache_kv[pg[..., None], row, layer, 2 + c, g_ix[..., None]]
        new = newb[b_ix, loc, :, c, g_ix]
        return jnp.where(in_past, past, new)

    ksel, vsel = pick(0), pick(1)
    kpos = idx[..., None] * SLC_BLOCK + jnp.arange(SLC_BLOCK)
    smask = ((kpos <= qpos[None, None, :, None, None]) & ok[..., None])[:, :, :, None]
    s = jnp.einsum('btghd,bgtnkd->bgthnk', qg, ksel)
    p_slc = _msoftmax(s, smask, (-2, -1))
    o_slc = jnp.einsum('bgthnk,bgtnkd->btghd', p_slc, vsel)
    wb = win_buf.shape[1]
    win_all = jnp.concatenate([win_buf, kv[:, :, 4:]], axis=1)
    wpos = past_len - wb + jnp.arange(wb + t)
    dpos = qpos[:, None] - wpos[None, :]
    wmask = ((wpos[None, :] >= 0) & (dpos >= 0) & (dpos < WINDOW))[None, :, None, None, :]
    p_win = _msoftmax(jnp.einsum('btghd,bwgd->btghw', qg, win_all[:, :, 0]), wmask, (-1,))
    o_win = jnp.einsum('btghw,bwgd->btghd', p_win, win_all[:, :, 1])
    g5 = gates.reshape(bsz, t, N_KV, N_HPG, 3)
    o = g5[..., 0:1] * o_cmp + g5[..., 1:2] * o_slc + g5[..., 2:3] * o_win
    return o.reshape(bsz, t, N_WIDTH), win_all[:, t:]


ROW_TILE = 1024
MERGE_TILE = 512
MOE_TILE = 512
PROMPT_CHUNK = 128


def kernel(x_prompt, x_sample, cache_kv, cache_win_kv, state_conv, state_C, state_n, state_m, page_table,
           norm1_g, w_in, b_in, w_conv, b_conv, w_mq, w_mk, g_mnorm, w_cmp, b_cmp,
           w_branch_m, w_branch_n, w_out, norm2_g, w_router, b_router, w_gu, b_gu, w_dn, b_dn, normf_g):
    bp, sp, d = x_prompt.shape
    bs, ts, _ = x_sample.shape
    depth = w_in.shape[0]
    n_p, n_s = bp * sp, bs * ts
    n = n_p + n_s
    assert n % ROW_TILE == 0 and n % MERGE_TILE == 0 and sp % PROMPT_CHUNK == 0 and ts % 8 == 0 and ts >= M_CONV - 1

    x = jnp.concatenate([x_prompt.reshape(n_p, d), x_sample.reshape(n_s, d)], axis=0)
    st_p, st_s = [], []
    for l in range(depth):
        w_in_p = _pack_cols(w_in[l]).astype(BF16)
        b_in_p = _pack_cols(b_in[l][None])
        z = norm_matmul(x, norm1_g[l][None], w_in_p, b_in_p, ROW_TILE, 1024)

        wq, wk = w_mq[l].astype(BF16), w_mk[l].astype(BF16)
        gn = g_mnorm[l].reshape(1, M_WIDTH)
        bconv = b_conv[l][None]

        def gate_rows(rows, bsz, tlen):
            g = rows[:, PK_SMALL:PK_SMALL + 2 * M_HEADS].reshape(bsz, tlen, 2 * M_HEADS).transpose(0, 2, 1)
            if tlen < LANES:
                g = jnp.pad(g, ((0, 0), (0, 0), (0, LANES - tlen)))
            return g

        zero = lambda *s: jnp.zeros(s, F32)
        hm_p, c_p, nn_p, m_p = mlstm(z, gate_rows(z[:n_p], bp, sp), 0, bp, sp, PROMPT_CHUNK,
                                     zero(bp, 8, M_WIDTH), zero(bp, M_HEADS, M_DQK, M_DV),
                                     zero(bp, M_HEADS, M_DQK), zero(bp, 1, M_HEADS),
                                     w_conv[l], bconv, wq, wk, gn)
        cbuf_s = jnp.pad(state_conv[l], ((0, 0), (8 - (M_CONV - 1), 0), (0, 0)))
        hm_s, c_s, nn_s, m_s = mlstm(z, gate_rows(z[n_p:], bs, ts), n_p, bs, ts, ts,
                                     cbuf_s, state_C[l], state_n[l], state_m[l][:, None, :],
                                     w_conv[l], bconv, wq, wk, gn)

        wcmp_p, bcmp_p = _pack_cmp_weights(w_cmp[l], b_cmp[l])
        hn_p = nsa_prompt(z, bp, sp, wcmp_p, bcmp_p)
        zs = z[n_p:]
        q_s = zs[:, PK_NQ:PK_NQ + N_WIDTH].reshape(bs, ts, N_HEADS, N_HD)
        kv_s = zs[:, PK_NKV:PK_NKV + 6 * KV_W].reshape(bs, ts, 6, N_KV, N_HD)
        ng_s = jax.nn.sigmoid(zs[:, PK_SMALL + SM_NG:PK_SMALL + SM_NG + 3 * N_HEADS]).reshape(bs, ts, N_HEADS, 3)
        hn_s, win_s = _nsa_sample_jax(q_s, kv_s, ng_s, cache_kv, l, page_table, cache_win_kv[l], w_cmp[l], b_cmp[l])

        hm = jnp.concatenate([hm_p.reshape(n_p, M_WIDTH), hm_s.reshape(n_s, M_WIDTH)], axis=0)
        hn = jnp.concatenate([hn_p, hn_s.reshape(n_s, N_WIDTH).astype(BF16)], axis=0)
        wr = jnp.pad(w_router[l], ((0, 0), (0, LANES - N_EXPERTS))).astype(BF16)
        br = jnp.pad(b_router[l], (0, LANES - N_EXPERTS))[None]
        h, xn2, logits = merge(x, hm, hn, z, w_branch_m[l].astype(BF16), w_branch_n[l].astype(BF16),
                               w_out[l].astype(BF16), norm2_g[l][None], wr, br, MERGE_TILE)

        top_w, slot_tok, inv_slot, tile_e, n_used = _route(logits[:, :N_EXPERTS], MOE_TILE)
        x_sorted = jnp.take(xn2, slot_tok, axis=0)
        y_slots = expert_ffn(x_sorted, tile_e, n_used, w_gu[l].astype(BF16), b_gu[l][:, None, :],
                             w_dn[l].astype(BF16), b_dn[l][:, None, :], MOE_TILE)
        y4 = jnp.take(y_slots, inv_slot.reshape(-1), axis=0).reshape(n, TOP_K * d)
        last = l == depth - 1
        gfin = normf_g[None] if last else jnp.ones((1, d), F32)
        assert last, "only DEPTH == 1 is supported"
        y = combine(h, y4, top_w, gfin, ROW_TILE)

        zp_, zs_ = z[:n_p], z[n_p:]
        kv_p = zp_[:, PK_NKV:PK_NKV + 6 * KV_W].reshape(bp, sp, 6, N_KV, N_HD)
        wlen = min(WINDOW, sp)
        st_p.append((kv_p[:, :, :4], kv_p[:, sp - wlen:, 4:], zp_[:, PK_MX:PK_MX + M_WIDTH].reshape(bp, sp, M_WIDTH)[:, sp - (M_CONV - 1):],
                     c_p, nn_p, m_p[:, 0]))
        xm_s = zs_[:, PK_MX:PK_MX + M_WIDTH].reshape(bs, ts, M_WIDTH)
        st_s.append((kv_s[:, :, :4], win_s, xm_s[:, ts - (M_CONV - 1):], c_s, nn_s, m_s[:, 0]))

    y_prompt = y[:n_p].reshape(bp, sp, d)
    y_sample = y[n_p:].reshape(bs, ts, d)
    stack = lambda sts, k, axis=0: jnp.stack([s[k] for s in sts], axis=axis)
    return (y_prompt, y_sample,
            stack(st_p, 0, 2), stack(st_p, 1), stack(st_p, 2), stack(st_p, 3), stack(st_p, 4), stack(st_p, 5),
            stack(st_s, 0, 2), stack(st_s, 1), stack(st_s, 2), stack(st_s, 3), stack(st_s, 4), stack(st_s, 5))
```

```python
import functools
import math

import jax
import jax.numpy as jnp
import numpy as np
from jax import lax
from jax.experimental import pallas as pl
from jax.experimental.pallas import tpu as pltpu

F32 = jnp.float32
BF16 = jnp.bfloat16

M_HEADS = 4
M_DV = 256
M_DQK = 128
M_WIDTH = M_HEADS * M_DV
M_CONV = 4
N_HEADS = 16
N_KV = 2
N_HD = 64
N_HPG = N_HEADS // N_KV
N_WIDTH = N_HEADS * N_HD
KV_W = N_KV * N_HD
CMP_LEN = 32
CMP_STRIDE = 16
SLC_BLOCK = 64
N_SELECT = 16
WINDOW = 512
FORCE_BONUS = 1000.0
N_EXPERTS = 32
TOP_K = 4
SWIGLU_LIMIT = 7.0
SWIGLU_ALPHA = 1.702
EPS = 1e-6
TINY = 1e-30
NEG_BIG = -1e30

LANES = 128
PK_MX = 0
PK_MV = 1024
PK_MO = 2048
PK_NQ = 3072
PK_NKV = 4096
PK_SMALL = PK_NKV + 6 * KV_W
PK_BG = 5120
PK_TOTAL = 7168
SM_IG = 0
SM_LF = M_HEADS
SM_NG = 2 * M_HEADS

VMEM_LIMIT = 56 * 1024 * 1024


def _cparams(sem):
    return pltpu.CompilerParams(dimension_semantics=sem, vmem_limit_bytes=VMEM_LIMIT)


def _log_sigmoid(x):
    return jnp.minimum(x, 0.0) - jnp.log1p(jnp.exp(-jnp.abs(x)))


def _dot(a, b):
    return jnp.dot(a, b, preferred_element_type=F32)


def _dot_nt(a, b):
    return lax.dot_general(a, b, (((1,), (1,)), ((), ())), preferred_element_type=F32)


def _dot_tn(a, b):
    return lax.dot_general(a, b, (((0,), (0,)), ((), ())), preferred_element_type=F32)


def _norm_matmul_kernel(x_ref, g_ref, w_ref, b_ref, o_ref, xn_ref):
    @pl.when(pl.program_id(1) == 0)
    def _():
        x = x_ref[...]
        ms = jnp.mean(x * x, axis=-1, keepdims=True)
        xn_ref[...] = (x * lax.rsqrt(ms + EPS) * g_ref[...]).astype(BF16)

    o_ref[...] = _dot(xn_ref[...], w_ref[...]) + b_ref[...]


def norm_matmul(x, g, w, b, tm, tn):
    n, d = x.shape
    nc = w.shape[1]
    return pl.pallas_call(
        _norm_matmul_kernel,
        grid=(n // tm, nc // tn),
        in_specs=[
            pl.BlockSpec((tm, d), lambda i, j: (i, 0)),
            pl.BlockSpec((1, d), lambda i, j: (0, 0)),
            pl.BlockSpec((d, tn), lambda i, j: (0, j)),
            pl.BlockSpec((1, tn), lambda i, j: (0, j)),
        ],
        out_specs=pl.BlockSpec((tm, tn), lambda i, j: (i, j)),
        out_shape=jax.ShapeDtypeStruct((n, nc), F32),
        scratch_shapes=[pltpu.VMEM((tm, d), BF16)],
        compiler_params=_cparams(("parallel", "arbitrary")),
        name="norm_matmul",
    )(x, g, w, b)


def _shift_rows(x, tail, d):
    rows = x.shape[0]
    xd = pltpu.roll(x, d, 0)
    td = pltpu.roll(tail, d, 0)
    head = jnp.where(lax.broadcasted_iota(jnp.int32, td.shape, 0) < d, td, xd[:8])
    if rows == 8:
        return head
    return jnp.concatenate([head, xd[8:]], axis=0)


def _cumsum_rows(x):
    n = x.shape[0]
    idx = lax.broadcasted_iota(jnp.int32, x.shape, 0)
    k = 1
    while k < n:
        x = x + jnp.where(idx >= k, pltpu.roll(x, k, 0), 0.0)
        k *= 2
    return x


def _cumsum_lanes(x, n):
    idx = lax.broadcasted_iota(jnp.int32, x.shape, 1)
    k = 1
    while k < n:
        x = x + jnp.where(idx >= k, pltpu.roll(x, k, 1), 0.0)
        k *= 2
    return x


def _mlstm_kernel(xm_ref, v_ref, o_ref, gc_ref, gr_ref, cbuf_ref, c0_ref, n0_ref, m0_ref,
                  wconv_ref, bconv_ref, wq_ref, wk_ref, gn_ref,
                  h_ref, cout_ref, nout_ref, mout_ref,
                  c_sc, n_sc, m_sc, tail_sc, *, chunk):
    L = chunk
    c = pl.program_id(1)

    @pl.when(c == 0)
    def _init():
        c_sc[...] = c0_ref[0]
        n_sc[...] = n0_ref[0]
        m_sc[...] = m0_ref[0]
        tail_sc[...] = cbuf_ref[0]

    x = xm_ref[...]
    tail = tail_sc[...]
    wc = wconv_ref[...]
    xc = x * wc[M_CONV - 1:M_CONV] + bconv_ref[...]
    for d in range(1, M_CONV):
        xc = xc + _shift_rows(x, tail, d) * wc[M_CONV - 1 - d:M_CONV - d]
    tail_sc[...] = x[L - 8:]
    xc = xc * jax.nn.sigmoid(xc)
    xcb = xc.astype(BF16)

    gc = gc_ref[...]
    gr = gr_ref[0]
    b_col = _cumsum_rows(_log_sigmoid(gc))
    b_row = _cumsum_lanes(_log_sigmoid(gr), L)
    causal = (lax.broadcasted_iota(jnp.int32, (L, L), 1) <= lax.broadcasted_iota(jnp.int32, (L, L), 0))

    vb = v_ref[...].astype(BF16)
    og = jax.nn.sigmoid(o_ref[...])
    gn = gn_ref[...]
    m_all = m_sc[...]
    scale = M_DQK ** -0.5
    m_new_list = []
    for h in range(M_HEADS):
        hs = slice(h * M_DV, (h + 1) * M_DV)
        q = _dot(xcb[:, hs], wq_ref[h])
        k = _dot(xcb[:, hs], wk_ref[h]) * scale
        qb = q.astype(BF16)
        m_prev = m_all[:, h:h + 1]
        bc = b_col[:, SM_LF + h:SM_LF + h + 1]
        igc = gc[:, SM_IG + h:SM_IG + h + 1]
        br = b_row[SM_LF + h:SM_LF + h + 1, :L]
        igr = gr[SM_IG + h:SM_IG + h + 1, :L]
        a = bc + m_prev
        dmat = jnp.where(causal, bc - br + igr, -jnp.inf)
        mt = jnp.maximum(a, jnp.max(dmat, axis=1, keepdims=True))
        w_intra = jnp.exp(dmat - mt)
        w_inter = jnp.exp(a - mt)
        qk = _dot_nt(qb, k.astype(BF16)) * w_intra
        cmat = c_sc[h]
        nrow = n_sc[h:h + 1, :]
        vh = vb[:, hs]
        num = _dot(qk.astype(BF16), vh) + w_inter * _dot(qb, cmat.astype(BF16))
        den = jnp.sum(qk, axis=1, keepdims=True) + w_inter * jnp.sum(q * nrow, axis=1, keepdims=True)
        hh = num / jnp.maximum(jnp.abs(den), jnp.exp(-mt))
        m_new = mt[L - 1:L, :]
        w_end = jnp.exp(bc[L - 1:L, :] - bc + igc - m_new)
        decay = jnp.exp(a[L - 1:L, :] - m_new)
        kw = k * w_end
        c_sc[h] = decay * cmat + _dot_tn(kw.astype(BF16), vh)
        n_sc[h:h + 1, :] = decay * nrow + jnp.sum(kw, axis=0, keepdims=True)
        m_new_list.append(m_new)
        hn = hh * lax.rsqrt(jnp.mean(hh * hh, axis=-1, keepdims=True) + EPS) * gn[:, hs]
        h_ref[0, :, hs] = (hn * og[:, hs]).astype(h_ref.dtype)
    m_sc[...] = jnp.concatenate(m_new_list, axis=1)

    @pl.when(c == pl.num_programs(1) - 1)
    def _fin():
        cout_ref[0] = c_sc[...]
        nout_ref[0] = n_sc[...]
        mout_ref[0] = m_sc[...]


def mlstm(z, g_rows, row0, bsz, tlen, chunk, conv_buf8, c0, n0, m0, w_conv, b_conv, wq, wk, gn):
    L = chunk
    nc = tlen // L
    rb0 = row0 // L
    lr = g_rows.shape[2]
    grl = L if nc > 1 else lr

    def zspec(col):
        return pl.BlockSpec((L, 1024), lambda b, c: (rb0 + b * nc + c, col))

    kern = functools.partial(_mlstm_kernel, chunk=L)
    return pl.pallas_call(
        kern,
        grid=(bsz, nc),
        in_specs=[
            zspec(PK_MX // 1024), zspec(PK_MV // 1024), zspec(PK_MO // 1024),
            pl.BlockSpec((L, LANES), lambda b, c: (rb0 + b * nc + c, PK_SMALL // LANES)),
            pl.BlockSpec((1, 8, grl), lambda b, c: (b, 0, c)),
            pl.BlockSpec((1, 8, M_WIDTH), lambda b, c: (b, 0, 0)),
            pl.BlockSpec((1, M_HEADS, M_DQK, M_DV), lambda b, c: (b, 0, 0, 0)),
            pl.BlockSpec((1, M_HEADS, M_DQK), lambda b, c: (b, 0, 0)),
            pl.BlockSpec((1, 1, M_HEADS), lambda b, c: (b, 0, 0)),
            pl.BlockSpec((M_CONV, M_WIDTH), lambda b, c: (0, 0)),
            pl.BlockSpec((1, M_WIDTH), lambda b, c: (0, 0)),
            pl.BlockSpec((M_HEADS, M_DV, M_DQK), lambda b, c: (0, 0, 0)),
            pl.BlockSpec((M_HEADS, M_DV, M_DQK), lambda b, c: (0, 0, 0)),
            pl.BlockSpec((1, M_WIDTH), lambda b, c: (0, 0)),
        ],
        out_specs=[
            pl.BlockSpec((1, L, M_WIDTH), lambda b, c: (b, c, 0)),
            pl.BlockSpec((1, M_HEADS, M_DQK, M_DV), lambda b, c: (b, 0, 0, 0)),
            pl.BlockSpec((1, M_HEADS, M_DQK), lambda b, c: (b, 0, 0)),
            pl.BlockSpec((1, 1, M_HEADS), lambda b, c: (b, 0, 0)),
        ],
        out_shape=[
            jax.ShapeDtypeStruct((bsz, tlen, M_WIDTH), BF16),
            jax.ShapeDtypeStruct((bsz, M_HEADS, M_DQK, M_DV), F32),
            jax.ShapeDtypeStruct((bsz, M_HEADS, M_DQK), F32),
            jax.ShapeDtypeStruct((bsz, 1, M_HEADS), F32),
        ],
        scratch_shapes=[
            pltpu.VMEM((M_HEADS, M_DQK, M_DV), F32),
            pltpu.VMEM((M_HEADS, M_DQK), F32),
            pltpu.VMEM((1, M_HEADS), F32),
            pltpu.VMEM((8, M_WIDTH), F32),
        ],
        compiler_params=_cparams(("parallel", "arbitrary")),
        name="mlstm",
    )(z, z, z, z, g_rows, conv_buf8, c0, n0, m0, w_conv, b_conv, wq, wk, gn)


OFF_MX = 0
OFF_MV = OFF_MX + M_WIDTH
OFF_MO = OFF_MV + M_WIDTH
OFF_MI = OFF_MO + M_WIDTH
OFF_MF = OFF_MI + M_HEADS
OFF_NQ = OFF_MF + M_HEADS
OFF_NKV = OFF_NQ + N_WIDTH
OFF_NG = OFF_NKV + 6 * KV_W
OFF_BG = OFF_NG + 3 * N_HEADS


def _pack_cols(a):
    lead = a.shape[:-1]
    d_model = a.shape[-1] - OFF_BG
    small = jnp.concatenate([a[..., OFF_MI:OFF_NQ], a[..., OFF_NG:OFF_BG]], axis=-1)
    small = jnp.concatenate([small, jnp.zeros(lead + (LANES - small.shape[-1],), a.dtype)], axis=-1)
    pad = jnp.zeros(lead + (PK_BG - PK_SMALL - LANES,), a.dtype)
    assert d_model == PK_TOTAL - PK_BG
    return jnp.concatenate([a[..., OFF_MX:OFF_MI], a[..., OFF_NQ:OFF_NKV], a[..., OFF_NKV:OFF_NG],
                            small, pad, a[..., OFF_BG:]], axis=-1)


def _dup_halves(x, lane):
    r = pltpu.roll(x, N_HD, 1)
    lo = lane < N_HD
    return jnp.where(lo, x, r), jnp.where(lo, r, x)


def _masked_softmax_rows(s, mask):
    s = jnp.where(mask, s, -jnp.inf)
    mx = jnp.max(s, axis=1, keepdims=True)
    mx = jnp.where(mx > -jnp.inf, mx, 0.0)
    p = jnp.exp(s - mx)
    return p / jnp.maximum(jnp.sum(p, axis=1, keepdims=True), TINY)


def _softmax_pv(s, v):
    m = jnp.max(s, axis=1, keepdims=True)
    p = jnp.exp(s - m)
    return _dot(p.astype(BF16), v) / jnp.sum(p, axis=1, keepdims=True)


def _slc_attend(qx, kx_ref, v_ref, g, s_sc, nfull, diag_bias, ck):
    rows = qx.shape[0]
    nl = ck // LANES
    lane_chunks = lambda s: [s[:, a * LANES:(a + 1) * LANES] for a in range(nl)]

    def scores(c):
        k0 = pl.multiple_of(c * ck, ck)
        return _dot_nt(qx, kx_ref[g, pl.ds(k0, ck), :])

    def max_pass(c, mvec):
        s = scores(c)
        s_sc[c] = s
        return functools.reduce(jnp.maximum, lane_chunks(s), mvec)

    mvec = lax.fori_loop(0, nfull, max_pass, jnp.full((rows, LANES), -jnp.inf, F32))
    sd = scores(nfull) + diag_bias
    mvec = functools.reduce(jnp.maximum, lane_chunks(sd), mvec)
    mb = jnp.broadcast_to(jnp.max(mvec, axis=1, keepdims=True), (rows, LANES))

    def probs(s, c):
        ps = [jnp.exp(x - mb) for x in lane_chunks(s)]
        k0 = pl.multiple_of(c * ck, ck)
        pv = _dot(jnp.concatenate(ps, axis=1).astype(BF16), v_ref[g, pl.ds(k0, ck), :])
        return functools.reduce(lambda x, y: x + y, ps), pv

    def exp_pass(c, carry):
        lp, pv = probs(s_sc[c], c)
        return carry[0] + lp, carry[1] + pv

    lvec, acc = lax.fori_loop(0, nfull, exp_pass, probs(sd, nfull))
    return acc / jnp.sum(lvec, axis=1, keepdims=True)


def _select_blocks(imp, qpos, ns):
    blk = lax.broadcasted_iota(jnp.int32, imp.shape, 1)
    cur = qpos // SLC_BLOCK
    allowed = blk <= cur
    forced = (blk == 0) | (blk == cur) | (blk == cur - 1)
    score = jnp.where(allowed, imp + jnp.where(forced, FORCE_BONUS, 0.0), -jnp.inf)
    cnt = jnp.zeros(imp.shape, F32)
    for i in range(ns):
        si = score[:, i:i + 1]
        ahead = (si > score) | ((si == score) & (blk > i))
        cnt = cnt + jnp.where(ahead, 1.0, 0.0)
    return jnp.where((cnt < float(N_SELECT)) & allowed, 1.0, 0.0)


def _stack_q(q_ref, g, rows):
    qs = jnp.concatenate([q_ref[:, (4 * g + p) * LANES:(4 * g + p + 1) * LANES] for p in range(N_HPG // 2)], axis=0)
    qs = qs * (N_HD ** -0.5)
    lane = lax.broadcasted_iota(jnp.int32, qs.shape, 1)
    qe = jnp.where(lane < N_HD, qs, 0.0).astype(BF16)
    qo = jnp.where(lane < N_HD, 0.0, qs).astype(BF16)
    return qe, qo


def _gate_pair(gs, g, p, j, lane):
    he = g * N_HPG + 2 * p
    ce = SM_NG + 3 * he + j
    co = ce + 3
    return jnp.where(lane < N_HD, gs[:, ce:ce + 1], gs[:, co:co + 1])


def _nsa_prompt_kernel(q_ref, kv_ref, wcmp_ref, bcmp_ref, mmat_ref, o_ref,
                       ck_sc, cv_sc, kx_sc, sv_sc, wk_sc, wv_sc, stage_sc, s_sc, *, seq, tq, ck, wn):
    i = pl.program_id(1)
    nseg = seq // CMP_STRIDE
    ns = seq // SLC_BLOCK
    npair = N_HPG // 2

    @pl.when(i == 0)
    def _prep():
        kr = lax.broadcasted_iota(jnp.int32, (seq, LANES), 0)
        kc = lax.broadcasted_iota(jnp.int32, (seq, LANES), 1)
        onehot = jnp.where(kr // SLC_BLOCK == kc, 1.0, 0.0).astype(BF16)
        for g in range(N_KV):
            kx_sc[g, :, LANES:] = onehot
        lane = lax.broadcasted_iota(jnp.int32, (nseg, LANES), 1)
        for c, dst in ((0, ck_sc), (1, cv_sc)):
            acc = jnp.zeros((nseg, 2 * LANES), F32)
            stage_sc[...] = kv_ref[:, c * LANES:(c + 1) * LANES]
            for s in range(CMP_STRIDE):
                xs = stage_sc[pl.ds(s, nseg, stride=CMP_STRIDE), :]
                acc = acc + _dot(xs.astype(BF16), wcmp_ref[c, s])
            blocks = acc[:, :LANES] + pltpu.roll(acc[:, LANES:], nseg - 1, 0) + bcmp_ref[c:c + 1, :]
            d0, d1 = _dup_halves(blocks, lane)
            dst[0] = d0.astype(BF16)
            dst[1] = d1.astype(BF16)
        lane_s = lax.broadcasted_iota(jnp.int32, (seq, LANES), 1)
        for off, dst in ((2, kx_sc), (3, sv_sc), (4, wk_sc), (5, wv_sc)):
            d0, d1 = _dup_halves(kv_ref[:, off * LANES:(off + 1) * LANES], lane_s)
            dst[0, :, 0:LANES] = d0.astype(BF16)
            dst[1, :, 0:LANES] = d1.astype(BF16)

    t0 = i * tq
    qpos = t0 + lax.broadcasted_iota(jnp.int32, (tq, 1), 0)
    lane = lax.broadcasted_iota(jnp.int32, (tq, LANES), 1)
    gs = jax.nn.sigmoid(kv_ref[pl.ds(pl.multiple_of(t0, tq), tq), 6 * LANES:7 * LANES])
    c_end = lax.broadcasted_iota(jnp.int32, (tq, nseg), 1) * CMP_STRIDE + (CMP_LEN - 1)
    cmask = jnp.concatenate([c_end <= qpos] * npair, axis=0)
    lane4 = lax.broadcasted_iota(jnp.int32, (npair * tq, LANES), 1)
    lo4 = lane4 < N_HD

    nfull = t0 // ck
    kpos_d = nfull * ck + lax.broadcasted_iota(jnp.int32, (tq, ck), 1)
    diag_bias = jnp.concatenate([jnp.where(kpos_d <= qpos, 0.0, NEG_BIG)] * npair, axis=0)
    ks = pl.multiple_of(jnp.clip(t0 + tq - wn, 0, seq - wn), LANES)
    dpos = qpos - (ks + lax.broadcasted_iota(jnp.int32, (tq, wn), 1))
    win_bias = jnp.concatenate([jnp.where((dpos >= 0) & (dpos < WINDOW), 0.0, NEG_BIG)] * npair, axis=0)

    for g in range(N_KV):
        qe, qo = _stack_q(q_ref, g, tq)
        ckg = ck_sc[g]
        cvg = cv_sc[g]
        p_e = _masked_softmax_rows(_dot_nt(qe, ckg), cmask)
        p_o = _masked_softmax_rows(_dot_nt(qo, ckg), cmask)
        o_cmp = jnp.where(lo4, _dot(p_e.astype(BF16), cvg), _dot(p_o.astype(BF16), cvg))
        psum = p_e + p_o
        psum = sum(psum[p * tq:(p + 1) * tq] for p in range(npair))
        imp = jnp.dot(psum, mmat_ref[...], precision=lax.Precision.HIGHEST, preferred_element_type=F32)
        sel = _select_blocks(imp, qpos, ns)
        selb = ((sel - 1.0) * (-NEG_BIG)).astype(BF16)
        selb4 = jnp.concatenate([selb] * npair, axis=0)

        o_slc = jnp.where(lo4,
                          _slc_attend(jnp.concatenate([qe, selb4], axis=1), kx_sc, sv_sc, g, s_sc, nfull, diag_bias, ck),
                          _slc_attend(jnp.concatenate([qo, selb4], axis=1), kx_sc, sv_sc, g, s_sc, nfull, diag_bias, ck))

        kw = wk_sc[g, pl.ds(ks, wn), :]
        vw = wv_sc[g, pl.ds(ks, wn), :]
        o_win = jnp.where(lo4, _softmax_pv(_dot_nt(qe, kw) + win_bias, vw), _softmax_pv(_dot_nt(qo, kw) + win_bias, vw))

        for p in range(npair):
            rs = slice(p * tq, (p + 1) * tq)
            o = (_gate_pair(gs, g, p, 0, lane) * o_cmp[rs] + _gate_pair(gs, g, p, 1, lane) * o_slc[rs]
                 + _gate_pair(gs, g, p, 2, lane) * o_win[rs])
            c0 = (g * npair + p) * LANES
            o_ref[:, c0:c0 + LANES] = o.astype(o_ref.dtype)


def _cmp_to_slc_matrix(nc, ns, rows, cols):
    c0 = np.arange(nc)[:, None] * CMP_STRIDE
    s0 = np.arange(ns)[None, :] * SLC_BLOCK
    ov = np.clip(np.minimum(c0 + CMP_LEN, s0 + SLC_BLOCK) - np.maximum(c0, s0), 0, None) / CMP_LEN
    out = np.zeros((rows, cols), np.float32)
    out[:nc, :ns] = ov
    return jnp.asarray(out)


def _pack_cmp_weights(w_cmp, b_cmp):
    r = CMP_LEN // CMP_STRIDE
    w = w_cmp.reshape(2, r, CMP_STRIDE, N_HD, N_HD)
    eye = jnp.eye(N_KV, dtype=w.dtype)
    bd = jnp.einsum('gh,crsde->crsgdhe', eye, w).reshape(2, r, CMP_STRIDE, KV_W, KV_W)
    wp = jnp.concatenate([bd[:, k] for k in range(r)], axis=-1)
    bp = jnp.concatenate([b_cmp] * N_KV, axis=-1)
    return wp.astype(BF16), bp


def nsa_prompt(z, bsz, seq, wcmp_p, bcmp_p, tq=128, ck=256):
    nseg = seq // CMP_STRIDE
    ns = seq // SLC_BLOCK
    nq = seq // tq
    ck = min(ck, seq)
    wn = min(WINDOW + tq, seq)
    assert seq % ck == 0 and ck % tq == 0 and ns <= LANES and seq % LANES == 0
    mmat = _cmp_to_slc_matrix(nseg - 1, ns, nseg, LANES)
    kern = functools.partial(_nsa_prompt_kernel, seq=seq, tq=tq, ck=ck, wn=wn)
    return pl.pallas_call(
        kern,
        grid=(bsz, nq),
        in_specs=[
            pl.BlockSpec((tq, 1024), lambda b, i: (b * nq + i, PK_NQ // 1024)),
            pl.BlockSpec((seq, 1024), lambda b, i: (b, PK_NKV // 1024)),
            pl.BlockSpec((2, CMP_STRIDE, KV_W, 2 * KV_W), lambda b, i: (0, 0, 0, 0)),
            pl.BlockSpec((2, KV_W), lambda b, i: (0, 0)),
            pl.BlockSpec((nseg, LANES), lambda b, i: (0, 0)),
        ],
        out_specs=pl.BlockSpec((tq, N_WIDTH), lambda b, i: (b * nq + i, 0)),
        out_shape=jax.ShapeDtypeStruct((bsz * seq, N_WIDTH), BF16),
        scratch_shapes=[
            pltpu.VMEM((N_KV, nseg, LANES), BF16), pltpu.VMEM((N_KV, nseg, LANES), BF16),
            pltpu.VMEM((N_KV, seq, 2 * LANES), BF16), pltpu.VMEM((N_KV, seq, LANES), BF16),
            pltpu.VMEM((N_KV, seq, LANES), BF16), pltpu.VMEM((N_KV, seq, LANES), BF16),
            pltpu.VMEM((seq, LANES), F32),
            pltpu.VMEM((seq // ck, (N_HPG // 2) * tq, ck), F32),
        ],
        compiler_params=_cparams(("parallel", "arbitrary")),
        name="nsa_prompt",
    )(z, z, wcmp_p, bcmp_p, mmat)


def _merge_kernel(x_ref, hm_ref, hn_ref, bgm_ref, bgn_ref, wbm_ref, wbn_ref, wout_ref, n2_ref, wr_ref, br_ref,
                  h_ref, xn2_ref, logit_ref):
    t = (jax.nn.sigmoid(bgm_ref[...]) * _dot(hm_ref[...], wbm_ref[...])
         + jax.nn.sigmoid(bgn_ref[...]) * _dot(hn_ref[...], wbn_ref[...]))
    h = x_ref[...] + _dot(t.astype(BF16), wout_ref[...])
    h_ref[...] = h
    xn2 = (h * lax.rsqrt(jnp.mean(h * h, axis=-1, keepdims=True) + EPS) * n2_ref[...]).astype(BF16)
    xn2_ref[...] = xn2
    logit_ref[...] = _dot(xn2, wr_ref[...]) + br_ref[...]


def merge(x, hm, hn, z, wbm, wbn, wout, n2, wr, br, tm):
    n, d = x.shape
    row = lambda i: (i, 0)
    fixed = lambda i: (0, 0)
    return pl.pallas_call(
        _merge_kernel,
        grid=(n // tm,),
        in_specs=[
            pl.BlockSpec((tm, d), row), pl.BlockSpec((tm, d), row), pl.BlockSpec((tm, d), row),
            pl.BlockSpec((tm, d), lambda i: (i, PK_BG // 1024)),
            pl.BlockSpec((tm, d), lambda i: (i, PK_BG // 1024 + 1)),
            pl.BlockSpec((d, d), fixed), pl.BlockSpec((d, d), fixed), pl.BlockSpec((d, d), fixed),
            pl.BlockSpec((1, d), fixed), pl.BlockSpec((d, LANES), fixed), pl.BlockSpec((1, LANES), fixed),
        ],
        out_specs=[pl.BlockSpec((tm, d), row), pl.BlockSpec((tm, d), row), pl.BlockSpec((tm, LANES), row)],
        out_shape=[jax.ShapeDtypeStruct((n, d), F32), jax.ShapeDtypeStruct((n, d), BF16),
                   jax.ShapeDtypeStruct((n, LANES), F32)],
        compiler_params=_cparams(("parallel",)),
        name="merge",
    )(x, hm, hn, z, z, wbm, wbn, wout, n2, wr, br)


def _expert_kernel(te_ref, nu_ref, x_ref, wgu_ref, bgu_ref, wdn_ref, bdn_ref, y_ref):
    @pl.when(pl.program_id(0) < nu_ref[0])
    def _():
        de = wdn_ref.shape[1]
        gu = _dot(x_ref[...], wgu_ref[0]) + bgu_ref[0]
        gate = jnp.minimum(gu[:, :de], SWIGLU_LIMIT)
        up = jnp.clip(gu[:, de:], -SWIGLU_LIMIT, SWIGLU_LIMIT)
        glu = gate * jax.nn.sigmoid(gate * SWIGLU_ALPHA)
        act = ((up + 1.0) * glu).astype(BF16)
        y_ref[...] = _dot(act, wdn_ref[0]) + bdn_ref[0]


def expert_ffn(x_sorted, tile_e, n_used, wgu, bgu, wdn, bdn, tm):
    slots, d = x_sorted.shape
    de = wdn.shape[1]
    n_tiles = slots // tm
    rows = lambda t, te, nu: (jnp.minimum(t, nu[0] - 1), 0)
    per_e = lambda t, te, nu: (te[t], 0, 0)
    grid_spec = pltpu.PrefetchScalarGridSpec(
        num_scalar_prefetch=2,
        grid=(n_tiles,),
        in_specs=[
            pl.BlockSpec((tm, d), rows),
            pl.BlockSpec((1, d, 2 * de), per_e), pl.BlockSpec((1, 1, 2 * de), per_e),
            pl.BlockSpec((1, de, d), per_e), pl.BlockSpec((1, 1, d), per_e),
        ],
        out_specs=pl.BlockSpec((tm, d), rows),
    )
    return pl.pallas_call(
        _expert_kernel,
        grid_spec=grid_spec,
        out_shape=jax.ShapeDtypeStruct((slots, d), F32),
        compiler_params=_cparams(("arbitrary",)),
        name="expert_ffn",
    )(tile_e, n_used, x_sorted, wgu, bgu, wdn, bdn)


def _combine_kernel(h_ref, *refs):
    y_refs = refs[:TOP_K]
    w_ref, g_ref, o_ref = refs[TOP_K:]
    w = w_ref[...]
    moe = w[:, 0:1] * y_refs[0][...]
    for k in range(1, TOP_K):
        moe = moe + w[:, k:k + 1] * y_refs[k][...]
    acc = h_ref[...] + moe
    o_ref[...] = acc * lax.rsqrt(jnp.mean(acc * acc, axis=-1, keepdims=True) + EPS) * g_ref[...]


def combine(h, y4, w4, g, tm):
    n, d = h.shape
    nb = n // tm
    return pl.pallas_call(
        _combine_kernel,
        grid=(nb,),
        in_specs=[pl.BlockSpec((tm, d), lambda i: (i, 0))]
        + [pl.BlockSpec((tm, d), functools.partial(lambda i, k: (k * nb + i, 0), k=k)) for k in range(TOP_K)]
        + [pl.BlockSpec((tm, TOP_K), lambda i: (i, 0)), pl.BlockSpec((1, d), lambda i: (0, 0))],
        out_specs=pl.BlockSpec((tm, d), lambda i: (i, 0)),
        out_shape=jax.ShapeDtypeStruct((n, d), F32),
        compiler_params=_cparams(("parallel",)),
        name="combine",
    )(h, *([y4] * TOP_K), w4, g)


def _route(logits, tm):
    n = logits.shape[0]
    top_v, top_e = lax.top_k(logits, TOP_K)
    top_w = jax.nn.softmax(top_v, axis=-1)
    nk = n * TOP_K
    flat_e = top_e.reshape(-1)
    order = jnp.argsort(flat_e)
    sorted_e = flat_e[order]
    counts = jnp.bincount(flat_e, length=N_EXPERTS)
    padded = (counts + tm - 1) // tm * tm
    pad_end = jnp.cumsum(padded)
    pad_start = pad_end - padded
    grp_start = jnp.cumsum(counts) - counts
    dest = (pad_start[sorted_e] + jnp.arange(nk) - grp_start[sorted_e]).astype(jnp.int32)
    n_tiles = -(-(nk + N_EXPERTS * (tm - 1)) // tm)
    tile_e = jnp.minimum(jnp.searchsorted(pad_end, jnp.arange(n_tiles) * tm, side='right'),
                         N_EXPERTS - 1).astype(jnp.int32)
    slot = jnp.arange(n_tiles * tm, dtype=jnp.int32)
    slot_e = jnp.repeat(tile_e, tm)
    src = jnp.clip(grp_start[slot_e] + slot - pad_start[slot_e], 0, nk - 1)
    slot_tok = (jnp.take(order, src, mode='clip') // TOP_K).astype(jnp.int32)
    inv_slot = jnp.take(dest, jnp.argsort(order), mode='clip')
    n_used = (pad_end[-1:] // tm).astype(jnp.int32)
    return top_w, slot_tok, inv_slot.reshape(n, TOP_K), tile_e, n_used


def _plain_softmax_parts(parts):
    ms = [jnp.where(m, s, NEG_BIG) if m is not None else s for s, m in parts]
    mx = functools.reduce(jnp.maximum, [jnp.max(s, axis=1, keepdims=True) for s in ms])
    ps = [jnp.exp(s - mx) for s in ms]
    den = functools.reduce(lambda a, b: a + b, [jnp.sum(p, axis=1, keepdims=True) for p in ps])
    return [p / den for p in ps]


def _nsa_sample_kernel(pt_ref, *refs, npages, pps, tlen, page, wlen):
    page_refs = refs[:pps]
    (q_ref, kvn_ref, win_ref, wc_ref, bc_ref, mmat_ref, o_ref,
     stk_sc, stv_sc, kx_sc, v_sc, s_sc) = refs[pps:]
    b = pl.program_id(0)
    j = pl.program_id(1)
    nsteps = npages // pps
    past = npages * page
    nseg = past // CMP_STRIDE
    nblk = past // SLC_BLOCK
    ns = nblk + 1
    rows = N_HEADS * tlen
    grows = N_HPG * tlen

    @pl.when((b == 0) & (j == 0))
    def _onehot():
        r = lax.broadcasted_iota(jnp.int32, (past, LANES), 0)
        c = lax.broadcasted_iota(jnp.int32, (past, LANES), 1)
        kx_sc[:, LANES:] = jnp.where(r // SLC_BLOCK == c, 1.0, 0.0).astype(BF16)

    for k in range(pps):
        r0 = pl.multiple_of((j * pps + k) * page, page)
        pg = page_refs[k]
        stk_sc[pl.ds(r0, page), :] = pg[0, :, 0:LANES]
        stv_sc[pl.ds(r0, page), :] = pg[0, :, LANES:2 * LANES]
        kx_sc[pl.ds(r0, page), 0:LANES] = pg[0, :, 2 * LANES:3 * LANES].astype(BF16)
        v_sc[pl.ds(r0, page), :] = pg[0, :, 3 * LANES:4 * LANES].astype(BF16)

    @pl.when(j == nsteps - 1)
    def _attend():
        lane = lax.broadcasted_iota(jnp.int32, (tlen, LANES), 1)
        lo = lane < N_HD

        def cmp_proj(st_sc, c):
            acc = jnp.zeros((nseg, 2 * LANES), F32)
            for sp in range(CMP_STRIDE // 2):
                xa = st_sc[pl.ds(2 * sp, nseg, stride=CMP_STRIDE), :]
                xb = st_sc[pl.ds(2 * sp + 1, nseg, stride=CMP_STRIDE), :]
                acc = acc + _dot(jnp.concatenate([xa, xb], axis=1).astype(BF16), wc_ref[c, sp])
            blocks = acc[:, :LANES] + pltpu.roll(acc[:, LANES:], nseg - 1, 0) + bc_ref[c:c + 1, :]
            return blocks.astype(BF16)

        ck = cmp_proj(stk_sc, 0)
        cv = cmp_proj(stv_sc, 1)

        qall = q_ref[...] * (N_HD ** -0.5)
        pieces = []
        for h in range(N_HEADS):
            slab = qall[:, (h // 2) * LANES:(h // 2 + 1) * LANES]
            g = h // N_HPG
            if (h % 2) != g:
                slab = pltpu.roll(slab, N_HD, 1)
            pieces.append(jnp.where(lo, slab, 0.0) if g == 0 else jnp.where(lo, 0.0, slab))
        qbd = jnp.concatenate(pieces, axis=0).astype(BF16)
        trow = lax.broadcasted_iota(jnp.int32, (rows, 1), 0) & (tlen - 1)
        qpos = past + trow
        t8 = lax.broadcasted_iota(jnp.int32, (tlen, 1), 0)

        kvn = kvn_ref[...]
        zpad = jnp.zeros((LANES - tlen, LANES), F32)
        newk = lambda off: jnp.concatenate([kvn[:, off * LANES:(off + 1) * LANES], zpad], axis=0).astype(BF16)
        new_lane = lax.broadcasted_iota(jnp.int32, (rows, LANES), 1)
        new_mask = new_lane <= trow

        c_end = lax.broadcasted_iota(jnp.int32, (rows, nseg), 1) * CMP_STRIDE + (CMP_LEN - 1)
        p_cmp = _masked_softmax_rows(_dot_nt(qbd, ck), c_end <= qpos)
        o_cmp = _dot(p_cmp.astype(BF16), cv)

        bias_rows, bias_new = [], []
        for g in range(N_KV):
            psum = sum(p_cmp[g * grows + h * tlen:g * grows + (h + 1) * tlen] for h in range(N_HPG))
            imp = jnp.dot(psum, mmat_ref[...], precision=lax.Precision.HIGHEST, preferred_element_type=F32)
            sel = _select_blocks(imp, past + t8, ns)
            selb = (sel - 1.0) * (-NEG_BIG)
            bias_rows += [selb[:, :LANES]] * N_HPG
            bias_new += [selb[:, nblk:nblk + 1]] * N_HPG
        qx = jnp.concatenate([qbd, jnp.concatenate(bias_rows, axis=0).astype(BF16)], axis=1)
        bias_new = jnp.concatenate(bias_new, axis=0)

        ckeys = min(past, 1024)
        mx = jnp.full((rows, 1), -jnp.inf, F32)
        for c0 in range(0, past, ckeys):
            s = _dot_nt(qx, kx_sc[c0:c0 + ckeys, :])
            s_sc[:, c0:c0 + ckeys] = s
            mx = jnp.maximum(mx, jnp.max(s, axis=1, keepdims=True))
        s_new = jnp.where(new_mask, _dot_nt(qbd, newk(2)) + bias_new, NEG_BIG)
        mx = jnp.maximum(mx, jnp.max(s_new, axis=1, keepdims=True))
        p_new = jnp.exp(s_new - mx)
        den = jnp.sum(p_new, axis=1, keepdims=True)
        acc = _dot(p_new.astype(BF16), newk(3))
        for c0 in range(0, past, ckeys):
            p = jnp.exp(s_sc[:, c0:c0 + ckeys] - mx)
            den = den + jnp.sum(p, axis=1, keepdims=True)
            acc = acc + _dot(p.astype(BF16), v_sc[c0:c0 + ckeys, :])
        o_slc = acc / den

        wr = lax.broadcasted_iota(jnp.int32, (rows, wlen), 1)
        dpos = trow + wlen - wr
        wk_old = win_ref[0, :, 0:LANES].astype(BF16)
        wv_old = win_ref[0, :, LANES:2 * LANES].astype(BF16)
        pw_old, pw_new = _plain_softmax_parts([(_dot_nt(qbd, wk_old), (dpos >= 0) & (dpos < WINDOW)),
                                               (_dot_nt(qbd, newk(4)), new_mask)])
        o_win = _dot(pw_old.astype(BF16), wv_old) + _dot(pw_new.astype(BF16), newk(5))

        gs = jax.nn.sigmoid(kvn[:, 6 * LANES:7 * LANES])
        for pr in range(N_HEADS // 2):
            halves = []
            for par in range(2):
                h = 2 * pr + par
                g = h // N_HPG
                rs = slice(h * tlen, (h + 1) * tlen)
                c0 = SM_NG + 3 * h
                o = (gs[:, c0:c0 + 1] * o_cmp[rs] + gs[:, c0 + 1:c0 + 2] * o_slc[rs]
                     + gs[:, c0 + 2:c0 + 3] * o_win[rs])
                halves.append(pltpu.roll(o, N_HD, 1) if par != g else o)
            o_ref[0, :, pr * LANES:(pr + 1) * LANES] = jnp.where(lo, halves[0], halves[1])


def nsa_sample(z, row0, page_table, cache3, layer, win2, wcmp_p, bcmp_p, tlen, pps=8):
    bsz, npages = page_table.shape
    page = cache3.shape[1]
    wlen = win2.shape[1]
    past = npages * page
    nseg = past // CMP_STRIDE
    nblk = past // SLC_BLOCK
    assert npages % pps == 0 and nblk <= LANES and tlen & (tlen - 1) == 0 and tlen <= SLC_BLOCK and wlen == WINDOW
    mmat = _cmp_to_slc_matrix(nseg - 1, nblk + 1, nseg, 2 * LANES)
    wc2 = wcmp_p.reshape(2, CMP_STRIDE // 2, 2 * KV_W, 2 * KV_W)
    rb0 = row0 // tlen
    kern = functools.partial(_nsa_sample_kernel, npages=npages, pps=pps, tlen=tlen, page=page, wlen=wlen)

    def page_spec(k):
        return pl.BlockSpec((1, page, 4 * KV_W), lambda b, j, pt: (pt[b, j * pps + k], 0, layer))

    grid_spec = pltpu.PrefetchScalarGridSpec(
        num_scalar_prefetch=1,
        grid=(bsz, npages // pps),
        in_specs=[page_spec(k) for k in range(pps)] + [
            pl.BlockSpec((tlen, 1024), lambda b, j, pt: (rb0 + b, PK_NQ // 1024)),
            pl.BlockSpec((tlen, 1024), lambda b, j, pt: (rb0 + b, PK_NKV // 1024)),
            pl.BlockSpec((1, wlen, 2 * KV_W), lambda b, j, pt: (b, 0, 0)),
            pl.BlockSpec((2, CMP_STRIDE // 2, 2 * KV_W, 2 * KV_W), lambda b, j, pt: (0, 0, 0, 0)),
            pl.BlockSpec((2, KV_W), lambda b, j, pt: (0, 0)),
            pl.BlockSpec((nseg, 2 * LANES), lambda b, j, pt: (0, 0)),
        ],
        out_specs=pl.BlockSpec((1, tlen, N_WIDTH), lambda b, j, pt: (b, 0, 0)),
        scratch_shapes=[
            pltpu.VMEM((past, LANES), F32), pltpu.VMEM((past, LANES), F32),
            pltpu.VMEM((past, 2 * LANES), BF16), pltpu.VMEM((past, LANES), BF16),
            pltpu.VMEM((N_HEADS * tlen, past), F32),
        ],
    )
    return pl.pallas_call(
        kern,
        grid_spec=grid_spec,
        out_shape=jax.ShapeDtypeStruct((bsz, tlen, N_WIDTH), F32),
        compiler_params=_cparams(("arbitrary", "arbitrary")),
        name="nsa_sample",
    )(page_table, *([cache3] * pps), z, z, win2, wc2, bcmp_p, mmat)


ROW_TILE = 1024
MERGE_TILE = 512
MOE_TILE = 512
PROMPT_CHUNK = 128


def kernel(x_prompt, x_sample, cache_kv, cache_win_kv, state_conv, state_C, state_n, state_m, page_table,
           norm1_g, w_in, b_in, w_conv, b_conv, w_mq, w_mk, g_mnorm, w_cmp, b_cmp,
           w_branch_m, w_branch_n, w_out, norm2_g, w_router, b_router, w_gu, b_gu, w_dn, b_dn, normf_g):
    bp, sp, d = x_prompt.shape
    bs, ts, _ = x_sample.shape
    depth = w_in.shape[0]
    n_p, n_s = bp * sp, bs * ts
    n = n_p + n_s
    assert n % ROW_TILE == 0 and n % MERGE_TILE == 0 and sp % PROMPT_CHUNK == 0 and ts % 8 == 0 and ts >= M_CONV - 1

    x = jnp.concatenate([x_prompt.reshape(n_p, d), x_sample.reshape(n_s, d)], axis=0)
    st_p, st_s = [], []
    for l in range(depth):
        w_in_p = _pack_cols(w_in[l]).astype(BF16)
        b_in_p = _pack_cols(b_in[l][None])
        z = norm_matmul(x, norm1_g[l][None], w_in_p, b_in_p, ROW_TILE, 1024)

        wq, wk = w_mq[l].astype(BF16), w_mk[l].astype(BF16)
        gn = g_mnorm[l].reshape(1, M_WIDTH)
        bconv = b_conv[l][None]

        def gate_rows(rows, bsz, tlen):
            g = rows[:, PK_SMALL:PK_SMALL + 2 * M_HEADS].reshape(bsz, tlen, 2 * M_HEADS).transpose(0, 2, 1)
            if tlen < LANES:
                g = jnp.pad(g, ((0, 0), (0, 0), (0, LANES - tlen)))
            return g

        zero = lambda *s: jnp.zeros(s, F32)
        hm_p, c_p, nn_p, m_p = mlstm(z, gate_rows(z[:n_p], bp, sp), 0, bp, sp, PROMPT_CHUNK,
                                     zero(bp, 8, M_WIDTH), zero(bp, M_HEADS, M_DQK, M_DV),
                                     zero(bp, M_HEADS, M_DQK), zero(bp, 1, M_HEADS),
                                     w_conv[l], bconv, wq, wk, gn)
        cbuf_s = jnp.pad(state_conv[l], ((0, 0), (8 - (M_CONV - 1), 0), (0, 0)))
        hm_s, c_s, nn_s, m_s = mlstm(z, gate_rows(z[n_p:], bs, ts), n_p, bs, ts, ts,
                                     cbuf_s, state_C[l], state_n[l], state_m[l][:, None, :],
                                     w_conv[l], bconv, wq, wk, gn)

        wcmp_p, bcmp_p = _pack_cmp_weights(w_cmp[l], b_cmp[l])
        hn_p = nsa_prompt(z, bp, sp, wcmp_p, bcmp_p)
        wlen_s = cache_win_kv.shape[2]
        win2 = cache_win_kv[l].reshape(bs, wlen_s, 2 * KV_W)
        cache3 = cache_kv.reshape(cache_kv.shape[0], cache_kv.shape[1], depth * 4 * KV_W)
        hn_s = nsa_sample(z, n_p, page_table, cache3, l, win2, wcmp_p, bcmp_p, ts)

        hm = jnp.concatenate([hm_p.reshape(n_p, M_WIDTH), hm_s.reshape(n_s, M_WIDTH)], axis=0)
        hn = jnp.concatenate([hn_p, hn_s.reshape(n_s, N_WIDTH).astype(BF16)], axis=0)
        wr = jnp.pad(w_router[l], ((0, 0), (0, LANES - N_EXPERTS))).astype(BF16)
        br = jnp.pad(b_router[l], (0, LANES - N_EXPERTS))[None]
        h, xn2, logits = merge(x, hm, hn, z, w_branch_m[l].astype(BF16), w_branch_n[l].astype(BF16),
                               w_out[l].astype(BF16), norm2_g[l][None], wr, br, MERGE_TILE)

        top_w, slot_tok, inv_slot, tile_e, n_used = _route(logits[:, :N_EXPERTS], MOE_TILE)
        x_sorted = jnp.take(xn2, slot_tok, axis=0, mode='clip')
        y_slots = expert_ffn(x_sorted, tile_e, n_used, w_gu[l].astype(BF16), b_gu[l][:, None, :],
                             w_dn[l].astype(BF16), b_dn[l][:, None, :], MOE_TILE)
        y4 = jnp.take(y_slots, inv_slot.T.reshape(-1), axis=0, mode='clip')
        assert l == depth - 1, "only DEPTH == 1 is supported (the final norm is fused into the combine step)"
        y = combine(h, y4, top_w, normf_g[None], ROW_TILE)

        z3p = z[:n_p].reshape(bp, sp, PK_TOTAL)
        z3s = z[n_p:].reshape(bs, ts, PK_TOTAL)
        wlen = min(WINDOW, sp)
        new_kv = lambda z3: z3[:, :, PK_NKV:PK_NKV + 4 * KV_W].reshape(z3.shape[0], z3.shape[1], 4, N_KV, N_HD)
        new_win = lambda z3: z3[:, :, PK_NKV + 4 * KV_W:PK_NKV + 6 * KV_W].reshape(z3.shape[0], z3.shape[1], 2, N_KV, N_HD)
        st_p.append((new_kv(z3p), new_win(z3p[:, sp - wlen:]), z3p[:, sp - (M_CONV - 1):, PK_MX:PK_MX + M_WIDTH],
                     c_p, nn_p, m_p[:, 0]))
        win_s = jnp.concatenate([cache_win_kv[l][:, ts:], new_win(z3s)], axis=1)
        st_s.append((new_kv(z3s), win_s, z3s[:, ts - (M_CONV - 1):, PK_MX:PK_MX + M_WIDTH], c_s, nn_s, m_s[:, 0]))

    y_prompt = y[:n_p].reshape(bp, sp, d)
    y_sample = y[n_p:].reshape(bs, ts, d)
    stack = lambda sts, k, axis=0: jnp.stack([s[k] for s in sts], axis=axis)
    return (y_prompt, y_sample,
            stack(st_p, 0, 2), stack(st_p, 1), stack(st_p, 2), stack(st_p, 3), stack(st_p, 4), stack(st_p, 5),
            stack(st_s, 0, 2), stack(st_s, 1), stack(st_s, 2), stack(st_s, 3), stack(st_s, 4), stack(st_s, 5))
```

```python
import functools
import math

import jax
import jax.numpy as jnp
import numpy as np
from jax import lax
from jax.experimental import pallas as pl
from jax.experimental.pallas import tpu as pltpu

F32 = jnp.float32
BF16 = jnp.bfloat16

M_HEADS = 4
M_DV = 256
M_DQK = 128
M_WIDTH = M_HEADS * M_DV
M_CONV = 4
N_HEADS = 16
N_KV = 2
N_HD = 64
N_HPG = N_HEADS // N_KV
N_WIDTH = N_HEADS * N_HD
KV_W = N_KV * N_HD
CMP_LEN = 32
CMP_STRIDE = 16
SLC_BLOCK = 64
N_SELECT = 16
WINDOW = 512
FORCE_BONUS = 1000.0
N_EXPERTS = 32
TOP_K = 4
SWIGLU_LIMIT = 7.0
SWIGLU_ALPHA = 1.702
EPS = 1e-6
TINY = 1e-30
NEG_BIG = -1e30

LANES = 128
PK_MX = 0
PK_MV = 1024
PK_MO = 2048
PK_NQ = 3072
PK_NKV = 4096
PK_SMALL = PK_NKV + 6 * KV_W
PK_BG = 5120
PK_TOTAL = 7168
SM_IG = 0
SM_LF = M_HEADS
SM_NG = 2 * M_HEADS

VMEM_LIMIT = 56 * 1024 * 1024


def _cparams(sem):
    return pltpu.CompilerParams(dimension_semantics=sem, vmem_limit_bytes=VMEM_LIMIT)


def _log_sigmoid(x):
    return jnp.minimum(x, 0.0) - jnp.log1p(jnp.exp(-jnp.abs(x)))


def _dot(a, b):
    return jnp.dot(a, b, preferred_element_type=F32)


def _dot_nt(a, b):
    return lax.dot_general(a, b, (((1,), (1,)), ((), ())), preferred_element_type=F32)


def _dot_tn(a, b):
    return lax.dot_general(a, b, (((0,), (0,)), ((), ())), preferred_element_type=F32)


def _norm_matmul_kernel(x_ref, g_ref, w_ref, b_ref, o_ref, xn_ref):
    @pl.when(pl.program_id(1) == 0)
    def _():
        x = x_ref[...]
        ms = jnp.mean(x * x, axis=-1, keepdims=True)
        xn_ref[...] = (x * lax.rsqrt(ms + EPS) * g_ref[...]).astype(BF16)

    o_ref[...] = _dot(xn_ref[...], w_ref[...]) + b_ref[...]


def norm_matmul(x, g, w, b, tm, tn):
    n, d = x.shape
    nc = w.shape[1]
    return pl.pallas_call(
        _norm_matmul_kernel,
        grid=(n // tm, nc // tn),
        in_specs=[
            pl.BlockSpec((tm, d), lambda i, j: (i, 0)),
            pl.BlockSpec((1, d), lambda i, j: (0, 0)),
            pl.BlockSpec((d, tn), lambda i, j: (0, j)),
            pl.BlockSpec((1, tn), lambda i, j: (0, j)),
        ],
        out_specs=pl.BlockSpec((tm, tn), lambda i, j: (i, j)),
        out_shape=jax.ShapeDtypeStruct((n, nc), F32),
        scratch_shapes=[pltpu.VMEM((tm, d), BF16)],
        compiler_params=_cparams(("parallel", "arbitrary")),
        name="norm_matmul",
    )(x, g, w, b)


def _shift_rows(x, tail, d):
    rows = x.shape[0]
    xd = pltpu.roll(x, d, 0)
    td = pltpu.roll(tail, d, 0)
    head = jnp.where(lax.broadcasted_iota(jnp.int32, td.shape, 0) < d, td, xd[:8])
    if rows == 8:
        return head
    return jnp.concatenate([head, xd[8:]], axis=0)


def _cumsum_rows(x):
    n = x.shape[0]
    idx = lax.broadcasted_iota(jnp.int32, x.shape, 0)
    k = 1
    while k < n:
        x = x + jnp.where(idx >= k, pltpu.roll(x, k, 0), 0.0)
        k *= 2
    return x


def _cumsum_lanes(x, n):
    idx = lax.broadcasted_iota(jnp.int32, x.shape, 1)
    k = 1
    while k < n:
        x = x + jnp.where(idx >= k, pltpu.roll(x, k, 1), 0.0)
        k *= 2
    return x


def _mlstm_kernel(xm_ref, v_ref, o_ref, gc_ref, gr_ref, cbuf_ref, c0_ref, n0_ref, m0_ref,
                  wconv_ref, bconv_ref, wq_ref, wk_ref, gn_ref,
                  h_ref, cout_ref, nout_ref, mout_ref, tout_ref,
                  c_sc, n_sc, m_sc, tail_sc, *, chunk):
    L = chunk
    c = pl.program_id(1)

    @pl.when(c == 0)
    def _init():
        c_sc[...] = c0_ref[0]
        n_sc[...] = n0_ref[0]
        m_sc[...] = m0_ref[0]
        tail_sc[...] = cbuf_ref[0]

    x = xm_ref[...]
    tail = tail_sc[...]
    wc = wconv_ref[...]
    xc = x * wc[M_CONV - 1:M_CONV] + bconv_ref[...]
    for d in range(1, M_CONV):
        xc = xc + _shift_rows(x, tail, d) * wc[M_CONV - 1 - d:M_CONV - d]
    tail_sc[...] = x[L - 8:]
    xc = xc * jax.nn.sigmoid(xc)
    xcb = xc.astype(BF16)

    gc = gc_ref[...]
    gr = gr_ref[0]
    b_col = _cumsum_rows(_log_sigmoid(gc))
    b_row = _cumsum_lanes(_log_sigmoid(gr), L)
    causal = (lax.broadcasted_iota(jnp.int32, (L, L), 1) <= lax.broadcasted_iota(jnp.int32, (L, L), 0))

    vb = v_ref[...].astype(BF16)
    og = jax.nn.sigmoid(o_ref[...])
    gn = gn_ref[...]
    m_all = m_sc[...]
    scale = M_DQK ** -0.5
    m_new_list = []
    for h in range(M_HEADS):
        hs = slice(h * M_DV, (h + 1) * M_DV)
        q = _dot(xcb[:, hs], wq_ref[h])
        k = _dot(xcb[:, hs], wk_ref[h]) * scale
        qb = q.astype(BF16)
        m_prev = m_all[:, h:h + 1]
        bc = b_col[:, SM_LF + h:SM_LF + h + 1]
        igc = gc[:, SM_IG + h:SM_IG + h + 1]
        br = b_row[SM_LF + h:SM_LF + h + 1, :L]
        igr = gr[SM_IG + h:SM_IG + h + 1, :L]
        a = bc + m_prev
        dmat = jnp.where(causal, bc - br + igr, -jnp.inf)
        mt = jnp.maximum(a, jnp.max(dmat, axis=1, keepdims=True))
        w_intra = jnp.exp(dmat - mt)
        w_inter = jnp.exp(a - mt)
        qk = _dot_nt(qb, k.astype(BF16)) * w_intra
        cmat = c_sc[h]
        nrow = n_sc[h:h + 1, :]
        vh = vb[:, hs]
        num = _dot(qk.astype(BF16), vh) + w_inter * _dot(qb, cmat.astype(BF16))
        den = jnp.sum(qk, axis=1, keepdims=True) + w_inter * jnp.sum(q * nrow, axis=1, keepdims=True)
        hh = num / jnp.maximum(jnp.abs(den), jnp.exp(-mt))
        m_new = mt[L - 1:L, :]
        w_end = jnp.exp(bc[L - 1:L, :] - bc + igc - m_new)
        decay = jnp.exp(a[L - 1:L, :] - m_new)
        kw = k * w_end
        c_sc[h] = decay * cmat + _dot_tn(kw.astype(BF16), vh)
        n_sc[h:h + 1, :] = decay * nrow + jnp.sum(kw, axis=0, keepdims=True)
        m_new_list.append(m_new)
        hn = hh * lax.rsqrt(jnp.mean(hh * hh, axis=-1, keepdims=True) + EPS) * gn[:, hs]
        h_ref[0, :, hs] = (hn * og[:, hs]).astype(h_ref.dtype)
    m_sc[...] = jnp.concatenate(m_new_list, axis=1)

    @pl.when(c == pl.num_programs(1) - 1)
    def _fin():
        cout_ref[0] = c_sc[...]
        nout_ref[0] = n_sc[...]
        mout_ref[0] = m_sc[...]
        tout_ref[0] = tail_sc[...]


def mlstm(z, g_rows, row0, bsz, tlen, chunk, conv_buf8, c0, n0, m0, w_conv, b_conv, wq, wk, gn):
    L = chunk
    nc = tlen // L
    rb0 = row0 // L
    lr = g_rows.shape[2]
    grl = L if nc > 1 else lr

    def zspec(col):
        return pl.BlockSpec((L, 1024), lambda b, c: (rb0 + b * nc + c, col))

    kern = functools.partial(_mlstm_kernel, chunk=L)
    return pl.pallas_call(
        kern,
        grid=(bsz, nc),
        in_specs=[
            zspec(PK_MX // 1024), zspec(PK_MV // 1024), zspec(PK_MO // 1024),
            pl.BlockSpec((L, LANES), lambda b, c: (rb0 + b * nc + c, PK_SMALL // LANES)),
            pl.BlockSpec((1, 8, grl), lambda b, c: (b, 0, c)),
            pl.BlockSpec((1, 8, M_WIDTH), lambda b, c: (b, 0, 0)),
            pl.BlockSpec((1, M_HEADS, M_DQK, M_DV), lambda b, c: (b, 0, 0, 0)),
            pl.BlockSpec((1, M_HEADS, M_DQK), lambda b, c: (b, 0, 0)),
            pl.BlockSpec((1, 1, M_HEADS), lambda b, c: (b, 0, 0)),
            pl.BlockSpec((M_CONV, M_WIDTH), lambda b, c: (0, 0)),
            pl.BlockSpec((1, M_WIDTH), lambda b, c: (0, 0)),
            pl.BlockSpec((M_HEADS, M_DV, M_DQK), lambda b, c: (0, 0, 0)),
            pl.BlockSpec((M_HEADS, M_DV, M_DQK), lambda b, c: (0, 0, 0)),
            pl.BlockSpec((1, M_WIDTH), lambda b, c: (0, 0)),
        ],
        out_specs=[
            pl.BlockSpec((1, L, M_WIDTH), lambda b, c: (b, c, 0)),
            pl.BlockSpec((1, M_HEADS, M_DQK, M_DV), lambda b, c: (b, 0, 0, 0)),
            pl.BlockSpec((1, M_HEADS, M_DQK), lambda b, c: (b, 0, 0)),
            pl.BlockSpec((1, 1, M_HEADS), lambda b, c: (b, 0, 0)),
            pl.BlockSpec((1, 8, M_WIDTH), lambda b, c: (b, 0, 0)),
        ],
        out_shape=[
            jax.ShapeDtypeStruct((bsz, tlen, M_WIDTH), BF16),
            jax.ShapeDtypeStruct((bsz, M_HEADS, M_DQK, M_DV), F32),
            jax.ShapeDtypeStruct((bsz, M_HEADS, M_DQK), F32),
            jax.ShapeDtypeStruct((bsz, 1, M_HEADS), F32),
            jax.ShapeDtypeStruct((bsz, 8, M_WIDTH), F32),
        ],
        scratch_shapes=[
            pltpu.VMEM((M_HEADS, M_DQK, M_DV), F32),
            pltpu.VMEM((M_HEADS, M_DQK), F32),
            pltpu.VMEM((1, M_HEADS), F32),
            pltpu.VMEM((8, M_WIDTH), F32),
        ],
        compiler_params=_cparams(("parallel", "arbitrary")),
        name="mlstm",
    )(z, z, z, z, g_rows, conv_buf8, c0, n0, m0, w_conv, b_conv, wq, wk, gn)


OFF_MX = 0
OFF_MV = OFF_MX + M_WIDTH
OFF_MO = OFF_MV + M_WIDTH
OFF_MI = OFF_MO + M_WIDTH
OFF_MF = OFF_MI + M_HEADS
OFF_NQ = OFF_MF + M_HEADS
OFF_NKV = OFF_NQ + N_WIDTH
OFF_NG = OFF_NKV + 6 * KV_W
OFF_BG = OFF_NG + 3 * N_HEADS


def _pack_cols(a):
    lead = a.shape[:-1]
    d_model = a.shape[-1] - OFF_BG
    small = jnp.concatenate([a[..., OFF_MI:OFF_NQ], a[..., OFF_NG:OFF_BG]], axis=-1)
    small = jnp.concatenate([small, jnp.zeros(lead + (LANES - small.shape[-1],), a.dtype)], axis=-1)
    pad = jnp.zeros(lead + (PK_BG - PK_SMALL - LANES,), a.dtype)
    assert d_model == PK_TOTAL - PK_BG
    return jnp.concatenate([a[..., OFF_MX:OFF_MI], a[..., OFF_NQ:OFF_NKV], a[..., OFF_NKV:OFF_NG],
                            small, pad, a[..., OFF_BG:]], axis=-1)


def _dup_halves(x, lane):
    r = pltpu.roll(x, N_HD, 1)
    lo = lane < N_HD
    return jnp.where(lo, x, r), jnp.where(lo, r, x)


def _masked_softmax_rows(s, mask):
    s = jnp.where(mask, s, -jnp.inf)
    mx = jnp.max(s, axis=1, keepdims=True)
    mx = jnp.where(mx > -jnp.inf, mx, 0.0)
    p = jnp.exp(s - mx)
    return p / jnp.maximum(jnp.sum(p, axis=1, keepdims=True), TINY)


def _softmax_pv(s, v):
    m = jnp.max(s, axis=1, keepdims=True)
    p = jnp.exp(s - m)
    return _dot(p.astype(BF16), v) / jnp.sum(p, axis=1, keepdims=True)


def _slc_attend(qx, kx_ref, v_ref, g, s_sc, nfull, diag_bias, ck):
    rows = qx.shape[0]
    nl = ck // LANES
    lane_chunks = lambda s: [s[:, a * LANES:(a + 1) * LANES] for a in range(nl)]

    def scores(c):
        k0 = pl.multiple_of(c * ck, ck)
        return _dot_nt(qx, kx_ref[g, pl.ds(k0, ck), :])

    def max_pass(c, mvec):
        s = scores(c)
        s_sc[c] = s
        return functools.reduce(jnp.maximum, lane_chunks(s), mvec)

    mvec = lax.fori_loop(0, nfull, max_pass, jnp.full((rows, LANES), -jnp.inf, F32))
    sd = scores(nfull) + diag_bias
    mvec = functools.reduce(jnp.maximum, lane_chunks(sd), mvec)
    mb = jnp.broadcast_to(jnp.max(mvec, axis=1, keepdims=True), (rows, LANES))

    def probs(s, c):
        ps = [jnp.exp(x - mb) for x in lane_chunks(s)]
        k0 = pl.multiple_of(c * ck, ck)
        pv = _dot(jnp.concatenate(ps, axis=1).astype(BF16), v_ref[g, pl.ds(k0, ck), :])
        return functools.reduce(lambda x, y: x + y, ps), pv

    def exp_pass(c, carry):
        lp, pv = probs(s_sc[c], c)
        return carry[0] + lp, carry[1] + pv

    lvec, acc = lax.fori_loop(0, nfull, exp_pass, probs(sd, nfull))
    return acc / jnp.sum(lvec, axis=1, keepdims=True)


def _select_blocks(imp, qpos, ns, axis):
    blk = lax.broadcasted_iota(jnp.int32, imp.shape, axis)
    cur = qpos // SLC_BLOCK
    allowed = blk <= cur
    forced = (blk == 0) | (blk == cur) | (blk == cur - 1)
    score = jnp.where(allowed, imp + jnp.where(forced, FORCE_BONUS, 0.0), -jnp.inf)
    cnt = jnp.zeros(imp.shape, F32)
    for i in range(ns):
        si = score[:, i:i + 1] if axis == 1 else score[i:i + 1, :]
        ahead = (si > score) | ((si == score) & (blk > i))
        cnt = cnt + jnp.where(ahead, 1.0, 0.0)
    return jnp.where((cnt < float(N_SELECT)) & allowed, 1.0, 0.0)


def _stack_q(q_ref, g, rows):
    qs = jnp.concatenate([q_ref[:, (4 * g + p) * LANES:(4 * g + p + 1) * LANES] for p in range(N_HPG // 2)], axis=0)
    qs = qs * (N_HD ** -0.5)
    lane = lax.broadcasted_iota(jnp.int32, qs.shape, 1)
    qe = jnp.where(lane < N_HD, qs, 0.0).astype(BF16)
    qo = jnp.where(lane < N_HD, 0.0, qs).astype(BF16)
    return qe, qo


def _gate_pair(gs, g, p, j, lane):
    he = g * N_HPG + 2 * p
    ce = SM_NG + 3 * he + j
    co = ce + 3
    return jnp.where(lane < N_HD, gs[:, ce:ce + 1], gs[:, co:co + 1])


def _nsa_prompt_kernel(q_ref, kv_ref, wcmp_ref, bcmp_ref, mmat_ref, o_ref, kvo_ref, wino_ref,
                       ck_sc, cv_sc, kx_sc, sv_sc, wk_sc, wv_sc, stage_sc, s_sc, *, seq, tq, ck, wn):
    i = pl.program_id(1)
    nseg = seq // CMP_STRIDE
    ns = seq // SLC_BLOCK
    npair = N_HPG // 2

    @pl.when(i == 0)
    def _prep():
        kvo_ref[0] = kv_ref[:, 0:4 * KV_W]
        wino_ref[0] = kv_ref[seq - wino_ref.shape[1]:, 4 * KV_W:6 * KV_W]
        kr = lax.broadcasted_iota(jnp.int32, (seq, LANES), 0)
        kc = lax.broadcasted_iota(jnp.int32, (seq, LANES), 1)
        onehot = jnp.where(kr // SLC_BLOCK == kc, 1.0, 0.0).astype(BF16)
        for g in range(N_KV):
            kx_sc[g, :, LANES:] = onehot
        lane = lax.broadcasted_iota(jnp.int32, (nseg, LANES), 1)
        for c, dst in ((0, ck_sc), (1, cv_sc)):
            acc = jnp.zeros((nseg, 2 * LANES), F32)
            stage_sc[...] = kv_ref[:, c * LANES:(c + 1) * LANES]
            for s in range(CMP_STRIDE):
                xs = stage_sc[pl.ds(s, nseg, stride=CMP_STRIDE), :]
                acc = acc + _dot(xs.astype(BF16), wcmp_ref[c, s])
            blocks = acc[:, :LANES] + pltpu.roll(acc[:, LANES:], nseg - 1, 0) + bcmp_ref[c:c + 1, :]
            d0, d1 = _dup_halves(blocks, lane)
            dst[0] = d0.astype(BF16)
            dst[1] = d1.astype(BF16)
        lane_s = lax.broadcasted_iota(jnp.int32, (seq, LANES), 1)
        for off, dst in ((2, kx_sc), (3, sv_sc), (4, wk_sc), (5, wv_sc)):
            d0, d1 = _dup_halves(kv_ref[:, off * LANES:(off + 1) * LANES], lane_s)
            dst[0, :, 0:LANES] = d0.astype(BF16)
            dst[1, :, 0:LANES] = d1.astype(BF16)

    t0 = i * tq
    qpos = t0 + lax.broadcasted_iota(jnp.int32, (tq, 1), 0)
    qpos_row = t0 + lax.broadcasted_iota(jnp.int32, (1, tq), 1)
    nsp = -(-ns // 8) * 8
    lane = lax.broadcasted_iota(jnp.int32, (tq, LANES), 1)
    gs = jax.nn.sigmoid(kv_ref[pl.ds(pl.multiple_of(t0, tq), tq), 6 * LANES:7 * LANES])
    c_end = lax.broadcasted_iota(jnp.int32, (tq, nseg), 1) * CMP_STRIDE + (CMP_LEN - 1)
    cmask = jnp.concatenate([c_end <= qpos] * npair, axis=0)
    lane4 = lax.broadcasted_iota(jnp.int32, (npair * tq, LANES), 1)
    lo4 = lane4 < N_HD

    nfull = t0 // ck
    kpos_d = nfull * ck + lax.broadcasted_iota(jnp.int32, (tq, ck), 1)
    diag_bias = jnp.concatenate([jnp.where(kpos_d <= qpos, 0.0, NEG_BIG)] * npair, axis=0)
    ks = pl.multiple_of(jnp.clip(t0 + tq - wn, 0, seq - wn), LANES)
    dpos = qpos - (ks + lax.broadcasted_iota(jnp.int32, (tq, wn), 1))
    win_bias = jnp.concatenate([jnp.where((dpos >= 0) & (dpos < WINDOW), 0.0, NEG_BIG)] * npair, axis=0)

    for g in range(N_KV):
        qe, qo = _stack_q(q_ref, g, tq)
        ckg = ck_sc[g]
        cvg = cv_sc[g]
        p_e = _masked_softmax_rows(_dot_nt(qe, ckg), cmask)
        p_o = _masked_softmax_rows(_dot_nt(qo, ckg), cmask)
        o_cmp = jnp.where(lo4, _dot(p_e.astype(BF16), cvg), _dot(p_o.astype(BF16), cvg))
        psum = p_e + p_o
        psum = sum(psum[p * tq:(p + 1) * tq] for p in range(npair))
        imp_t = lax.dot_general(mmat_ref[...], psum, (((1,), (1,)), ((), ())),
                                precision=lax.Precision.HIGHEST, preferred_element_type=F32)
        sel_t = _select_blocks(imp_t[:nsp], qpos_row, ns, 0)
        sel = jnp.concatenate([sel_t, jnp.zeros((LANES - nsp, tq), F32)], axis=0).T
        selb = ((sel - 1.0) * (-NEG_BIG)).astype(BF16)
        selb4 = jnp.concatenate([selb] * npair, axis=0)

        o_slc = jnp.where(lo4,
                          _slc_attend(jnp.concatenate([qe, selb4], axis=1), kx_sc, sv_sc, g, s_sc, nfull, diag_bias, ck),
                          _slc_attend(jnp.concatenate([qo, selb4], axis=1), kx_sc, sv_sc, g, s_sc, nfull, diag_bias, ck))

        kw = wk_sc[g, pl.ds(ks, wn), :]
        vw = wv_sc[g, pl.ds(ks, wn), :]
        o_win = jnp.where(lo4, _softmax_pv(_dot_nt(qe, kw) + win_bias, vw), _softmax_pv(_dot_nt(qo, kw) + win_bias, vw))

        for p in range(npair):
            rs = slice(p * tq, (p + 1) * tq)
            o = (_gate_pair(gs, g, p, 0, lane) * o_cmp[rs] + _gate_pair(gs, g, p, 1, lane) * o_slc[rs]
                 + _gate_pair(gs, g, p, 2, lane) * o_win[rs])
            c0 = (g * npair + p) * LANES
            o_ref[:, c0:c0 + LANES] = o.astype(o_ref.dtype)


def _cmp_to_slc_matrix(nc, ns, rows, cols):
    c0 = np.arange(nc)[:, None] * CMP_STRIDE
    s0 = np.arange(ns)[None, :] * SLC_BLOCK
    ov = np.clip(np.minimum(c0 + CMP_LEN, s0 + SLC_BLOCK) - np.maximum(c0, s0), 0, None) / CMP_LEN
    out = np.zeros((rows, cols), np.float32)
    out[:nc, :ns] = ov
    return jnp.asarray(out)


def _pack_cmp_weights(w_cmp, b_cmp):
    r = CMP_LEN // CMP_STRIDE
    w = w_cmp.reshape(2, r, CMP_STRIDE, N_HD, N_HD)
    eye = jnp.eye(N_KV, dtype=w.dtype)
    bd = jnp.einsum('gh,crsde->crsgdhe', eye, w).reshape(2, r, CMP_STRIDE, KV_W, KV_W)
    wp = jnp.concatenate([bd[:, k] for k in range(r)], axis=-1)
    bp = jnp.concatenate([b_cmp] * N_KV, axis=-1)
    return wp.astype(BF16), bp


def nsa_prompt(z, bsz, seq, wcmp_p, bcmp_p, tq=128, ck=256):
    nseg = seq // CMP_STRIDE
    ns = seq // SLC_BLOCK
    nq = seq // tq
    ck = min(ck, seq)
    wn = min(WINDOW + tq, seq)
    assert seq % ck == 0 and ck % tq == 0 and ns <= LANES and seq % LANES == 0
    mmat = _cmp_to_slc_matrix(nseg - 1, ns, nseg, LANES).T
    wlen = min(WINDOW, seq)
    kern = functools.partial(_nsa_prompt_kernel, seq=seq, tq=tq, ck=ck, wn=wn)
    return pl.pallas_call(
        kern,
        grid=(bsz, nq),
        in_specs=[
            pl.BlockSpec((tq, 1024), lambda b, i: (b * nq + i, PK_NQ // 1024)),
            pl.BlockSpec((seq, 1024), lambda b, i: (b, PK_NKV // 1024)),
            pl.BlockSpec((2, CMP_STRIDE, KV_W, 2 * KV_W), lambda b, i: (0, 0, 0, 0)),
            pl.BlockSpec((2, KV_W), lambda b, i: (0, 0)),
            pl.BlockSpec((LANES, nseg), lambda b, i: (0, 0)),
        ],
        out_specs=[pl.BlockSpec((tq, N_WIDTH), lambda b, i: (b * nq + i, 0)),
                   pl.BlockSpec((1, seq, 4 * KV_W), lambda b, i: (b, 0, 0)),
                   pl.BlockSpec((1, wlen, 2 * KV_W), lambda b, i: (b, 0, 0))],
        out_shape=[jax.ShapeDtypeStruct((bsz * seq, N_WIDTH), BF16),
                   jax.ShapeDtypeStruct((bsz, seq, 4 * KV_W), F32),
                   jax.ShapeDtypeStruct((bsz, wlen, 2 * KV_W), F32)],
        scratch_shapes=[
            pltpu.VMEM((N_KV, nseg, LANES), BF16), pltpu.VMEM((N_KV, nseg, LANES), BF16),
            pltpu.VMEM((N_KV, seq, 2 * LANES), BF16), pltpu.VMEM((N_KV, seq, LANES), BF16),
            pltpu.VMEM((N_KV, seq, LANES), BF16), pltpu.VMEM((N_KV, seq, LANES), BF16),
            pltpu.VMEM((seq, LANES), F32),
            pltpu.VMEM((seq // ck, (N_HPG // 2) * tq, ck), F32),
        ],
        compiler_params=_cparams(("parallel", "arbitrary")),
        name="nsa_prompt",
    )(z, z, wcmp_p, bcmp_p, mmat)


def _merge_kernel(x_ref, hm_ref, hn_ref, bgm_ref, bgn_ref, wbm_ref, wbn_ref, wout_ref, n2_ref, wr_ref, br_ref,
                  h_ref, xn2_ref, logit_ref):
    t = (jax.nn.sigmoid(bgm_ref[...]) * _dot(hm_ref[...], wbm_ref[...])
         + jax.nn.sigmoid(bgn_ref[...]) * _dot(hn_ref[...], wbn_ref[...]))
    h = x_ref[...] + _dot(t.astype(BF16), wout_ref[...])
    h_ref[...] = h
    xn2 = (h * lax.rsqrt(jnp.mean(h * h, axis=-1, keepdims=True) + EPS) * n2_ref[...]).astype(BF16)
    xn2_ref[...] = xn2
    logit_ref[...] = _dot(xn2, wr_ref[...]) + br_ref[...]


def merge(x, hm, hn, z, wbm, wbn, wout, n2, wr, br, tm):
    n, d = x.shape
    row = lambda i: (i, 0)
    fixed = lambda i: (0, 0)
    return pl.pallas_call(
        _merge_kernel,
        grid=(n // tm,),
        in_specs=[
            pl.BlockSpec((tm, d), row), pl.BlockSpec((tm, d), row), pl.BlockSpec((tm, d), row),
            pl.BlockSpec((tm, d), lambda i: (i, PK_BG // 1024)),
            pl.BlockSpec((tm, d), lambda i: (i, PK_BG // 1024 + 1)),
            pl.BlockSpec((d, d), fixed), pl.BlockSpec((d, d), fixed), pl.BlockSpec((d, d), fixed),
            pl.BlockSpec((1, d), fixed), pl.BlockSpec((d, LANES), fixed), pl.BlockSpec((1, LANES), fixed),
        ],
        out_specs=[pl.BlockSpec((tm, d), row), pl.BlockSpec((tm, d), row), pl.BlockSpec((tm, LANES), row)],
        out_shape=[jax.ShapeDtypeStruct((n, d), F32), jax.ShapeDtypeStruct((n, d), BF16),
                   jax.ShapeDtypeStruct((n, LANES), F32)],
        compiler_params=_cparams(("parallel",)),
        name="merge",
    )(x, hm, hn, z, z, wbm, wbn, wout, n2, wr, br)


def _expert_kernel(te_ref, nu_ref, x_ref, wgu_ref, bgu_ref, wdn_ref, bdn_ref, y_ref):
    @pl.when(pl.program_id(0) < nu_ref[0])
    def _():
        de = wdn_ref.shape[1]
        gu = _dot(x_ref[...], wgu_ref[0]) + bgu_ref[0]
        gate = jnp.minimum(gu[:, :de], SWIGLU_LIMIT)
        up = jnp.clip(gu[:, de:], -SWIGLU_LIMIT, SWIGLU_LIMIT)
        glu = gate * jax.nn.sigmoid(gate * SWIGLU_ALPHA)
        act = ((up + 1.0) * glu).astype(BF16)
        y_ref[...] = _dot(act, wdn_ref[0]) + bdn_ref[0]


def expert_ffn(x_sorted, tile_e, n_used, wgu, bgu, wdn, bdn, tm):
    slots, d = x_sorted.shape
    de = wdn.shape[1]
    n_tiles = slots // tm
    rows = lambda t, te, nu: (jnp.minimum(t, nu[0] - 1), 0)
    per_e = lambda t, te, nu: (te[t], 0, 0)
    grid_spec = pltpu.PrefetchScalarGridSpec(
        num_scalar_prefetch=2,
        grid=(n_tiles,),
        in_specs=[
            pl.BlockSpec((tm, d), rows),
            pl.BlockSpec((1, d, 2 * de), per_e), pl.BlockSpec((1, 1, 2 * de), per_e),
            pl.BlockSpec((1, de, d), per_e), pl.BlockSpec((1, 1, d), per_e),
        ],
        out_specs=pl.BlockSpec((tm, d), rows),
    )
    return pl.pallas_call(
        _expert_kernel,
        grid_spec=grid_spec,
        out_shape=jax.ShapeDtypeStruct((slots, d), F32),
        compiler_params=_cparams(("arbitrary",)),
        name="expert_ffn",
    )(tile_e, n_used, x_sorted, wgu, bgu, wdn, bdn)


def _combine_kernel(h_ref, *refs):
    y_refs = refs[:TOP_K]
    w_ref, g_ref, o_ref = refs[TOP_K:]
    w = w_ref[...]
    moe = w[:, 0:1] * y_refs[0][...]
    for k in range(1, TOP_K):
        moe = moe + w[:, k:k + 1] * y_refs[k][...]
    acc = h_ref[...] + moe
    o_ref[...] = acc * lax.rsqrt(jnp.mean(acc * acc, axis=-1, keepdims=True) + EPS) * g_ref[...]


def combine(h, y4, w4, g, tm, row0, nrows):
    n, d = h.shape
    nb = n // tm
    rb0 = row0 // tm
    return pl.pallas_call(
        _combine_kernel,
        grid=(nrows // tm,),
        in_specs=[pl.BlockSpec((tm, d), lambda i: (rb0 + i, 0))]
        + [pl.BlockSpec((tm, d), functools.partial(lambda i, k: (k * nb + rb0 + i, 0), k=k)) for k in range(TOP_K)]
        + [pl.BlockSpec((tm, TOP_K), lambda i: (rb0 + i, 0)), pl.BlockSpec((1, d), lambda i: (0, 0))],
        out_specs=pl.BlockSpec((tm, d), lambda i: (i, 0)),
        out_shape=jax.ShapeDtypeStruct((nrows, d), F32),
        compiler_params=_cparams(("parallel",)),
        name="combine",
    )(h, *([y4] * TOP_K), w4, g)


def _route(logits, tm):
    n = logits.shape[0]
    top_v, top_e = lax.top_k(logits, TOP_K)
    top_w = jax.nn.softmax(top_v, axis=-1)
    nk = n * TOP_K
    flat_e = top_e.reshape(-1)
    order = jnp.argsort(flat_e)
    sorted_e = flat_e[order]
    counts = jnp.bincount(flat_e, length=N_EXPERTS)
    padded = (counts + tm - 1) // tm * tm
    pad_end = jnp.cumsum(padded)
    pad_start = pad_end - padded
    grp_start = jnp.cumsum(counts) - counts
    dest = (pad_start[sorted_e] + jnp.arange(nk) - grp_start[sorted_e]).astype(jnp.int32)
    n_tiles = -(-(nk + N_EXPERTS * (tm - 1)) // tm)
    tile_e = jnp.minimum(jnp.sum(pad_end[None, :] <= (jnp.arange(n_tiles) * tm)[:, None], axis=1),
                         N_EXPERTS - 1).astype(jnp.int32)
    slot = jnp.arange(n_tiles * tm, dtype=jnp.int32)
    slot_e = jnp.repeat(tile_e, tm)
    src = jnp.clip(grp_start[slot_e] + slot - pad_start[slot_e], 0, nk - 1)
    slot_tok = (jnp.take(order, src, mode='clip') // TOP_K).astype(jnp.int32)
    inv_slot = jnp.take(dest, jnp.argsort(order), mode='clip')
    n_used = (pad_end[-1:] // tm).astype(jnp.int32)
    return top_w, slot_tok, inv_slot.reshape(n, TOP_K), tile_e, n_used


def _plain_softmax_parts(parts):
    ms = [jnp.where(m, s, NEG_BIG) if m is not None else s for s, m in parts]
    mx = functools.reduce(jnp.maximum, [jnp.max(s, axis=1, keepdims=True) for s in ms])
    ps = [jnp.exp(s - mx) for s in ms]
    den = functools.reduce(lambda a, b: a + b, [jnp.sum(p, axis=1, keepdims=True) for p in ps])
    return [p / den for p in ps]


def _nsa_sample_kernel(pt_ref, *refs, npages, pps, tlen, page, wlen):
    page_refs = refs[:pps]
    (q_ref, kvn_ref, win_ref, wc_ref, bc_ref, mmat_ref, o_ref,
     stk_sc, stv_sc, kx_sc, v_sc, s_sc) = refs[pps:]
    b = pl.program_id(0)
    j = pl.program_id(1)
    nsteps = npages // pps
    past = npages * page
    nseg = past // CMP_STRIDE
    nblk = past // SLC_BLOCK
    ns = nblk + 1
    rows = N_HEADS * tlen
    grows = N_HPG * tlen

    @pl.when((b == 0) & (j == 0))
    def _onehot():
        r = lax.broadcasted_iota(jnp.int32, (LANES, past), 0)
        c = lax.broadcasted_iota(jnp.int32, (LANES, past), 1)
        kx_sc[LANES:, :] = jnp.where(c // SLC_BLOCK == r, 1.0, 0.0).astype(BF16)

    for k in range(pps):
        r0 = pl.multiple_of((j * pps + k) * page, page)
        pg = page_refs[k]
        stk_sc[pl.ds(r0, page), :] = pg[0, 0:LANES, :].T
        stv_sc[pl.ds(r0, page), :] = pg[0, LANES:2 * LANES, :].T
        kx_sc[0:LANES, pl.ds(r0, page)] = pg[0, 2 * LANES:3 * LANES, :].astype(BF16)
        v_sc[:, pl.ds(r0, page)] = pg[0, 3 * LANES:4 * LANES, :].astype(BF16)

    @pl.when(j == nsteps - 1)
    def _attend():
        lane = lax.broadcasted_iota(jnp.int32, (tlen, LANES), 1)
        lo = lane < N_HD

        def cmp_proj(st_sc, c):
            acc = jnp.zeros((nseg, 2 * LANES), F32)
            for sp in range(CMP_STRIDE // 2):
                xa = st_sc[pl.ds(2 * sp, nseg, stride=CMP_STRIDE), :]
                xb = st_sc[pl.ds(2 * sp + 1, nseg, stride=CMP_STRIDE), :]
                acc = acc + _dot(jnp.concatenate([xa, xb], axis=1).astype(BF16), wc_ref[c, sp])
            blocks = acc[:, :LANES] + pltpu.roll(acc[:, LANES:], nseg - 1, 0) + bc_ref[c:c + 1, :]
            return blocks.astype(BF16)

        ck = cmp_proj(stk_sc, 0)
        cv = cmp_proj(stv_sc, 1)

        qall = q_ref[...] * (N_HD ** -0.5)
        pieces = []
        for h in range(N_HEADS):
            slab = qall[:, (h // 2) * LANES:(h // 2 + 1) * LANES]
            g = h // N_HPG
            if (h % 2) != g:
                slab = pltpu.roll(slab, N_HD, 1)
            pieces.append(jnp.where(lo, slab, 0.0) if g == 0 else jnp.where(lo, 0.0, slab))
        qbd = jnp.concatenate(pieces, axis=0).astype(BF16)
        trow = lax.broadcasted_iota(jnp.int32, (rows, 1), 0) & (tlen - 1)
        qpos = past + trow
        t8 = lax.broadcasted_iota(jnp.int32, (tlen, 1), 0)

        kvn = kvn_ref[...]
        zpad = jnp.zeros((LANES - tlen, LANES), F32)
        newk = lambda off: jnp.concatenate([kvn[:, off * LANES:(off + 1) * LANES], zpad], axis=0).astype(BF16)
        new_lane = lax.broadcasted_iota(jnp.int32, (rows, LANES), 1)
        new_mask = new_lane <= trow

        c_end = lax.broadcasted_iota(jnp.int32, (rows, nseg), 1) * CMP_STRIDE + (CMP_LEN - 1)
        p_cmp = _masked_softmax_rows(_dot_nt(qbd, ck), c_end <= qpos)
        o_cmp = _dot(p_cmp.astype(BF16), cv)

        bias_rows, bias_new = [], []
        for g in range(N_KV):
            psum = sum(p_cmp[g * grows + h * tlen:g * grows + (h + 1) * tlen] for h in range(N_HPG))
            imp = jnp.dot(psum, mmat_ref[...], precision=lax.Precision.HIGHEST, preferred_element_type=F32)
            sel = _select_blocks(imp, past + t8, ns, 1)
            selb = (sel - 1.0) * (-NEG_BIG)
            bias_rows += [selb[:, :LANES]] * N_HPG
            bias_new += [selb[:, nblk:nblk + 1]] * N_HPG
        qx = jnp.concatenate([qbd, jnp.concatenate(bias_rows, axis=0).astype(BF16)], axis=1)
        bias_new = jnp.concatenate(bias_new, axis=0)

        ckeys = min(past, 1024)
        mx = jnp.full((rows, 1), -jnp.inf, F32)
        for c0 in range(0, past, ckeys):
            s = _dot(qx, kx_sc[:, c0:c0 + ckeys])
            s_sc[:, c0:c0 + ckeys] = s
            mx = jnp.maximum(mx, jnp.max(s, axis=1, keepdims=True))
        s_new = jnp.where(new_mask, _dot_nt(qbd, newk(2)) + bias_new, NEG_BIG)
        mx = jnp.maximum(mx, jnp.max(s_new, axis=1, keepdims=True))
        p_new = jnp.exp(s_new - mx)
        den = jnp.sum(p_new, axis=1, keepdims=True)
        acc = _dot(p_new.astype(BF16), newk(3))
        for c0 in range(0, past, ckeys):
            p = jnp.exp(s_sc[:, c0:c0 + ckeys] - mx)
            den = den + jnp.sum(p, axis=1, keepdims=True)
            acc = acc + _dot_nt(p.astype(BF16), v_sc[:, c0:c0 + ckeys])
        o_slc = acc / den

        wr = lax.broadcasted_iota(jnp.int32, (rows, wlen), 1)
        dpos = trow + wlen - wr
        wk_old = win_ref[0, :, 0:LANES].astype(BF16)
        wv_old = win_ref[0, :, LANES:2 * LANES].astype(BF16)
        pw_old, pw_new = _plain_softmax_parts([(_dot_nt(qbd, wk_old), (dpos >= 0) & (dpos < WINDOW)),
                                               (_dot_nt(qbd, newk(4)), new_mask)])
        o_win = _dot(pw_old.astype(BF16), wv_old) + _dot(pw_new.astype(BF16), newk(5))

        gs = jax.nn.sigmoid(kvn[:, 6 * LANES:7 * LANES])
        for pr in range(N_HEADS // 2):
            halves = []
            for par in range(2):
                h = 2 * pr + par
                g = h // N_HPG
                rs = slice(h * tlen, (h + 1) * tlen)
                c0 = SM_NG + 3 * h
                o = (gs[:, c0:c0 + 1] * o_cmp[rs] + gs[:, c0 + 1:c0 + 2] * o_slc[rs]
                     + gs[:, c0 + 2:c0 + 3] * o_win[rs])
                halves.append(pltpu.roll(o, N_HD, 1) if par != g else o)
            o_ref[0, :, pr * LANES:(pr + 1) * LANES] = jnp.where(lo, halves[0], halves[1])


def nsa_sample(z, row0, page_table, cache_t, layer, win2, wcmp_p, bcmp_p, tlen, pps=8):
    bsz, npages = page_table.shape
    page = cache_t.shape[2]
    wlen = win2.shape[1]
    past = npages * page
    nseg = past // CMP_STRIDE
    nblk = past // SLC_BLOCK
    assert npages % pps == 0 and nblk <= LANES and tlen & (tlen - 1) == 0 and tlen <= SLC_BLOCK and wlen == WINDOW
    mmat = _cmp_to_slc_matrix(nseg - 1, nblk + 1, nseg, 2 * LANES)
    wc2 = wcmp_p.reshape(2, CMP_STRIDE // 2, 2 * KV_W, 2 * KV_W)
    rb0 = row0 // tlen
    kern = functools.partial(_nsa_sample_kernel, npages=npages, pps=pps, tlen=tlen, page=page, wlen=wlen)

    def page_spec(k):
        return pl.BlockSpec((1, 4 * KV_W, page), lambda b, j, pt: (pt[b, j * pps + k], layer, 0))

    grid_spec = pltpu.PrefetchScalarGridSpec(
        num_scalar_prefetch=1,
        grid=(bsz, npages // pps),
        in_specs=[page_spec(k) for k in range(pps)] + [
            pl.BlockSpec((tlen, 1024), lambda b, j, pt: (rb0 + b, PK_NQ // 1024)),
            pl.BlockSpec((tlen, 1024), lambda b, j, pt: (rb0 + b, PK_NKV // 1024)),
            pl.BlockSpec((1, wlen, 2 * KV_W), lambda b, j, pt: (b, 0, 0)),
            pl.BlockSpec((2, CMP_STRIDE // 2, 2 * KV_W, 2 * KV_W), lambda b, j, pt: (0, 0, 0, 0)),
            pl.BlockSpec((2, KV_W), lambda b, j, pt: (0, 0)),
            pl.BlockSpec((nseg, 2 * LANES), lambda b, j, pt: (0, 0)),
        ],
        out_specs=pl.BlockSpec((1, tlen, N_WIDTH), lambda b, j, pt: (b, 0, 0)),
        scratch_shapes=[
            pltpu.VMEM((past, LANES), F32), pltpu.VMEM((past, LANES), F32),
            pltpu.VMEM((2 * LANES, past), BF16), pltpu.VMEM((LANES, past), BF16),
            pltpu.VMEM((N_HEADS * tlen, past), F32),
        ],
    )
    return pl.pallas_call(
        kern,
        grid_spec=grid_spec,
        out_shape=jax.ShapeDtypeStruct((bsz, tlen, N_WIDTH), F32),
        compiler_params=_cparams(("arbitrary", "arbitrary")),
        name="nsa_sample",
    )(page_table, *([cache_t] * pps), z, z, win2, wc2, bcmp_p, mmat)


ROW_TILE = 1024
MERGE_TILE = 512
MOE_TILE = 512
PROMPT_CHUNK = 128


def kernel(x_prompt, x_sample, cache_kv, cache_win_kv, state_conv, state_C, state_n, state_m, page_table,
           norm1_g, w_in, b_in, w_conv, b_conv, w_mq, w_mk, g_mnorm, w_cmp, b_cmp,
           w_branch_m, w_branch_n, w_out, norm2_g, w_router, b_router, w_gu, b_gu, w_dn, b_dn, normf_g):
    bp, sp, d = x_prompt.shape
    bs, ts, _ = x_sample.shape
    depth = w_in.shape[0]
    n_p, n_s = bp * sp, bs * ts
    n = n_p + n_s
    assert n_p % ROW_TILE == 0 and n_s % ROW_TILE == 0 and n % MERGE_TILE == 0
    assert sp % PROMPT_CHUNK == 0 and ts % 8 == 0 and ts >= M_CONV - 1

    x = jnp.concatenate([x_prompt.reshape(n_p, d), x_sample.reshape(n_s, d)], axis=0)
    st_p, st_s = [], []
    for l in range(depth):
        w_in_p = _pack_cols(w_in[l]).astype(BF16)
        b_in_p = _pack_cols(b_in[l][None])
        z = norm_matmul(x, norm1_g[l][None], w_in_p, b_in_p, ROW_TILE, 1024)

        wq, wk = w_mq[l].astype(BF16), w_mk[l].astype(BF16)
        gn = g_mnorm[l].reshape(1, M_WIDTH)
        bconv = b_conv[l][None]

        def gate_rows(r0, bsz, tlen):
            g = z[r0:r0 + bsz * tlen, PK_SMALL:PK_SMALL + 2 * M_HEADS]
            g = g.reshape(bsz, tlen, 2 * M_HEADS).transpose(0, 2, 1)
            if tlen < LANES:
                g = jnp.pad(g, ((0, 0), (0, 0), (0, LANES - tlen)))
            return g

        zero = lambda *s: jnp.zeros(s, F32)
        hm_p, c_p, nn_p, m_p, tail_p = mlstm(z, gate_rows(0, bp, sp), 0, bp, sp, PROMPT_CHUNK,
                                             zero(bp, 8, M_WIDTH), zero(bp, M_HEADS, M_DQK, M_DV),
                                             zero(bp, M_HEADS, M_DQK), zero(bp, 1, M_HEADS),
                                             w_conv[l], bconv, wq, wk, gn)
        cbuf_s = jnp.pad(state_conv[l], ((0, 0), (8 - (M_CONV - 1), 0), (0, 0)))
        hm_s, c_s, nn_s, m_s, tail_s = mlstm(z, gate_rows(n_p, bs, ts), n_p, bs, ts, ts,
                                             cbuf_s, state_C[l], state_n[l], state_m[l][:, None, :],
                                             w_conv[l], bconv, wq, wk, gn)

        wcmp_p, bcmp_p = _pack_cmp_weights(w_cmp[l], b_cmp[l])
        hn_p, kv_p, win_p = nsa_prompt(z, bp, sp, wcmp_p, bcmp_p)
        wlen_s = cache_win_kv.shape[2]
        win2 = cache_win_kv[l].reshape(bs, wlen_s, 2 * KV_W)
        cache_t = jnp.transpose(cache_kv, (0, 2, 3, 4, 5, 1)).reshape(cache_kv.shape[0], depth * 4 * KV_W,
                                                                      cache_kv.shape[1])
        hn_s = nsa_sample(z, n_p, page_table, cache_t, l, win2, wcmp_p, bcmp_p, ts)

        hm = jnp.concatenate([hm_p.reshape(n_p, M_WIDTH), hm_s.reshape(n_s, M_WIDTH)], axis=0)
        hn = jnp.concatenate([hn_p, hn_s.reshape(n_s, N_WIDTH).astype(BF16)], axis=0)
        wr = jnp.pad(w_router[l], ((0, 0), (0, LANES - N_EXPERTS))).astype(BF16)
        br = jnp.pad(b_router[l], (0, LANES - N_EXPERTS))[None]
        h, xn2, logits = merge(x, hm, hn, z, w_branch_m[l].astype(BF16), w_branch_n[l].astype(BF16),
                               w_out[l].astype(BF16), norm2_g[l][None], wr, br, MERGE_TILE)

        top_w, slot_tok, inv_slot, tile_e, n_used = _route(logits[:, :N_EXPERTS], MOE_TILE)
        x_sorted = jnp.take(xn2, slot_tok, axis=0, mode='clip')
        y_slots = expert_ffn(x_sorted, tile_e, n_used, w_gu[l].astype(BF16), b_gu[l][:, None, :],
                             w_dn[l].astype(BF16), b_dn[l][:, None, :], MOE_TILE)
        y4 = jnp.take(y_slots, inv_slot.T.reshape(-1), axis=0, mode='clip')
        assert l == depth - 1, "only DEPTH == 1 is supported (the final norm is fused into the combine step)"
        y_prompt = combine(h, y4, top_w, normf_g[None], ROW_TILE, 0, n_p).reshape(bp, sp, d)
        y_sample = combine(h, y4, top_w, normf_g[None], ROW_TILE, n_p, n_s).reshape(bs, ts, d)

        kvw_s = z[n_p:, PK_NKV:PK_NKV + 6 * KV_W].reshape(bs, ts, 6, N_KV, N_HD)
        win_s = jnp.concatenate([cache_win_kv[l][:, ts:], kvw_s[:, :, 4:]], axis=1)
        tail0 = 8 - (M_CONV - 1)
        st_p.append((kv_p.reshape(bp, sp, 4, N_KV, N_HD), win_p.reshape(bp, -1, 2, N_KV, N_HD), tail_p[:, tail0:],
                     c_p, nn_p, m_p[:, 0]))
        st_s.append((kvw_s[:, :, :4], win_s, tail_s[:, tail0:], c_s, nn_s, m_s[:, 0]))

    stack = lambda sts, k, axis=0: jnp.stack([s[k] for s in sts], axis=axis)
    return (y_prompt, y_sample,
            stack(st_p, 0, 2), stack(st_p, 1), stack(st_p, 2), stack(st_p, 3), stack(st_p, 4), stack(st_p, 5),
            stack(st_s, 0, 2), stack(st_s, 1), stack(st_s, 2), stack(st_s, 3), stack(st_s, 4), stack(st_s, 5))
```

```python
import functools
import math

import jax
import jax.numpy as jnp
import numpy as np
from jax import lax
from jax.experimental import pallas as pl
from jax.experimental.pallas import tpu as pltpu

F32 = jnp.float32
BF16 = jnp.bfloat16

M_HEADS = 4
M_DV = 256
M_DQK = 128
M_WIDTH = M_HEADS * M_DV
M_CONV = 4
N_HEADS = 16
N_KV = 2
N_HD = 64
N_HPG = N_HEADS // N_KV
N_WIDTH = N_HEADS * N_HD
KV_W = N_KV * N_HD
CMP_LEN = 32
CMP_STRIDE = 16
SLC_BLOCK = 64
N_SELECT = 16
WINDOW = 512
FORCE_BONUS = 1000.0
N_EXPERTS = 32
TOP_K = 4
SWIGLU_LIMIT = 7.0
SWIGLU_ALPHA = 1.702
EPS = 1e-6
TINY = 1e-30
NEG_BIG = -1e30

LANES = 128
PK_MX = 0
PK_MV = 1024
PK_MO = 2048
PK_NQ = 3072
PK_NKV = 4096
PK_SMALL = PK_NKV + 6 * KV_W
PK_BG = 5120
PK_TOTAL = 7168
SM_IG = 0
SM_LF = M_HEADS
SM_NG = 2 * M_HEADS

VMEM_LIMIT = 56 * 1024 * 1024


def _cparams(sem):
    return pltpu.CompilerParams(dimension_semantics=sem, vmem_limit_bytes=VMEM_LIMIT)


def _log_sigmoid(x):
    return jnp.minimum(x, 0.0) - jnp.log1p(jnp.exp(-jnp.abs(x)))


def _dot(a, b):
    return jnp.dot(a, b, preferred_element_type=F32)


def _dot_nt(a, b):
    return lax.dot_general(a, b, (((1,), (1,)), ((), ())), preferred_element_type=F32)


def _dot_tn(a, b):
    return lax.dot_general(a, b, (((0,), (0,)), ((), ())), preferred_element_type=F32)


def _norm_matmul_kernel(xp_ref, xs_ref, g_ref, w_ref, b_ref, o_ref, xn_ref, *, nbp):
    @pl.when(pl.program_id(1) == 0)
    def _():
        x = jnp.where(pl.program_id(0) < nbp, xp_ref[...], xs_ref[...])
        ms = jnp.mean(x * x, axis=-1, keepdims=True)
        xn_ref[...] = (x * lax.rsqrt(ms + EPS) * g_ref[...]).astype(BF16)

    o_ref[...] = _dot(xn_ref[...], w_ref[...]) + b_ref[...]


def norm_matmul(xp, xs, g, w, b, tm, tn):
    n_p, d = xp.shape
    n = n_p + xs.shape[0]
    nbp = n_p // tm
    nc = w.shape[1]
    return pl.pallas_call(
        functools.partial(_norm_matmul_kernel, nbp=nbp),
        grid=(n // tm, nc // tn),
        in_specs=[
            pl.BlockSpec((tm, d), lambda i, j: (jnp.minimum(i, nbp - 1), 0)),
            pl.BlockSpec((tm, d), lambda i, j: (jnp.maximum(i - nbp, 0), 0)),
            pl.BlockSpec((1, d), lambda i, j: (0, 0)),
            pl.BlockSpec((d, tn), lambda i, j: (0, j)),
            pl.BlockSpec((1, tn), lambda i, j: (0, j)),
        ],
        out_specs=pl.BlockSpec((tm, tn), lambda i, j: (i, j)),
        out_shape=jax.ShapeDtypeStruct((n, nc), F32),
        scratch_shapes=[pltpu.VMEM((tm, d), BF16)],
        compiler_params=_cparams(("parallel", "arbitrary")),
        name="norm_matmul",
    )(xp, xs, g, w, b)


def _shift_rows(x, tail, d):
    rows = x.shape[0]
    xd = pltpu.roll(x, d, 0)
    td = pltpu.roll(tail, d, 0)
    head = jnp.where(lax.broadcasted_iota(jnp.int32, td.shape, 0) < d, td, xd[:8])
    if rows == 8:
        return head
    return jnp.concatenate([head, xd[8:]], axis=0)


def _cumsum_rows(x):
    n = x.shape[0]
    idx = lax.broadcasted_iota(jnp.int32, x.shape, 0)
    k = 1
    while k < n:
        x = x + jnp.where(idx >= k, pltpu.roll(x, k, 0), 0.0)
        k *= 2
    return x


def _cumsum_lanes(x, n):
    idx = lax.broadcasted_iota(jnp.int32, x.shape, 1)
    k = 1
    while k < n:
        x = x + jnp.where(idx >= k, pltpu.roll(x, k, 1), 0.0)
        k *= 2
    return x


def _mlstm_kernel(xm_ref, v_ref, o_ref, gc_ref, gr_ref, cbuf_ref, c0_ref, n0_ref, m0_ref,
                  wconv_ref, bconv_ref, wq_ref, wk_ref, gn_ref,
                  h_ref, cout_ref, nout_ref, mout_ref, tout_ref,
                  c_sc, n_sc, m_sc, tail_sc, *, chunk):
    L = chunk
    c = pl.program_id(1)

    @pl.when(c == 0)
    def _init():
        c_sc[...] = c0_ref[0]
        n_sc[...] = n0_ref[0]
        m_sc[...] = m0_ref[0]
        tail_sc[...] = cbuf_ref[0]

    x = xm_ref[...]
    tail = tail_sc[...]
    wc = wconv_ref[...]
    xc = x * wc[M_CONV - 1:M_CONV] + bconv_ref[...]
    for d in range(1, M_CONV):
        xc = xc + _shift_rows(x, tail, d) * wc[M_CONV - 1 - d:M_CONV - d]
    tail_sc[...] = x[L - 8:]
    xc = xc * jax.nn.sigmoid(xc)
    xcb = xc.astype(BF16)

    gc = gc_ref[...]
    gr = gr_ref[0]
    b_col = _cumsum_rows(_log_sigmoid(gc))
    b_row = _cumsum_lanes(_log_sigmoid(gr), L)
    causal = (lax.broadcasted_iota(jnp.int32, (L, L), 1) <= lax.broadcasted_iota(jnp.int32, (L, L), 0))

    vb = v_ref[...].astype(BF16)
    og = jax.nn.sigmoid(o_ref[...])
    gn = gn_ref[...]
    m_all = m_sc[...]
    scale = M_DQK ** -0.5
    m_new_list = []
    for h in range(M_HEADS):
        hs = slice(h * M_DV, (h + 1) * M_DV)
        q = _dot(xcb[:, hs], wq_ref[h])
        k = _dot(xcb[:, hs], wk_ref[h]) * scale
        qb = q.astype(BF16)
        m_prev = m_all[:, h:h + 1]
        bc = b_col[:, SM_LF + h:SM_LF + h + 1]
        igc = gc[:, SM_IG + h:SM_IG + h + 1]
        br = b_row[SM_LF + h:SM_LF + h + 1, :L]
        igr = gr[SM_IG + h:SM_IG + h + 1, :L]
        a = bc + m_prev
        dmat = jnp.where(causal, bc - br + igr, -jnp.inf)
        mt = jnp.maximum(a, jnp.max(dmat, axis=1, keepdims=True))
        w_intra = jnp.exp(dmat - mt)
        w_inter = jnp.exp(a - mt)
        qk = _dot_nt(qb, k.astype(BF16)) * w_intra
        cmat = c_sc[h]
        nrow = n_sc[h:h + 1, :]
        vh = vb[:, hs]
        num = _dot(qk.astype(BF16), vh) + w_inter * _dot(qb, cmat.astype(BF16))
        den = jnp.sum(qk, axis=1, keepdims=True) + w_inter * jnp.sum(q * nrow, axis=1, keepdims=True)
        hh = num * (1.0 / jnp.maximum(jnp.abs(den), jnp.exp(-mt)))
        m_new = mt[L - 1:L, :]
        w_end = jnp.exp(bc[L - 1:L, :] - bc + igc - m_new)
        decay = jnp.exp(a[L - 1:L, :] - m_new)
        kw = k * w_end
        c_sc[h] = decay * cmat + _dot_tn(kw.astype(BF16), vh)
        n_sc[h:h + 1, :] = decay * nrow + jnp.sum(kw, axis=0, keepdims=True)
        m_new_list.append(m_new)
        hn = hh * lax.rsqrt(jnp.mean(hh * hh, axis=-1, keepdims=True) + EPS) * gn[:, hs]
        h_ref[0, :, hs] = (hn * og[:, hs]).astype(h_ref.dtype)
    m_sc[...] = jnp.concatenate(m_new_list, axis=1)

    @pl.when(c == pl.num_programs(1) - 1)
    def _fin():
        cout_ref[0] = c_sc[...]
        nout_ref[0] = n_sc[...]
        mout_ref[0] = m_sc[...]
        tout_ref[0] = tail_sc[...]


def mlstm(z, g_rows, row0, bsz, tlen, chunk, conv_buf8, c0, n0, m0, w_conv, b_conv, wq, wk, gn):
    L = chunk
    nc = tlen // L
    rb0 = row0 // L
    lr = g_rows.shape[2]
    grl = L if nc > 1 else lr

    def zspec(col):
        return pl.BlockSpec((L, 1024), lambda b, c: (rb0 + b * nc + c, col))

    kern = functools.partial(_mlstm_kernel, chunk=L)
    return pl.pallas_call(
        kern,
        grid=(bsz, nc),
        in_specs=[
            zspec(PK_MX // 1024), zspec(PK_MV // 1024), zspec(PK_MO // 1024),
            pl.BlockSpec((L, LANES), lambda b, c: (rb0 + b * nc + c, PK_SMALL // LANES)),
            pl.BlockSpec((1, 8, grl), lambda b, c: (b, 0, c)),
            pl.BlockSpec((1, 8, M_WIDTH), lambda b, c: (b, 0, 0)),
            pl.BlockSpec((1, M_HEADS, M_DQK, M_DV), lambda b, c: (b, 0, 0, 0)),
            pl.BlockSpec((1, M_HEADS, M_DQK), lambda b, c: (b, 0, 0)),
            pl.BlockSpec((1, 1, M_HEADS), lambda b, c: (b, 0, 0)),
            pl.BlockSpec((M_CONV, M_WIDTH), lambda b, c: (0, 0)),
            pl.BlockSpec((1, M_WIDTH), lambda b, c: (0, 0)),
            pl.BlockSpec((M_HEADS, M_DV, M_DQK), lambda b, c: (0, 0, 0)),
            pl.BlockSpec((M_HEADS, M_DV, M_DQK), lambda b, c: (0, 0, 0)),
            pl.BlockSpec((1, M_WIDTH), lambda b, c: (0, 0)),
        ],
        out_specs=[
            pl.BlockSpec((1, L, M_WIDTH), lambda b, c: (b, c, 0)),
            pl.BlockSpec((1, M_HEADS, M_DQK, M_DV), lambda b, c: (b, 0, 0, 0)),
            pl.BlockSpec((1, M_HEADS, M_DQK), lambda b, c: (b, 0, 0)),
            pl.BlockSpec((1, 1, M_HEADS), lambda b, c: (b, 0, 0)),
            pl.BlockSpec((1, 8, M_WIDTH), lambda b, c: (b, 0, 0)),
        ],
        out_shape=[
            jax.ShapeDtypeStruct((bsz, tlen, M_WIDTH), BF16),
            jax.ShapeDtypeStruct((bsz, M_HEADS, M_DQK, M_DV), F32),
            jax.ShapeDtypeStruct((bsz, M_HEADS, M_DQK), F32),
            jax.ShapeDtypeStruct((bsz, 1, M_HEADS), F32),
            jax.ShapeDtypeStruct((bsz, 8, M_WIDTH), F32),
        ],
        scratch_shapes=[
            pltpu.VMEM((M_HEADS, M_DQK, M_DV), F32),
            pltpu.VMEM((M_HEADS, M_DQK), F32),
            pltpu.VMEM((1, M_HEADS), F32),
            pltpu.VMEM((8, M_WIDTH), F32),
        ],
        compiler_params=_cparams(("parallel", "arbitrary")),
        name="mlstm",
    )(z, z, z, z, g_rows, conv_buf8, c0, n0, m0, w_conv, b_conv, wq, wk, gn)


OFF_MX = 0
OFF_MV = OFF_MX + M_WIDTH
OFF_MO = OFF_MV + M_WIDTH
OFF_MI = OFF_MO + M_WIDTH
OFF_MF = OFF_MI + M_HEADS
OFF_NQ = OFF_MF + M_HEADS
OFF_NKV = OFF_NQ + N_WIDTH
OFF_NG = OFF_NKV + 6 * KV_W
OFF_BG = OFF_NG + 3 * N_HEADS


def _pack_cols(a):
    lead = a.shape[:-1]
    d_model = a.shape[-1] - OFF_BG
    small = jnp.concatenate([a[..., OFF_MI:OFF_NQ], a[..., OFF_NG:OFF_BG]], axis=-1)
    small = jnp.concatenate([small, jnp.zeros(lead + (LANES - small.shape[-1],), a.dtype)], axis=-1)
    pad = jnp.zeros(lead + (PK_BG - PK_SMALL - LANES,), a.dtype)
    assert d_model == PK_TOTAL - PK_BG
    return jnp.concatenate([a[..., OFF_MX:OFF_MI], a[..., OFF_NQ:OFF_NKV], a[..., OFF_NKV:OFF_NG],
                            small, pad, a[..., OFF_BG:]], axis=-1)


def _dup_halves(x, lane):
    r = pltpu.roll(x, N_HD, 1)
    lo = lane < N_HD
    return jnp.where(lo, x, r), jnp.where(lo, r, x)


def _masked_softmax_rows(s, mask):
    s = jnp.where(mask, s, -jnp.inf)
    mx = jnp.max(s, axis=1, keepdims=True)
    mx = jnp.where(mx > -jnp.inf, mx, 0.0)
    p = jnp.exp(s - mx)
    return p * (1.0 / jnp.maximum(jnp.sum(p, axis=1, keepdims=True), TINY))


def _softmax_pv(s, v):
    m = jnp.max(s, axis=1, keepdims=True)
    p = jnp.exp(s - m)
    return _dot(p.astype(BF16), v) * (1.0 / jnp.sum(p, axis=1, keepdims=True))


def _slc_attend(qxs, kx_ref, v_ref, g, s_sc, nfull, diag_bias, ck):
    rows = qxs[0].shape[0]
    nl = ck // LANES
    streams = range(len(qxs))
    lane_chunks = lambda s: [s[:, a * LANES:(a + 1) * LANES] for a in range(nl)]

    def scores(c):
        kt = kx_ref[g, pl.ds(pl.multiple_of(c * ck, ck), ck), :]
        return [_dot_nt(qx, kt) for qx in qxs]

    def max_pass(c, mvecs):
        out = []
        for n, s in enumerate(scores(c)):
            s_sc[n, c] = s
            out.append(functools.reduce(jnp.maximum, lane_chunks(s), mvecs[n]))
        return tuple(out)

    mvecs = lax.fori_loop(0, nfull, max_pass, tuple(jnp.full((rows, LANES), -jnp.inf, F32) for _ in streams))
    sds = [s + diag_bias for s in scores(nfull)]
    mbs = [jnp.broadcast_to(jnp.max(functools.reduce(jnp.maximum, lane_chunks(sd), mv), axis=1, keepdims=True),
                            (rows, LANES)) for sd, mv in zip(sds, mvecs)]

    def probs(ss, c):
        vt = v_ref[g, pl.ds(pl.multiple_of(c * ck, ck), ck), :]
        out = []
        for s, mb in zip(ss, mbs):
            ps = [jnp.exp(x - mb) for x in lane_chunks(s)]
            out.append((functools.reduce(lambda x, y: x + y, ps), _dot(jnp.concatenate(ps, axis=1).astype(BF16), vt)))
        return tuple(out)

    def exp_pass(c, carry):
        new = probs([s_sc[n, c] for n in streams], c)
        return tuple((carry[n][0] + new[n][0], carry[n][1] + new[n][1]) for n in streams)

    res = lax.fori_loop(0, nfull, exp_pass, probs(sds, nfull))
    return [acc * (1.0 / jnp.sum(lvec, axis=1, keepdims=True)) for lvec, acc in res]


def _select_blocks(imp, qpos, ns, axis):
    blk = lax.broadcasted_iota(jnp.int32, imp.shape, axis)
    cur = qpos // SLC_BLOCK
    allowed = blk <= cur
    forced = (blk == 0) | (blk == cur) | (blk == cur - 1)
    score = jnp.where(allowed, imp + jnp.where(forced, FORCE_BONUS, 0.0), -jnp.inf)
    cnt = jnp.zeros(imp.shape, F32)
    for i in range(ns):
        si = score[:, i:i + 1] if axis == 1 else score[i:i + 1, :]
        ahead = (si > score) | ((si == score) & (blk > i))
        cnt = cnt + jnp.where(ahead, 1.0, 0.0)
    return jnp.where((cnt < float(N_SELECT)) & allowed, 1.0, 0.0)


def _stack_q(q_ref, g, rows):
    qs = jnp.concatenate([q_ref[:, (4 * g + p) * LANES:(4 * g + p + 1) * LANES] for p in range(N_HPG // 2)], axis=0)
    qs = qs * (N_HD ** -0.5)
    lane = lax.broadcasted_iota(jnp.int32, qs.shape, 1)
    qe = jnp.where(lane < N_HD, qs, 0.0).astype(BF16)
    qo = jnp.where(lane < N_HD, 0.0, qs).astype(BF16)
    return qe, qo


def _gate_pair(gs, g, p, j, lane):
    he = g * N_HPG + 2 * p
    ce = SM_NG + 3 * he + j
    co = ce + 3
    return jnp.where(lane < N_HD, gs[:, ce:ce + 1], gs[:, co:co + 1])


def _nsa_prompt_kernel(q_ref, kv_ref, wcmp_ref, bcmp_ref, mmat_ref, o_ref, kvo_ref, wino_ref,
                       ck_sc, cv_sc, kx_sc, sv_sc, wk_sc, wv_sc, stage_sc, s_sc, *, seq, tq, ck, wn):
    i = pl.program_id(1)
    nseg = seq // CMP_STRIDE
    ns = seq // SLC_BLOCK
    npair = N_HPG // 2

    @pl.when(i == 0)
    def _prep():
        kvo_ref[0] = kv_ref[:, 0:4 * KV_W]
        wino_ref[0] = kv_ref[seq - wino_ref.shape[1]:, 4 * KV_W:6 * KV_W]
        kr = lax.broadcasted_iota(jnp.int32, (seq, LANES), 0)
        kc = lax.broadcasted_iota(jnp.int32, (seq, LANES), 1)
        onehot = jnp.where(kr // SLC_BLOCK == kc, 1.0, 0.0).astype(BF16)
        for g in range(N_KV):
            kx_sc[g, :, LANES:] = onehot
        lane = lax.broadcasted_iota(jnp.int32, (nseg, LANES), 1)
        for c, dst in ((0, ck_sc), (1, cv_sc)):
            acc = jnp.zeros((nseg, 2 * LANES), F32)
            stage_sc[...] = kv_ref[:, c * LANES:(c + 1) * LANES]
            for s in range(CMP_STRIDE):
                xs = stage_sc[pl.ds(s, nseg, stride=CMP_STRIDE), :]
                acc = acc + _dot(xs.astype(BF16), wcmp_ref[c, s])
            blocks = acc[:, :LANES] + pltpu.roll(acc[:, LANES:], nseg - 1, 0) + bcmp_ref[c:c + 1, :]
            d0, d1 = _dup_halves(blocks, lane)
            dst[0] = d0.astype(BF16)
            dst[1] = d1.astype(BF16)
        lane_s = lax.broadcasted_iota(jnp.int32, (seq, LANES), 1)
        for off, dst in ((2, kx_sc), (3, sv_sc), (4, wk_sc), (5, wv_sc)):
            d0, d1 = _dup_halves(kv_ref[:, off * LANES:(off + 1) * LANES], lane_s)
            dst[0, :, 0:LANES] = d0.astype(BF16)
            dst[1, :, 0:LANES] = d1.astype(BF16)

    t0 = i * tq
    qpos = t0 + lax.broadcasted_iota(jnp.int32, (tq, 1), 0)
    qpos_row = t0 + lax.broadcasted_iota(jnp.int32, (1, tq), 1)
    nsp = -(-ns // 8) * 8
    lane = lax.broadcasted_iota(jnp.int32, (tq, LANES), 1)
    gs = jax.nn.sigmoid(kv_ref[pl.ds(pl.multiple_of(t0, tq), tq), 6 * LANES:7 * LANES])
    c_end = lax.broadcasted_iota(jnp.int32, (tq, nseg), 1) * CMP_STRIDE + (CMP_LEN - 1)
    cmask = jnp.concatenate([c_end <= qpos] * npair, axis=0)
    lane4 = lax.broadcasted_iota(jnp.int32, (npair * tq, LANES), 1)
    lo4 = lane4 < N_HD

    nfull = t0 // ck
    kpos_d = nfull * ck + lax.broadcasted_iota(jnp.int32, (tq, ck), 1)
    diag_bias = jnp.concatenate([jnp.where(kpos_d <= qpos, 0.0, NEG_BIG)] * npair, axis=0)
    ks = pl.multiple_of(jnp.clip(t0 + tq - wn, 0, seq - wn), LANES)
    dpos = qpos - (ks + lax.broadcasted_iota(jnp.int32, (tq, wn), 1))
    win_bias = jnp.concatenate([jnp.where((dpos >= 0) & (dpos < WINDOW), 0.0, NEG_BIG)] * npair, axis=0)

    for g in range(N_KV):
        qe, qo = _stack_q(q_ref, g, tq)
        ckg = ck_sc[g]
        cvg = cv_sc[g]
        p_e = _masked_softmax_rows(_dot_nt(qe, ckg), cmask)
        p_o = _masked_softmax_rows(_dot_nt(qo, ckg), cmask)
        o_cmp = jnp.where(lo4, _dot(p_e.astype(BF16), cvg), _dot(p_o.astype(BF16), cvg))
        psum = p_e + p_o
        psum = sum(psum[p * tq:(p + 1) * tq] for p in range(npair))
        imp_t = lax.dot_general(mmat_ref[...], psum, (((1,), (1,)), ((), ())),
                                precision=lax.Precision.HIGHEST, preferred_element_type=F32)
        sel_t = _select_blocks(imp_t[:nsp], qpos_row, ns, 0)
        sel = jnp.concatenate([sel_t, jnp.zeros((LANES - nsp, tq), F32)], axis=0).T
        selb = ((sel - 1.0) * (-NEG_BIG)).astype(BF16)
        selb4 = jnp.concatenate([selb] * npair, axis=0)

        o_e, o_o = _slc_attend([jnp.concatenate([qe, selb4], axis=1), jnp.concatenate([qo, selb4], axis=1)],
                               kx_sc, sv_sc, g, s_sc, nfull, diag_bias, ck)
        o_slc = jnp.where(lo4, o_e, o_o)

        kw = wk_sc[g, pl.ds(ks, wn), :]
        vw = wv_sc[g, pl.ds(ks, wn), :]
        o_win = jnp.where(lo4, _softmax_pv(_dot_nt(qe, kw) + win_bias, vw), _softmax_pv(_dot_nt(qo, kw) + win_bias, vw))

        for p in range(npair):
            rs = slice(p * tq, (p + 1) * tq)
            o = (_gate_pair(gs, g, p, 0, lane) * o_cmp[rs] + _gate_pair(gs, g, p, 1, lane) * o_slc[rs]
                 + _gate_pair(gs, g, p, 2, lane) * o_win[rs])
            c0 = (g * npair + p) * LANES
            o_ref[:, c0:c0 + LANES] = o.astype(o_ref.dtype)


def _cmp_to_slc_matrix(nc, ns, rows, cols):
    c0 = np.arange(nc)[:, None] * CMP_STRIDE
    s0 = np.arange(ns)[None, :] * SLC_BLOCK
    ov = np.clip(np.minimum(c0 + CMP_LEN, s0 + SLC_BLOCK) - np.maximum(c0, s0), 0, None) / CMP_LEN
    out = np.zeros((rows, cols), np.float32)
    out[:nc, :ns] = ov
    return jnp.asarray(out)


def _pack_cmp_weights(w_cmp, b_cmp):
    r = CMP_LEN // CMP_STRIDE
    w = w_cmp.reshape(2, r, CMP_STRIDE, N_HD, N_HD)
    eye = jnp.eye(N_KV, dtype=w.dtype)
    bd = jnp.einsum('gh,crsde->crsgdhe', eye, w).reshape(2, r, CMP_STRIDE, KV_W, KV_W)
    wp = jnp.concatenate([bd[:, k] for k in range(r)], axis=-1)
    bp = jnp.concatenate([b_cmp] * N_KV, axis=-1)
    return wp.astype(BF16), bp


def nsa_prompt(z, bsz, seq, wcmp_p, bcmp_p, tq=128, ck=256):
    nseg = seq // CMP_STRIDE
    ns = seq // SLC_BLOCK
    nq = seq // tq
    ck = min(ck, seq)
    wn = min(WINDOW + tq, seq)
    assert seq % ck == 0 and ck % tq == 0 and ns <= LANES and seq % LANES == 0
    mmat = _cmp_to_slc_matrix(nseg - 1, ns, nseg, LANES).T
    wlen = min(WINDOW, seq)
    kern = functools.partial(_nsa_prompt_kernel, seq=seq, tq=tq, ck=ck, wn=wn)
    return pl.pallas_call(
        kern,
        grid=(bsz, nq),
        in_specs=[
            pl.BlockSpec((tq, 1024), lambda b, i: (b * nq + i, PK_NQ // 1024)),
            pl.BlockSpec((seq, 1024), lambda b, i: (b, PK_NKV // 1024)),
            pl.BlockSpec((2, CMP_STRIDE, KV_W, 2 * KV_W), lambda b, i: (0, 0, 0, 0)),
            pl.BlockSpec((2, KV_W), lambda b, i: (0, 0)),
            pl.BlockSpec((LANES, nseg), lambda b, i: (0, 0)),
        ],
        out_specs=[pl.BlockSpec((tq, N_WIDTH), lambda b, i: (b * nq + i, 0)),
                   pl.BlockSpec((1, seq, 4 * KV_W), lambda b, i: (b, 0, 0)),
                   pl.BlockSpec((1, wlen, 2 * KV_W), lambda b, i: (b, 0, 0))],
        out_shape=[jax.ShapeDtypeStruct((bsz * seq, N_WIDTH), BF16),
                   jax.ShapeDtypeStruct((bsz, seq, 4 * KV_W), F32),
                   jax.ShapeDtypeStruct((bsz, wlen, 2 * KV_W), F32)],
        scratch_shapes=[
            pltpu.VMEM((N_KV, nseg, LANES), BF16), pltpu.VMEM((N_KV, nseg, LANES), BF16),
            pltpu.VMEM((N_KV, seq, 2 * LANES), BF16), pltpu.VMEM((N_KV, seq, LANES), BF16),
            pltpu.VMEM((N_KV, seq, LANES), BF16), pltpu.VMEM((N_KV, seq, LANES), BF16),
            pltpu.VMEM((seq, LANES), F32),
            pltpu.VMEM((2, seq // ck, (N_HPG // 2) * tq, ck), F32),
        ],
        compiler_params=_cparams(("parallel", "arbitrary")),
        name="nsa_prompt",
    )(z, z, wcmp_p, bcmp_p, mmat)


def _merge_kernel(xp_ref, xs_ref, hmp_ref, hms_ref, hnp_ref, hns_ref, bgm_ref, bgn_ref,
                  wbm_ref, wbn_ref, wout_ref, n2_ref, wr_ref, br_ref,
                  h_ref, xn2_ref, te_ref, tw_ref, pos_ref, cnt_ref, run_sc, *, nbp):
    i = pl.program_id(0)
    is_p = i < nbp
    x = jnp.where(is_p, xp_ref[...], xs_ref[...])
    hm = jnp.where(is_p, hmp_ref[...], hms_ref[...])
    hn = jnp.where(is_p, hnp_ref[...], hns_ref[...])
    t = (jax.nn.sigmoid(bgm_ref[...]) * _dot(hm, wbm_ref[...])
         + jax.nn.sigmoid(bgn_ref[...]) * _dot(hn, wbn_ref[...]))
    h = x + _dot(t.astype(BF16), wout_ref[...])
    h_ref[...] = h
    xn2 = (h * lax.rsqrt(jnp.mean(h * h, axis=-1, keepdims=True) + EPS) * n2_ref[...]).astype(BF16)
    xn2_ref[...] = xn2

    @pl.when(i == 0)
    def _():
        run_sc[...] = jnp.zeros_like(run_sc)

    logits = _dot(xn2, wr_ref[...]) + br_ref[...]
    lane = lax.broadcasted_iota(jnp.int32, logits.shape, 1)
    cur = jnp.where(lane < N_EXPERTS, logits, -jnp.inf)
    vals, idxs, hots = [], [], []
    for _ in range(TOP_K):
        m = jnp.max(cur, axis=1, keepdims=True)
        idx = jnp.min(jnp.where(cur == m, lane, LANES), axis=1, keepdims=True)
        hot = lane == idx
        vals.append(m)
        idxs.append(idx)
        hots.append(hot)
        cur = jnp.where(hot, -jnp.inf, cur)
    es = [jnp.exp(v - vals[0]) for v in vals]
    inv = 1.0 / functools.reduce(lambda a, b: a + b, es)
    tw_ref[...] = jnp.concatenate([e * inv for e in es], axis=1)
    te_ref[...] = jnp.concatenate(idxs, axis=1)
    cnt = functools.reduce(lambda a, b: a + b, [jnp.where(hot, 1.0, 0.0) for hot in hots])
    incl = _cumsum_rows(cnt)
    before = incl - cnt + run_sc[...]
    pos = [jnp.sum(jnp.where(hot, before, 0.0), axis=1, keepdims=True) for hot in hots]
    pos_ref[...] = jnp.concatenate(pos, axis=1).astype(jnp.int32)
    run_sc[...] = run_sc[...] + incl[incl.shape[0] - 1:, :]
    cnt_ref[...] = run_sc[...]


def merge(xp, xs, hmp, hms, hnp, hns, z, wbm, wbn, wout, n2, wr, br, tm):
    n_p, d = xp.shape
    n = n_p + xs.shape[0]
    nbp = n_p // tm
    row = lambda i: (i, 0)
    prow = lambda i: (jnp.minimum(i, nbp - 1), 0)
    srow = lambda i: (jnp.maximum(i - nbp, 0), 0)
    fixed = lambda i: (0, 0)
    pair = [pl.BlockSpec((tm, d), prow), pl.BlockSpec((tm, d), srow)]
    return pl.pallas_call(
        functools.partial(_merge_kernel, nbp=nbp),
        grid=(n // tm,),
        in_specs=pair * 3 + [
            pl.BlockSpec((tm, d), lambda i: (i, PK_BG // 1024)),
            pl.BlockSpec((tm, d), lambda i: (i, PK_BG // 1024 + 1)),
            pl.BlockSpec((d, d), fixed), pl.BlockSpec((d, d), fixed), pl.BlockSpec((d, d), fixed),
            pl.BlockSpec((1, d), fixed), pl.BlockSpec((d, LANES), fixed), pl.BlockSpec((1, LANES), fixed),
        ],
        out_specs=[pl.BlockSpec((tm, d), row), pl.BlockSpec((tm, d), row),
                   pl.BlockSpec((tm, TOP_K), row), pl.BlockSpec((tm, TOP_K), row), pl.BlockSpec((tm, TOP_K), row),
                   pl.BlockSpec((1, LANES), fixed)],
        out_shape=[jax.ShapeDtypeStruct((n, d), F32), jax.ShapeDtypeStruct((n, d), BF16),
                   jax.ShapeDtypeStruct((n, TOP_K), jnp.int32), jax.ShapeDtypeStruct((n, TOP_K), F32),
                   jax.ShapeDtypeStruct((n, TOP_K), jnp.int32), jax.ShapeDtypeStruct((1, LANES), F32)],
        scratch_shapes=[pltpu.VMEM((1, LANES), F32)],
        compiler_params=_cparams(("arbitrary",)),
        name="merge",
    )(xp, xs, hmp, hms, hnp, hns, z, z, wbm, wbn, wout, n2, wr, br)


def _expert_kernel(te_ref, nu_ref, x_ref, wgu_ref, bgu_ref, wdn_ref, bdn_ref, y_ref, wgu_sc, wdn_sc):
    t = pl.program_id(0)

    @pl.when(t < nu_ref[0])
    def _():
        @pl.when((t == 0) | (te_ref[t] != te_ref[jnp.maximum(t - 1, 0)]))
        def _cast():
            wgu_sc[...] = wgu_ref[0].astype(BF16)
            wdn_sc[...] = wdn_ref[0].astype(BF16)

        de = wdn_sc.shape[0]
        gu = _dot(x_ref[...], wgu_sc[...]) + bgu_ref[0]
        gate = jnp.minimum(gu[:, :de], SWIGLU_LIMIT)
        up = jnp.clip(gu[:, de:], -SWIGLU_LIMIT, SWIGLU_LIMIT)
        glu = gate * jax.nn.sigmoid(gate * SWIGLU_ALPHA)
        act = ((up + 1.0) * glu).astype(BF16)
        y_ref[...] = _dot(act, wdn_sc[...]) + bdn_ref[0]


def expert_ffn(x_sorted, tile_e, n_used, wgu, bgu, wdn, bdn, tm):
    slots, d = x_sorted.shape
    de = wdn.shape[1]
    n_tiles = slots // tm
    rows = lambda t, te, nu: (jnp.minimum(t, nu[0] - 1), 0)
    per_e = lambda t, te, nu: (te[t], 0, 0)
    grid_spec = pltpu.PrefetchScalarGridSpec(
        num_scalar_prefetch=2,
        grid=(n_tiles,),
        in_specs=[
            pl.BlockSpec((tm, d), rows),
            pl.BlockSpec((1, d, 2 * de), per_e), pl.BlockSpec((1, 1, 2 * de), per_e),
            pl.BlockSpec((1, de, d), per_e), pl.BlockSpec((1, 1, d), per_e),
        ],
        out_specs=pl.BlockSpec((tm, d), rows),
        scratch_shapes=[pltpu.VMEM((d, 2 * de), BF16), pltpu.VMEM((de, d), BF16)],
    )
    return pl.pallas_call(
        _expert_kernel,
        grid_spec=grid_spec,
        out_shape=jax.ShapeDtypeStruct((slots, d), F32),
        compiler_params=_cparams(("arbitrary",)),
        name="expert_ffn",
    )(tile_e, n_used, x_sorted, wgu, bgu, wdn, bdn)


def _combine_kernel(h_ref, *refs):
    y_refs = refs[:TOP_K]
    w_ref, g_ref, o_ref = refs[TOP_K:]
    w = w_ref[...]
    moe = w[:, 0:1] * y_refs[0][...]
    for k in range(1, TOP_K):
        moe = moe + w[:, k:k + 1] * y_refs[k][...]
    acc = h_ref[...] + moe
    o_ref[...] = acc * lax.rsqrt(jnp.mean(acc * acc, axis=-1, keepdims=True) + EPS) * g_ref[...]


def combine(h, y4, w4, g, tm, row0, nrows):
    n, d = h.shape
    nb = n // tm
    rb0 = row0 // tm
    return pl.pallas_call(
        _combine_kernel,
        grid=(nrows // tm,),
        in_specs=[pl.BlockSpec((tm, d), lambda i: (rb0 + i, 0))]
        + [pl.BlockSpec((tm, d), functools.partial(lambda i, k: (k * nb + rb0 + i, 0), k=k)) for k in range(TOP_K)]
        + [pl.BlockSpec((tm, TOP_K), lambda i: (rb0 + i, 0)), pl.BlockSpec((1, d), lambda i: (0, 0))],
        out_specs=pl.BlockSpec((tm, d), lambda i: (i, 0)),
        out_shape=jax.ShapeDtypeStruct((nrows, d), F32),
        compiler_params=_cparams(("parallel",)),
        name="combine",
    )(h, *([y4] * TOP_K), w4, g)


def _slot_layout(top_e, pos, counts, tm):
    n = top_e.shape[0]
    nk = n * TOP_K
    padded = (counts + tm - 1) // tm * tm
    pad_end = jnp.cumsum(padded)
    pad_start = pad_end - padded
    grp_start = jnp.cumsum(counts) - counts
    n_tiles = -(-(nk + N_EXPERTS * (tm - 1)) // tm)
    tile_e = jnp.minimum(jnp.sum(pad_end[None, :] <= (jnp.arange(n_tiles) * tm)[:, None], axis=1),
                         N_EXPERTS - 1).astype(jnp.int32)
    hot = top_e[..., None] == jnp.arange(N_EXPERTS)
    inv_slot = (jnp.sum(jnp.where(hot, pad_start, 0), axis=-1) + pos).astype(jnp.int32)
    order = jnp.argsort(top_e.reshape(-1))
    slot = jnp.arange(n_tiles * tm, dtype=jnp.int32)
    slot_e = jnp.repeat(tile_e, tm)
    src = jnp.clip(grp_start[slot_e] + slot - pad_start[slot_e], 0, nk - 1)
    slot_tok = (jnp.take(order, src, mode='clip') // TOP_K).astype(jnp.int32)
    n_used = (pad_end[-1:] // tm).astype(jnp.int32)
    return slot_tok, inv_slot, tile_e, n_used


def _plain_softmax_parts(parts):
    ms = [jnp.where(m, s, NEG_BIG) if m is not None else s for s, m in parts]
    mx = functools.reduce(jnp.maximum, [jnp.max(s, axis=1, keepdims=True) for s in ms])
    ps = [jnp.exp(s - mx) for s in ms]
    den = functools.reduce(lambda a, b: a + b, [jnp.sum(p, axis=1, keepdims=True) for p in ps])
    inv = 1.0 / den
    return [p * inv for p in ps]


def _nsa_sample_kernel(pt_ref, *refs, npages, pps, tlen, page, wlen):
    page_refs = refs[:pps]
    (q_ref, kvn_ref, win_ref, wc_ref, bc_ref, mmat_ref, o_ref,
     stk_sc, stv_sc, kx_sc, v_sc, s_sc) = refs[pps:]
    b = pl.program_id(0)
    j = pl.program_id(1)
    nsteps = npages // pps
    past = npages * page
    nseg = past // CMP_STRIDE
    nblk = past // SLC_BLOCK
    ns = nblk + 1
    rows = N_HEADS * tlen
    grows = N_HPG * tlen

    @pl.when((b == 0) & (j == 0))
    def _onehot():
        r = lax.broadcasted_iota(jnp.int32, (LANES, past), 0)
        c = lax.broadcasted_iota(jnp.int32, (LANES, past), 1)
        kx_sc[LANES:, :] = jnp.where(c // SLC_BLOCK == r, 1.0, 0.0).astype(BF16)

    for k in range(pps):
        r0 = pl.multiple_of((j * pps + k) * page, page)
        pg = page_refs[k]
        stk_sc[pl.ds(r0, page), :] = pg[0, 0:LANES, :].T
        stv_sc[pl.ds(r0, page), :] = pg[0, LANES:2 * LANES, :].T
        kx_sc[0:LANES, pl.ds(r0, page)] = pg[0, 2 * LANES:3 * LANES, :].astype(BF16)
        v_sc[:, pl.ds(r0, page)] = pg[0, 3 * LANES:4 * LANES, :].astype(BF16)

    @pl.when(j == nsteps - 1)
    def _attend():
        lane = lax.broadcasted_iota(jnp.int32, (tlen, LANES), 1)
        lo = lane < N_HD

        def cmp_proj(st_sc, c):
            acc = jnp.zeros((nseg, 2 * LANES), F32)
            for sp in range(CMP_STRIDE // 2):
                xa = st_sc[pl.ds(2 * sp, nseg, stride=CMP_STRIDE), :]
                xb = st_sc[pl.ds(2 * sp + 1, nseg, stride=CMP_STRIDE), :]
                acc = acc + _dot(jnp.concatenate([xa, xb], axis=1).astype(BF16), wc_ref[c, sp])
            blocks = acc[:, :LANES] + pltpu.roll(acc[:, LANES:], nseg - 1, 0) + bc_ref[c:c + 1, :]
            return blocks.astype(BF16)

        ck = cmp_proj(stk_sc, 0)
        cv = cmp_proj(stv_sc, 1)

        qall = q_ref[...] * (N_HD ** -0.5)
        pieces = []
        for h in range(N_HEADS):
            slab = qall[:, (h // 2) * LANES:(h // 2 + 1) * LANES]
            g = h // N_HPG
            if (h % 2) != g:
                slab = pltpu.roll(slab, N_HD, 1)
            pieces.append(jnp.where(lo, slab, 0.0) if g == 0 else jnp.where(lo, 0.0, slab))
        qbd = jnp.concatenate(pieces, axis=0).astype(BF16)
        trow = lax.broadcasted_iota(jnp.int32, (rows, 1), 0) & (tlen - 1)
        qpos = past + trow
        t8 = lax.broadcasted_iota(jnp.int32, (tlen, 1), 0)

        kvn = kvn_ref[...]
        zpad = jnp.zeros((LANES - tlen, LANES), F32)
        newk = lambda off: jnp.concatenate([kvn[:, off * LANES:(off + 1) * LANES], zpad], axis=0).astype(BF16)
        new_lane = lax.broadcasted_iota(jnp.int32, (rows, LANES), 1)
        new_mask = new_lane <= trow

        c_end = lax.broadcasted_iota(jnp.int32, (rows, nseg), 1) * CMP_STRIDE + (CMP_LEN - 1)
        p_cmp = _masked_softmax_rows(_dot_nt(qbd, ck), c_end <= qpos)
        o_cmp = _dot(p_cmp.astype(BF16), cv)

        bias_rows, bias_new = [], []
        for g in range(N_KV):
            psum = sum(p_cmp[g * grows + h * tlen:g * grows + (h + 1) * tlen] for h in range(N_HPG))
            imp = jnp.dot(psum, mmat_ref[...], precision=lax.Precision.HIGHEST, preferred_element_type=F32)
            sel = _select_blocks(imp, past + t8, ns, 1)
            selb = (sel - 1.0) * (-NEG_BIG)
            bias_rows += [selb[:, :LANES]] * N_HPG
            bias_new += [selb[:, nblk:nblk + 1]] * N_HPG
        qx = jnp.concatenate([qbd, jnp.concatenate(bias_rows, axis=0).astype(BF16)], axis=1)
        bias_new = jnp.concatenate(bias_new, axis=0)

        ckeys = min(past, 1024)
        mx = jnp.full((rows, 1), -jnp.inf, F32)
        for c0 in range(0, past, ckeys):
            s = _dot(qx, kx_sc[:, c0:c0 + ckeys])
            s_sc[:, c0:c0 + ckeys] = s
            mx = jnp.maximum(mx, jnp.max(s, axis=1, keepdims=True))
        s_new = jnp.where(new_mask, _dot_nt(qbd, newk(2)) + bias_new, NEG_BIG)
        mx = jnp.maximum(mx, jnp.max(s_new, axis=1, keepdims=True))
        p_new = jnp.exp(s_new - mx)
        den = jnp.sum(p_new, axis=1, keepdims=True)
        acc = _dot(p_new.astype(BF16), newk(3))
        for c0 in range(0, past, ckeys):
            p = jnp.exp(s_sc[:, c0:c0 + ckeys] - mx)
            den = den + jnp.sum(p, axis=1, keepdims=True)
            acc = acc + _dot_nt(p.astype(BF16), v_sc[:, c0:c0 + ckeys])
        o_slc = acc * (1.0 / den)

        wr = lax.broadcasted_iota(jnp.int32, (rows, wlen), 1)
        dpos = trow + wlen - wr
        wk_old = win_ref[0, :, 0:LANES].astype(BF16)
        wv_old = win_ref[0, :, LANES:2 * LANES].astype(BF16)
        pw_old, pw_new = _plain_softmax_parts([(_dot_nt(qbd, wk_old), (dpos >= 0) & (dpos < WINDOW)),
                                               (_dot_nt(qbd, newk(4)), new_mask)])
        o_win = _dot(pw_old.astype(BF16), wv_old) + _dot(pw_new.astype(BF16), newk(5))

        gs = jax.nn.sigmoid(kvn[:, 6 * LANES:7 * LANES])
        for pr in range(N_HEADS // 2):
            halves = []
            for par in range(2):
                h = 2 * pr + par
                g = h // N_HPG
                rs = slice(h * tlen, (h + 1) * tlen)
                c0 = SM_NG + 3 * h
                o = (gs[:, c0:c0 + 1] * o_cmp[rs] + gs[:, c0 + 1:c0 + 2] * o_slc[rs]
                     + gs[:, c0 + 2:c0 + 3] * o_win[rs])
                halves.append(pltpu.roll(o, N_HD, 1) if par != g else o)
            o_ref[0, :, pr * LANES:(pr + 1) * LANES] = jnp.where(lo, halves[0], halves[1])


def nsa_sample(z, row0, page_table, cache_t, layer, win2, wcmp_p, bcmp_p, tlen, pps=8):
    bsz, npages = page_table.shape
    page = cache_t.shape[2]
    wlen = win2.shape[1]
    past = npages * page
    nseg = past // CMP_STRIDE
    nblk = past // SLC_BLOCK
    assert npages % pps == 0 and nblk <= LANES and tlen & (tlen - 1) == 0 and tlen <= SLC_BLOCK and wlen == WINDOW
    mmat = _cmp_to_slc_matrix(nseg - 1, nblk + 1, nseg, 2 * LANES)
    wc2 = wcmp_p.reshape(2, CMP_STRIDE // 2, 2 * KV_W, 2 * KV_W)
    rb0 = row0 // tlen
    kern = functools.partial(_nsa_sample_kernel, npages=npages, pps=pps, tlen=tlen, page=page, wlen=wlen)

    def page_spec(k):
        return pl.BlockSpec((1, 4 * KV_W, page), lambda b, j, pt: (pt[b, j * pps + k], layer, 0))

    grid_spec = pltpu.PrefetchScalarGridSpec(
        num_scalar_prefetch=1,
        grid=(bsz, npages // pps),
        in_specs=[page_spec(k) for k in range(pps)] + [
            pl.BlockSpec((tlen, 1024), lambda b, j, pt: (rb0 + b, PK_NQ // 1024)),
            pl.BlockSpec((tlen, 1024), lambda b, j, pt: (rb0 + b, PK_NKV // 1024)),
            pl.BlockSpec((1, wlen, 2 * KV_W), lambda b, j, pt: (b, 0, 0)),
            pl.BlockSpec((2, CMP_STRIDE // 2, 2 * KV_W, 2 * KV_W), lambda b, j, pt: (0, 0, 0, 0)),
            pl.BlockSpec((2, KV_W), lambda b, j, pt: (0, 0)),
            pl.BlockSpec((nseg, 2 * LANES), lambda b, j, pt: (0, 0)),
        ],
        out_specs=pl.BlockSpec((1, tlen, N_WIDTH), lambda b, j, pt: (b, 0, 0)),
        scratch_shapes=[
            pltpu.VMEM((past, LANES), F32), pltpu.VMEM((past, LANES), F32),
            pltpu.VMEM((2 * LANES, past), BF16), pltpu.VMEM((LANES, past), BF16),
            pltpu.VMEM((N_HEADS * tlen, past), F32),
        ],
    )
    return pl.pallas_call(
        kern,
        grid_spec=grid_spec,
        out_shape=jax.ShapeDtypeStruct((bsz, tlen, N_WIDTH), F32),
        compiler_params=_cparams(("arbitrary", "arbitrary")),
        name="nsa_sample",
    )(page_table, *([cache_t] * pps), z, z, win2, wc2, bcmp_p, mmat)


ROW_TILE = 1024
MERGE_TILE = 512
MOE_TILE = 512
PROMPT_CHUNK = 256


def kernel(x_prompt, x_sample, cache_kv, cache_win_kv, state_conv, state_C, state_n, state_m, page_table,
           norm1_g, w_in, b_in, w_conv, b_conv, w_mq, w_mk, g_mnorm, w_cmp, b_cmp,
           w_branch_m, w_branch_n, w_out, norm2_g, w_router, b_router, w_gu, b_gu, w_dn, b_dn, normf_g):
    bp, sp, d = x_prompt.shape
    bs, ts, _ = x_sample.shape
    depth = w_in.shape[0]
    n_p, n_s = bp * sp, bs * ts
    n = n_p + n_s
    assert n_p % ROW_TILE == 0 and n_s % ROW_TILE == 0 and n_p % MERGE_TILE == 0 and n_s % MERGE_TILE == 0
    assert sp % PROMPT_CHUNK == 0 and ts % 8 == 0 and ts >= M_CONV - 1

    assert depth == 1, "only DEPTH == 1 is supported (the final norm is fused into the combine step)"
    xp2, xs2 = x_prompt.reshape(n_p, d), x_sample.reshape(n_s, d)
    st_p, st_s = [], []
    for l in range(depth):
        w_in_p = _pack_cols(w_in[l]).astype(BF16)
        b_in_p = _pack_cols(b_in[l][None])
        z = norm_matmul(xp2, xs2, norm1_g[l][None], w_in_p, b_in_p, ROW_TILE, 1024)

        wq, wk = w_mq[l].astype(BF16), w_mk[l].astype(BF16)
        gn = g_mnorm[l].reshape(1, M_WIDTH)
        bconv = b_conv[l][None]

        def gate_rows(r0, bsz, tlen):
            g = z[r0:r0 + bsz * tlen, PK_SMALL:PK_SMALL + 2 * M_HEADS]
            g = g.reshape(bsz, tlen, 2 * M_HEADS).transpose(0, 2, 1)
            if tlen < LANES:
                g = jnp.pad(g, ((0, 0), (0, 0), (0, LANES - tlen)))
            return g

        zero = lambda *s: jnp.zeros(s, F32)
        hm_p, c_p, nn_p, m_p, tail_p = mlstm(z, gate_rows(0, bp, sp), 0, bp, sp, PROMPT_CHUNK,
                                             zero(bp, 8, M_WIDTH), zero(bp, M_HEADS, M_DQK, M_DV),
                                             zero(bp, M_HEADS, M_DQK), zero(bp, 1, M_HEADS),
                                             w_conv[l], bconv, wq, wk, gn)
        cbuf_s = jnp.pad(state_conv[l], ((0, 0), (8 - (M_CONV - 1), 0), (0, 0)))
        hm_s, c_s, nn_s, m_s, tail_s = mlstm(z, gate_rows(n_p, bs, ts), n_p, bs, ts, ts,
                                             cbuf_s, state_C[l], state_n[l], state_m[l][:, None, :],
                                             w_conv[l], bconv, wq, wk, gn)

        wcmp_p, bcmp_p = _pack_cmp_weights(w_cmp[l], b_cmp[l])
        hn_p, kv_p, win_p = nsa_prompt(z, bp, sp, wcmp_p, bcmp_p)
        wlen_s = cache_win_kv.shape[2]
        win2 = cache_win_kv[l].reshape(bs, wlen_s, 2 * KV_W)
        cache_t = jnp.transpose(cache_kv, (0, 2, 3, 4, 5, 1)).reshape(cache_kv.shape[0], depth * 4 * KV_W,
                                                                      cache_kv.shape[1])
        hn_s = nsa_sample(z, n_p, page_table, cache_t, l, win2, wcmp_p, bcmp_p, ts)

        wr = jnp.pad(w_router[l], ((0, 0), (0, LANES - N_EXPERTS))).astype(BF16)
        br = jnp.pad(b_router[l], (0, LANES - N_EXPERTS))[None]
        h, xn2, top_e, top_w, pos, counts = merge(
            xp2, xs2, hm_p.reshape(n_p, M_WIDTH), hm_s.reshape(n_s, M_WIDTH),
            hn_p, hn_s.reshape(n_s, N_WIDTH).astype(BF16), z,
            w_branch_m[l].astype(BF16), w_branch_n[l].astype(BF16), w_out[l].astype(BF16),
            norm2_g[l][None], wr, br, MERGE_TILE)

        slot_tok, inv_slot, tile_e, n_used = _slot_layout(top_e, pos, counts[0, :N_EXPERTS].astype(jnp.int32),
                                                          MOE_TILE)
        x_sorted = jnp.take(xn2, slot_tok, axis=0, mode='clip')
        y_slots = expert_ffn(x_sorted, tile_e, n_used, w_gu[l], b_gu[l][:, None, :],
                             w_dn[l], b_dn[l][:, None, :], MOE_TILE)
        y4 = jnp.take(y_slots, inv_slot.T.reshape(-1), axis=0, mode='clip')
        y_prompt = combine(h, y4, top_w, normf_g[None], ROW_TILE, 0, n_p).reshape(bp, sp, d)
        y_sample = combine(h, y4, top_w, normf_g[None], ROW_TILE, n_p, n_s).reshape(bs, ts, d)

        kvw_s = z[n_p:, PK_NKV:PK_NKV + 6 * KV_W].reshape(bs, ts, 6, N_KV, N_HD)
        win_s = jnp.concatenate([cache_win_kv[l][:, ts:], kvw_s[:, :, 4:]], axis=1)
        tail0 = 8 - (M_CONV - 1)
        st_p.append((kv_p.reshape(bp, sp, 4, N_KV, N_HD), win_p.reshape(bp, -1, 2, N_KV, N_HD), tail_p[:, tail0:],
                     c_p, nn_p, m_p[:, 0]))
        st_s.append((kvw_s[:, :, :4], win_s, tail_s[:, tail0:], c_s, nn_s, m_s[:, 0]))

    stack = lambda sts, k, axis=0: jnp.stack([s[k] for s in sts], axis=axis)
    return (y_prompt, y_sample,
            stack(st_p, 0, 2), stack(st_p, 1), stack(st_p, 2), stack(st_p, 3), stack(st_p, 4), stack(st_p, 5),
            stack(st_s, 0, 2), stack(st_s, 1), stack(st_s, 2), stack(st_s, 3), stack(st_s, 4), stack(st_s, 5))
```

```python
import functools
import math

import jax
import jax.numpy as jnp
import numpy as np
from jax import lax
from jax.experimental import pallas as pl
from jax.experimental.pallas import tpu as pltpu

F32 = jnp.float32
BF16 = jnp.bfloat16

M_HEADS = 4
M_DV = 256
M_DQK = 128
M_WIDTH = M_HEADS * M_DV
M_CONV = 4
N_HEADS = 16
N_KV = 2
N_HD = 64
N_HPG = N_HEADS // N_KV
N_WIDTH = N_HEADS * N_HD
KV_W = N_KV * N_HD
CMP_LEN = 32
CMP_STRIDE = 16
SLC_BLOCK = 64
N_SELECT = 16
WINDOW = 512
FORCE_BONUS = 1000.0
N_EXPERTS = 32
TOP_K = 4
SWIGLU_LIMIT = 7.0
SWIGLU_ALPHA = 1.702
EPS = 1e-6
TINY = 1e-30
NEG_BIG = -1e30

LANES = 128
PK_MX = 0
PK_MV = 1024
PK_MO = 2048
PK_NQ = 3072
PK_NKV = 4096
PK_SMALL = PK_NKV + 6 * KV_W
PK_BG = 5120
PK_TOTAL = 7168
SM_IG = 0
SM_LF = M_HEADS
SM_NG = 2 * M_HEADS

VMEM_LIMIT = 56 * 1024 * 1024


def _cparams(sem):
    return pltpu.CompilerParams(dimension_semantics=sem, vmem_limit_bytes=VMEM_LIMIT)


def _log_sigmoid(x):
    return jnp.minimum(x, 0.0) - jnp.log1p(jnp.exp(-jnp.abs(x)))


def _dot(a, b):
    return jnp.dot(a, b, preferred_element_type=F32)


def _dot_nt(a, b):
    return lax.dot_general(a, b, (((1,), (1,)), ((), ())), preferred_element_type=F32)


def _dot_tn(a, b):
    return lax.dot_general(a, b, (((0,), (0,)), ((), ())), preferred_element_type=F32)


def _norm_matmul_kernel(xp_ref, xs_ref, g_ref, w_ref, b_ref, o_ref, xn_ref, *, nbp):
    @pl.when(pl.program_id(1) == 0)
    def _():
        x = jnp.where(pl.program_id(0) < nbp, xp_ref[...], xs_ref[...])
        ms = jnp.mean(x * x, axis=-1, keepdims=True)
        xn_ref[...] = (x * lax.rsqrt(ms + EPS) * g_ref[...]).astype(BF16)

    o_ref[...] = _dot(xn_ref[...], w_ref[...]) + b_ref[...]


def norm_matmul(xp, xs, g, w, b, tm, tn):
    n_p, d = xp.shape
    n = n_p + xs.shape[0]
    nbp = n_p // tm
    nc = w.shape[1]
    return pl.pallas_call(
        functools.partial(_norm_matmul_kernel, nbp=nbp),
        grid=(n // tm, nc // tn),
        in_specs=[
            pl.BlockSpec((tm, d), lambda i, j: (jnp.minimum(i, nbp - 1), 0)),
            pl.BlockSpec((tm, d), lambda i, j: (jnp.maximum(i - nbp, 0), 0)),
            pl.BlockSpec((1, d), lambda i, j: (0, 0)),
            pl.BlockSpec((d, tn), lambda i, j: (0, j)),
            pl.BlockSpec((1, tn), lambda i, j: (0, j)),
        ],
        out_specs=pl.BlockSpec((tm, tn), lambda i, j: (i, j)),
        out_shape=jax.ShapeDtypeStruct((n, nc), F32),
        scratch_shapes=[pltpu.VMEM((tm, d), BF16)],
        compiler_params=_cparams(("parallel", "arbitrary")),
        name="norm_matmul",
    )(xp, xs, g, w, b)


def _shift_rows(x, tail, d):
    rows = x.shape[0]
    xd = pltpu.roll(x, d, 0)
    td = pltpu.roll(tail, d, 0)
    head = jnp.where(lax.broadcasted_iota(jnp.int32, td.shape, 0) < d, td, xd[:8])
    if rows == 8:
        return head
    return jnp.concatenate([head, xd[8:]], axis=0)


def _cumsum_rows(x):
    n = x.shape[0]
    idx = lax.broadcasted_iota(jnp.int32, x.shape, 0)
    k = 1
    while k < n:
        x = x + jnp.where(idx >= k, pltpu.roll(x, k, 0), 0.0)
        k *= 2
    return x


def _cumsum_lanes(x, n):
    idx = lax.broadcasted_iota(jnp.int32, x.shape, 1)
    k = 1
    while k < n:
        x = x + jnp.where(idx >= k, pltpu.roll(x, k, 1), 0.0)
        k *= 2
    return x


def _mlstm_kernel(xm_ref, v_ref, o_ref, gc_ref, gr_ref, cbuf_ref, c0_ref, n0_ref, m0_ref,
                  wconv_ref, bconv_ref, wq_ref, wk_ref, gn_ref,
                  h_ref, cout_ref, nout_ref, mout_ref, tout_ref,
                  c_sc, n_sc, m_sc, tail_sc, *, chunk):
    L = chunk
    c = pl.program_id(1)

    @pl.when(c == 0)
    def _init():
        c_sc[...] = c0_ref[0]
        n_sc[...] = n0_ref[0]
        m_sc[...] = m0_ref[0]
        tail_sc[...] = cbuf_ref[0]

    x = xm_ref[...]
    tail = tail_sc[...]
    wc = wconv_ref[...]
    xc = x * wc[M_CONV - 1:M_CONV] + bconv_ref[...]
    for d in range(1, M_CONV):
        xc = xc + _shift_rows(x, tail, d) * wc[M_CONV - 1 - d:M_CONV - d]
    tail_sc[...] = x[L - 8:]
    xc = xc * jax.nn.sigmoid(xc)
    xcb = xc.astype(BF16)

    gc = gc_ref[...]
    gr = gr_ref[0]
    b_col = _cumsum_rows(_log_sigmoid(gc))
    b_row = _cumsum_lanes(_log_sigmoid(gr), L)
    causal = (lax.broadcasted_iota(jnp.int32, (L, L), 1) <= lax.broadcasted_iota(jnp.int32, (L, L), 0))

    vb = v_ref[...].astype(BF16)
    og = jax.nn.sigmoid(o_ref[...])
    gn = gn_ref[...]
    m_all = m_sc[...]
    scale = M_DQK ** -0.5
    m_new_list = []
    for h in range(M_HEADS):
        hs = slice(h * M_DV, (h + 1) * M_DV)
        q = _dot(xcb[:, hs], wq_ref[h])
        k = _dot(xcb[:, hs], wk_ref[h]) * scale
        qb = q.astype(BF16)
        m_prev = m_all[:, h:h + 1]
        bc = b_col[:, SM_LF + h:SM_LF + h + 1]
        igc = gc[:, SM_IG + h:SM_IG + h + 1]
        br = b_row[SM_LF + h:SM_LF + h + 1, :L]
        igr = gr[SM_IG + h:SM_IG + h + 1, :L]
        a = bc + m_prev
        dmat = jnp.where(causal, bc - br + igr, -jnp.inf)
        mt = jnp.maximum(a, jnp.max(dmat, axis=1, keepdims=True))
        w_intra = jnp.exp(dmat - mt)
        w_inter = jnp.exp(a - mt)
        qk = _dot_nt(qb, k.astype(BF16)) * w_intra
        cmat = c_sc[h]
        nrow = n_sc[h:h + 1, :]
        vh = vb[:, hs]
        num = _dot(qk.astype(BF16), vh) + w_inter * _dot(qb, cmat.astype(BF16))
        den = jnp.sum(qk, axis=1, keepdims=True) + w_inter * jnp.sum(q * nrow, axis=1, keepdims=True)
        hh = num * (1.0 / jnp.maximum(jnp.abs(den), jnp.exp(-mt)))
        m_new = mt[L - 1:L, :]
        w_end = jnp.exp(bc[L - 1:L, :] - bc + igc - m_new)
        decay = jnp.exp(a[L - 1:L, :] - m_new)
        kw = k * w_end
        c_sc[h] = decay * cmat + _dot_tn(kw.astype(BF16), vh)
        n_sc[h:h + 1, :] = decay * nrow + jnp.sum(kw, axis=0, keepdims=True)
        m_new_list.append(m_new)
        hn = hh * lax.rsqrt(jnp.mean(hh * hh, axis=-1, keepdims=True) + EPS) * gn[:, hs]
        h_ref[0, :, hs] = (hn * og[:, hs]).astype(h_ref.dtype)
    m_sc[...] = jnp.concatenate(m_new_list, axis=1)

    @pl.when(c == pl.num_programs(1) - 1)
    def _fin():
        cout_ref[0] = c_sc[...]
        nout_ref[0] = n_sc[...]
        mout_ref[0] = m_sc[...]
        tout_ref[0] = tail_sc[...]


def mlstm(z, g_rows, row0, bsz, tlen, chunk, conv_buf8, c0, n0, m0, w_conv, b_conv, wq, wk, gn):
    L = chunk
    nc = tlen // L
    rb0 = row0 // L
    lr = g_rows.shape[2]
    grl = L if nc > 1 else lr

    def zspec(col):
        return pl.BlockSpec((L, 1024), lambda b, c: (rb0 + b * nc + c, col))

    kern = functools.partial(_mlstm_kernel, chunk=L)
    return pl.pallas_call(
        kern,
        grid=(bsz, nc),
        in_specs=[
            zspec(PK_MX // 1024), zspec(PK_MV // 1024), zspec(PK_MO // 1024),
            pl.BlockSpec((L, LANES), lambda b, c: (rb0 + b * nc + c, PK_SMALL // LANES)),
            pl.BlockSpec((1, 8, grl), lambda b, c: (b, 0, c)),
            pl.BlockSpec((1, 8, M_WIDTH), lambda b, c: (b, 0, 0)),
            pl.BlockSpec((1, M_HEADS, M_DQK, M_DV), lambda b, c: (b, 0, 0, 0)),
            pl.BlockSpec((1, M_HEADS, M_DQK), lambda b, c: (b, 0, 0)),
            pl.BlockSpec((1, 1, M_HEADS), lambda b, c: (b, 0, 0)),
            pl.BlockSpec((M_CONV, M_WIDTH), lambda b, c: (0, 0)),
            pl.BlockSpec((1, M_WIDTH), lambda b, c: (0, 0)),
            pl.BlockSpec((M_HEADS, M_DV, M_DQK), lambda b, c: (0, 0, 0)),
            pl.BlockSpec((M_HEADS, M_DV, M_DQK), lambda b, c: (0, 0, 0)),
            pl.BlockSpec((1, M_WIDTH), lambda b, c: (0, 0)),
        ],
        out_specs=[
            pl.BlockSpec((1, L, M_WIDTH), lambda b, c: (b, c, 0)),
            pl.BlockSpec((1, M_HEADS, M_DQK, M_DV), lambda b, c: (b, 0, 0, 0)),
            pl.BlockSpec((1, M_HEADS, M_DQK), lambda b, c: (b, 0, 0)),
            pl.BlockSpec((1, 1, M_HEADS), lambda b, c: (b, 0, 0)),
            pl.BlockSpec((1, 8, M_WIDTH), lambda b, c: (b, 0, 0)),
        ],
        out_shape=[
            jax.ShapeDtypeStruct((bsz, tlen, M_WIDTH), BF16),
            jax.ShapeDtypeStruct((bsz, M_HEADS, M_DQK, M_DV), F32),
            jax.ShapeDtypeStruct((bsz, M_HEADS, M_DQK), F32),
            jax.ShapeDtypeStruct((bsz, 1, M_HEADS), F32),
            jax.ShapeDtypeStruct((bsz, 8, M_WIDTH), F32),
        ],
        scratch_shapes=[
            pltpu.VMEM((M_HEADS, M_DQK, M_DV), F32),
            pltpu.VMEM((M_HEADS, M_DQK), F32),
            pltpu.VMEM((1, M_HEADS), F32),
            pltpu.VMEM((8, M_WIDTH), F32),
        ],
        compiler_params=_cparams(("parallel", "arbitrary")),
        name="mlstm",
    )(z, z, z, z, g_rows, conv_buf8, c0, n0, m0, w_conv, b_conv, wq, wk, gn)


OFF_MX = 0
OFF_MV = OFF_MX + M_WIDTH
OFF_MO = OFF_MV + M_WIDTH
OFF_MI = OFF_MO + M_WIDTH
OFF_MF = OFF_MI + M_HEADS
OFF_NQ = OFF_MF + M_HEADS
OFF_NKV = OFF_NQ + N_WIDTH
OFF_NG = OFF_NKV + 6 * KV_W
OFF_BG = OFF_NG + 3 * N_HEADS


def _pack_cols(a):
    lead = a.shape[:-1]
    d_model = a.shape[-1] - OFF_BG
    small = jnp.concatenate([a[..., OFF_MI:OFF_NQ], a[..., OFF_NG:OFF_BG]], axis=-1)
    small = jnp.concatenate([small, jnp.zeros(lead + (LANES - small.shape[-1],), a.dtype)], axis=-1)
    pad = jnp.zeros(lead + (PK_BG - PK_SMALL - LANES,), a.dtype)
    assert d_model == PK_TOTAL - PK_BG
    return jnp.concatenate([a[..., OFF_MX:OFF_MI], a[..., OFF_NQ:OFF_NKV], a[..., OFF_NKV:OFF_NG],
                            small, pad, a[..., OFF_BG:]], axis=-1)


def _dup_halves(x, lane):
    r = pltpu.roll(x, N_HD, 1)
    lo = lane < N_HD
    return jnp.where(lo, x, r), jnp.where(lo, r, x)


def _masked_softmax_rows(s, mask):
    s = jnp.where(mask, s, -jnp.inf)
    mx = jnp.max(s, axis=1, keepdims=True)
    mx = jnp.where(mx > -jnp.inf, mx, 0.0)
    p = jnp.exp(s - mx)
    return p * (1.0 / jnp.maximum(jnp.sum(p, axis=1, keepdims=True), TINY))


def _softmax_pv(s, v):
    m = jnp.max(s, axis=1, keepdims=True)
    p = jnp.exp(s - m)
    return _dot(p.astype(BF16), v) * (1.0 / jnp.sum(p, axis=1, keepdims=True))


def _slc_attend(qxs, kx_ref, v_ref, g, s_sc, nfull, diag_bias, ck):
    rows = qxs[0].shape[0]
    nl = ck // LANES
    streams = range(len(qxs))
    lane_chunks = lambda s: [s[:, a * LANES:(a + 1) * LANES] for a in range(nl)]

    def scores(c):
        kt = kx_ref[g, pl.ds(pl.multiple_of(c * ck, ck), ck), :]
        return [_dot_nt(qx, kt) for qx in qxs]

    def max_pass(c, mvecs):
        out = []
        for n, s in enumerate(scores(c)):
            s_sc[n, c] = s
            out.append(functools.reduce(jnp.maximum, lane_chunks(s), mvecs[n]))
        return tuple(out)

    mvecs = lax.fori_loop(0, nfull, max_pass, tuple(jnp.full((rows, LANES), -jnp.inf, F32) for _ in streams))
    sds = [s + diag_bias for s in scores(nfull)]
    mbs = [jnp.broadcast_to(jnp.max(functools.reduce(jnp.maximum, lane_chunks(sd), mv), axis=1, keepdims=True),
                            (rows, LANES)) for sd, mv in zip(sds, mvecs)]

    def probs(ss, c):
        vt = v_ref[g, pl.ds(pl.multiple_of(c * ck, ck), ck), :]
        out = []
        for s, mb in zip(ss, mbs):
            ps = [jnp.exp(x - mb) for x in lane_chunks(s)]
            out.append((functools.reduce(lambda x, y: x + y, ps), _dot(jnp.concatenate(ps, axis=1).astype(BF16), vt)))
        return tuple(out)

    def exp_pass(c, carry):
        new = probs([s_sc[n, c] for n in streams], c)
        return tuple((carry[n][0] + new[n][0], carry[n][1] + new[n][1]) for n in streams)

    res = lax.fori_loop(0, nfull, exp_pass, probs(sds, nfull))
    return [acc * (1.0 / jnp.sum(lvec, axis=1, keepdims=True)) for lvec, acc in res]


def _select_blocks(imp, qpos, ns, axis):
    blk = lax.broadcasted_iota(jnp.int32, imp.shape, axis)
    cur = qpos // SLC_BLOCK
    allowed = blk <= cur
    forced = (blk == 0) | (blk == cur) | (blk == cur - 1)
    score = jnp.where(allowed, imp + jnp.where(forced, FORCE_BONUS, 0.0), -jnp.inf)
    cnt = jnp.zeros(imp.shape, F32)
    for i in range(ns):
        si = score[:, i:i + 1] if axis == 1 else score[i:i + 1, :]
        ahead = (si > score) | ((si == score) & (blk > i))
        cnt = cnt + jnp.where(ahead, 1.0, 0.0)
    return jnp.where((cnt < float(N_SELECT)) & allowed, 1.0, 0.0)


def _stack_q(q_ref, g, rows):
    qs = jnp.concatenate([q_ref[:, (4 * g + p) * LANES:(4 * g + p + 1) * LANES] for p in range(N_HPG // 2)], axis=0)
    qs = qs * (N_HD ** -0.5)
    lane = lax.broadcasted_iota(jnp.int32, qs.shape, 1)
    qe = jnp.where(lane < N_HD, qs, 0.0).astype(BF16)
    qo = jnp.where(lane < N_HD, 0.0, qs).astype(BF16)
    return qe, qo


def _gate_pair(gs, g, p, j, lane):
    he = g * N_HPG + 2 * p
    ce = SM_NG + 3 * he + j
    co = ce + 3
    return jnp.where(lane < N_HD, gs[:, ce:ce + 1], gs[:, co:co + 1])


def _nsa_prompt_kernel(q_ref, kv_ref, wcmp_ref, bcmp_ref, mmat_ref, o_ref, kvo_ref, wino_ref,
                       ck_sc, cv_sc, kx_sc, sv_sc, wk_sc, wv_sc, stage_sc, s_sc, *, seq, tq, ck, wn):
    i = pl.program_id(1)
    nseg = seq // CMP_STRIDE
    ns = seq // SLC_BLOCK
    npair = N_HPG // 2

    @pl.when(i == 0)
    def _prep():
        kvo_ref[0] = kv_ref[:, 0:4 * KV_W]
        wino_ref[0] = kv_ref[seq - wino_ref.shape[1]:, 4 * KV_W:6 * KV_W]
        kr = lax.broadcasted_iota(jnp.int32, (seq, LANES), 0)
        kc = lax.broadcasted_iota(jnp.int32, (seq, LANES), 1)
        onehot = jnp.where(kr // SLC_BLOCK == kc, 1.0, 0.0).astype(BF16)
        for g in range(N_KV):
            kx_sc[g, :, LANES:] = onehot
        lane = lax.broadcasted_iota(jnp.int32, (nseg, LANES), 1)
        for c, dst in ((0, ck_sc), (1, cv_sc)):
            acc = jnp.zeros((nseg, 2 * LANES), F32)
            stage_sc[...] = kv_ref[:, c * LANES:(c + 1) * LANES]
            for s in range(CMP_STRIDE):
                xs = stage_sc[pl.ds(s, nseg, stride=CMP_STRIDE), :]
                acc = acc + _dot(xs.astype(BF16), wcmp_ref[c, s])
            blocks = acc[:, :LANES] + pltpu.roll(acc[:, LANES:], nseg - 1, 0) + bcmp_ref[c:c + 1, :]
            d0, d1 = _dup_halves(blocks, lane)
            dst[0] = d0.astype(BF16)
            dst[1] = d1.astype(BF16)
        lane_s = lax.broadcasted_iota(jnp.int32, (seq, LANES), 1)
        for off, dst in ((2, kx_sc), (3, sv_sc), (4, wk_sc), (5, wv_sc)):
            d0, d1 = _dup_halves(kv_ref[:, off * LANES:(off + 1) * LANES], lane_s)
            dst[0, :, 0:LANES] = d0.astype(BF16)
            dst[1, :, 0:LANES] = d1.astype(BF16)

    t0 = i * tq
    qpos = t0 + lax.broadcasted_iota(jnp.int32, (tq, 1), 0)
    qpos_row = t0 + lax.broadcasted_iota(jnp.int32, (1, tq), 1)
    nsp = -(-ns // 8) * 8
    lane = lax.broadcasted_iota(jnp.int32, (tq, LANES), 1)
    gs = jax.nn.sigmoid(kv_ref[pl.ds(pl.multiple_of(t0, tq), tq), 6 * LANES:7 * LANES])
    c_end = lax.broadcasted_iota(jnp.int32, (tq, nseg), 1) * CMP_STRIDE + (CMP_LEN - 1)
    cmask = jnp.concatenate([c_end <= qpos] * npair, axis=0)
    lane4 = lax.broadcasted_iota(jnp.int32, (npair * tq, LANES), 1)
    lo4 = lane4 < N_HD

    nfull = t0 // ck
    kpos_d = nfull * ck + lax.broadcasted_iota(jnp.int32, (tq, ck), 1)
    diag_bias = jnp.concatenate([jnp.where(kpos_d <= qpos, 0.0, NEG_BIG)] * npair, axis=0)
    ks = pl.multiple_of(jnp.clip(t0 + tq - wn, 0, seq - wn), LANES)
    dpos = qpos - (ks + lax.broadcasted_iota(jnp.int32, (tq, wn), 1))
    win_bias = jnp.concatenate([jnp.where((dpos >= 0) & (dpos < WINDOW), 0.0, NEG_BIG)] * npair, axis=0)

    for g in range(N_KV):
        qe, qo = _stack_q(q_ref, g, tq)
        ckg = ck_sc[g]
        cvg = cv_sc[g]
        p_e = _masked_softmax_rows(_dot_nt(qe, ckg), cmask)
        p_o = _masked_softmax_rows(_dot_nt(qo, ckg), cmask)
        o_cmp = jnp.where(lo4, _dot(p_e.astype(BF16), cvg), _dot(p_o.astype(BF16), cvg))
        psum = p_e + p_o
        psum = sum(psum[p * tq:(p + 1) * tq] for p in range(npair))
        imp_t = lax.dot_general(mmat_ref[...], psum, (((1,), (1,)), ((), ())),
                                precision=lax.Precision.HIGHEST, preferred_element_type=F32)
        sel_t = _select_blocks(imp_t[:nsp], qpos_row, ns, 0)
        sel = jnp.concatenate([sel_t, jnp.zeros((LANES - nsp, tq), F32)], axis=0).T
        selb = ((sel - 1.0) * (-NEG_BIG)).astype(BF16)
        selb4 = jnp.concatenate([selb] * npair, axis=0)

        o_e, o_o = _slc_attend([jnp.concatenate([qe, selb4], axis=1), jnp.concatenate([qo, selb4], axis=1)],
                               kx_sc, sv_sc, g, s_sc, nfull, diag_bias, ck)
        o_slc = jnp.where(lo4, o_e, o_o)

        kw = wk_sc[g, pl.ds(ks, wn), :]
        vw = wv_sc[g, pl.ds(ks, wn), :]
        o_win = jnp.where(lo4, _softmax_pv(_dot_nt(qe, kw) + win_bias, vw), _softmax_pv(_dot_nt(qo, kw) + win_bias, vw))

        for p in range(npair):
            rs = slice(p * tq, (p + 1) * tq)
            o = (_gate_pair(gs, g, p, 0, lane) * o_cmp[rs] + _gate_pair(gs, g, p, 1, lane) * o_slc[rs]
                 + _gate_pair(gs, g, p, 2, lane) * o_win[rs])
            c0 = (g * npair + p) * LANES
            o_ref[:, c0:c0 + LANES] = o.astype(o_ref.dtype)


def _cmp_to_slc_matrix(nc, ns, rows, cols):
    c0 = np.arange(nc)[:, None] * CMP_STRIDE
    s0 = np.arange(ns)[None, :] * SLC_BLOCK
    ov = np.clip(np.minimum(c0 + CMP_LEN, s0 + SLC_BLOCK) - np.maximum(c0, s0), 0, None) / CMP_LEN
    out = np.zeros((rows, cols), np.float32)
    out[:nc, :ns] = ov
    return jnp.asarray(out)


def _pack_cmp_weights(w_cmp, b_cmp):
    r = CMP_LEN // CMP_STRIDE
    w = w_cmp.reshape(2, r, CMP_STRIDE, N_HD, N_HD)
    eye = jnp.eye(N_KV, dtype=w.dtype)
    bd = jnp.einsum('gh,crsde->crsgdhe', eye, w).reshape(2, r, CMP_STRIDE, KV_W, KV_W)
    wp = jnp.concatenate([bd[:, k] for k in range(r)], axis=-1)
    bp = jnp.concatenate([b_cmp] * N_KV, axis=-1)
    return wp.astype(BF16), bp


def nsa_prompt(z, bsz, seq, wcmp_p, bcmp_p, tq=256, ck=256):
    nseg = seq // CMP_STRIDE
    ns = seq // SLC_BLOCK
    nq = seq // tq
    ck = min(ck, seq)
    wn = min(WINDOW + tq, seq)
    assert seq % ck == 0 and ck % tq == 0 and ns <= LANES and seq % LANES == 0
    mmat = _cmp_to_slc_matrix(nseg - 1, ns, nseg, LANES).T
    wlen = min(WINDOW, seq)
    once = pl.Buffered(1)
    kern = functools.partial(_nsa_prompt_kernel, seq=seq, tq=tq, ck=ck, wn=wn)
    return pl.pallas_call(
        kern,
        grid=(bsz, nq),
        in_specs=[
            pl.BlockSpec((tq, 1024), lambda b, i: (b * nq + i, PK_NQ // 1024)),
            pl.BlockSpec((seq, 1024), lambda b, i: (b, PK_NKV // 1024), pipeline_mode=once),
            pl.BlockSpec((2, CMP_STRIDE, KV_W, 2 * KV_W), lambda b, i: (0, 0, 0, 0)),
            pl.BlockSpec((2, KV_W), lambda b, i: (0, 0)),
            pl.BlockSpec((LANES, nseg), lambda b, i: (0, 0)),
        ],
        out_specs=[pl.BlockSpec((tq, N_WIDTH), lambda b, i: (b * nq + i, 0)),
                   pl.BlockSpec((1, seq, 4 * KV_W), lambda b, i: (b, 0, 0), pipeline_mode=once),
                   pl.BlockSpec((1, wlen, 2 * KV_W), lambda b, i: (b, 0, 0), pipeline_mode=once)],
        out_shape=[jax.ShapeDtypeStruct((bsz * seq, N_WIDTH), BF16),
                   jax.ShapeDtypeStruct((bsz, seq, 4 * KV_W), F32),
                   jax.ShapeDtypeStruct((bsz, wlen, 2 * KV_W), F32)],
        scratch_shapes=[
            pltpu.VMEM((N_KV, nseg, LANES), BF16), pltpu.VMEM((N_KV, nseg, LANES), BF16),
            pltpu.VMEM((N_KV, seq, 2 * LANES), BF16), pltpu.VMEM((N_KV, seq, LANES), BF16),
            pltpu.VMEM((N_KV, seq, LANES), BF16), pltpu.VMEM((N_KV, seq, LANES), BF16),
            pltpu.VMEM((seq, LANES), F32),
            pltpu.VMEM((2, seq // ck, (N_HPG // 2) * tq, ck), F32),
        ],
        compiler_params=_cparams(("parallel", "arbitrary")),
        name="nsa_prompt",
    )(z, z, wcmp_p, bcmp_p, mmat)


def _merge_kernel(xp_ref, xs_ref, hmp_ref, hms_ref, hnp_ref, hns_ref, bgm_ref, bgn_ref,
                  wbm_ref, wbn_ref, wout_ref, n2_ref, wr_ref, br_ref,
                  h_ref, xn2_ref, te_ref, tw_ref, pos_ref, cnt_ref, run_sc, *, nbp):
    i = pl.program_id(0)
    is_p = i < nbp
    x = jnp.where(is_p, xp_ref[...], xs_ref[...])
    hm = jnp.where(is_p, hmp_ref[...], hms_ref[...])
    hn = jnp.where(is_p, hnp_ref[...], hns_ref[...])
    t = (jax.nn.sigmoid(bgm_ref[...]) * _dot(hm, wbm_ref[...])
         + jax.nn.sigmoid(bgn_ref[...]) * _dot(hn, wbn_ref[...]))
    h = x + _dot(t.astype(BF16), wout_ref[...])
    h_ref[...] = h
    xn2 = (h * lax.rsqrt(jnp.mean(h * h, axis=-1, keepdims=True) + EPS) * n2_ref[...]).astype(BF16)
    xn2_ref[...] = xn2

    @pl.when(i == 0)
    def _():
        run_sc[...] = jnp.zeros_like(run_sc)

    logits = _dot(xn2, wr_ref[...]) + br_ref[...]
    lane = lax.broadcasted_iota(jnp.int32, logits.shape, 1)
    cur = jnp.where(lane < N_EXPERTS, logits, -jnp.inf)
    vals, idxs, hots = [], [], []
    for _ in range(TOP_K):
        m = jnp.max(cur, axis=1, keepdims=True)
        idx = jnp.min(jnp.where(cur == m, lane, LANES), axis=1, keepdims=True)
        hot = lane == idx
        vals.append(m)
        idxs.append(idx)
        hots.append(hot)
        cur = jnp.where(hot, -jnp.inf, cur)
    es = [jnp.exp(v - vals[0]) for v in vals]
    inv = 1.0 / functools.reduce(lambda a, b: a + b, es)
    tw_ref[...] = jnp.concatenate([e * inv for e in es], axis=1)
    te_ref[...] = jnp.concatenate(idxs, axis=1)
    cnt = functools.reduce(lambda a, b: a + b, [jnp.where(hot, 1.0, 0.0) for hot in hots])
    incl = _cumsum_rows(cnt)
    before = incl - cnt + run_sc[...]
    pos = [jnp.sum(jnp.where(hot, before, 0.0), axis=1, keepdims=True) for hot in hots]
    pos_ref[...] = jnp.concatenate(pos, axis=1).astype(jnp.int32)
    run_sc[...] = run_sc[...] + incl[incl.shape[0] - 1:, :]
    cnt_ref[...] = run_sc[...]


def merge(xp, xs, hmp, hms, hnp, hns, z, wbm, wbn, wout, n2, wr, br, tm):
    n_p, d = xp.shape
    n = n_p + xs.shape[0]
    nbp = n_p // tm
    row = lambda i: (i, 0)
    prow = lambda i: (jnp.minimum(i, nbp - 1), 0)
    srow = lambda i: (jnp.maximum(i - nbp, 0), 0)
    fixed = lambda i: (0, 0)
    pair = [pl.BlockSpec((tm, d), prow), pl.BlockSpec((tm, d), srow)]
    return pl.pallas_call(
        functools.partial(_merge_kernel, nbp=nbp),
        grid=(n // tm,),
        in_specs=pair * 3 + [
            pl.BlockSpec((tm, d), lambda i: (i, PK_BG // 1024)),
            pl.BlockSpec((tm, d), lambda i: (i, PK_BG // 1024 + 1)),
            pl.BlockSpec((d, d), fixed), pl.BlockSpec((d, d), fixed), pl.BlockSpec((d, d), fixed),
            pl.BlockSpec((1, d), fixed), pl.BlockSpec((d, LANES), fixed), pl.BlockSpec((1, LANES), fixed),
        ],
        out_specs=[pl.BlockSpec((tm, d), row), pl.BlockSpec((tm, d), row),
                   pl.BlockSpec((tm, TOP_K), row), pl.BlockSpec((tm, TOP_K), row), pl.BlockSpec((tm, TOP_K), row),
                   pl.BlockSpec((1, LANES), fixed)],
        out_shape=[jax.ShapeDtypeStruct((n, d), F32), jax.ShapeDtypeStruct((n, d), BF16),
                   jax.ShapeDtypeStruct((n, TOP_K), jnp.int32), jax.ShapeDtypeStruct((n, TOP_K), F32),
                   jax.ShapeDtypeStruct((n, TOP_K), jnp.int32), jax.ShapeDtypeStruct((1, LANES), F32)],
        scratch_shapes=[pltpu.VMEM((1, LANES), F32)],
        compiler_params=_cparams(("arbitrary",)),
        name="merge",
    )(xp, xs, hmp, hms, hnp, hns, z, z, wbm, wbn, wout, n2, wr, br)


def _expert_kernel(te_ref, nu_ref, x_ref, wgu_ref, bgu_ref, wdn_ref, bdn_ref, *rest, tile0):
    y_ref, wgu_sc, wdn_sc = rest[-3:]
    t = pl.program_id(0)
    tg = tile0 + t

    @pl.when(tg < nu_ref[0])
    def _():
        @pl.when((t == 0) | (te_ref[tg] != te_ref[jnp.maximum(tg - 1, 0)]))
        def _cast():
            wgu_sc[...] = wgu_ref[0].astype(BF16)
            wdn_sc[...] = wdn_ref[0].astype(BF16)

        de = wdn_sc.shape[0]
        gu = _dot(x_ref[...], wgu_sc[...]) + bgu_ref[0]
        gate = jnp.minimum(gu[:, :de], SWIGLU_LIMIT)
        up = jnp.clip(gu[:, de:], -SWIGLU_LIMIT, SWIGLU_LIMIT)
        glu = gate * jax.nn.sigmoid(gate * SWIGLU_ALPHA)
        act = ((up + 1.0) * glu).astype(BF16)
        y_ref[...] = _dot(act, wdn_sc[...]) + bdn_ref[0]


def expert_ffn(x_part, tile0, tile_e, n_used, y_prev, wgu, bgu, wdn, bdn, tm):
    d = x_part.shape[1]
    de = wdn.shape[1]
    n_part = x_part.shape[0] // tm
    n_all = tile_e.shape[0]

    def local(t, nu):
        return jnp.clip(jnp.minimum(tile0 + t, nu[0] - 1) - tile0, 0, n_part - 1)

    per_e = lambda t, te, nu: (te[tile0 + t], 0, 0)
    in_specs = [
        pl.BlockSpec((tm, d), lambda t, te, nu: (local(t, nu), 0)),
        pl.BlockSpec((1, d, 2 * de), per_e), pl.BlockSpec((1, 1, 2 * de), per_e),
        pl.BlockSpec((1, de, d), per_e), pl.BlockSpec((1, 1, d), per_e),
    ]
    operands = [tile_e, n_used, x_part, wgu, bgu, wdn, bdn]
    aliases = {}
    if y_prev is not None:
        in_specs.append(pl.BlockSpec(memory_space=pl.ANY))
        aliases = {len(operands): 0}
        operands.append(y_prev)
    grid_spec = pltpu.PrefetchScalarGridSpec(
        num_scalar_prefetch=2,
        grid=(n_part,),
        in_specs=in_specs,
        out_specs=pl.BlockSpec((tm, d), lambda t, te, nu: (tile0 + local(t, nu), 0)),
        scratch_shapes=[pltpu.VMEM((d, 2 * de), BF16), pltpu.VMEM((de, d), BF16)],
    )
    return pl.pallas_call(
        functools.partial(_expert_kernel, tile0=tile0),
        grid_spec=grid_spec,
        out_shape=jax.ShapeDtypeStruct((n_all * tm, d), F32),
        input_output_aliases=aliases,
        compiler_params=_cparams(("arbitrary",)),
        name="expert_ffn",
    )(*operands)


def _combine_kernel(h_ref, *refs):
    y_refs = refs[:TOP_K]
    w_ref, g_ref = refs[TOP_K:TOP_K + 2]
    o_ref = refs[-1]
    w = w_ref[...]
    moe = w[:, 0:1] * y_refs[0][...]
    for k in range(1, TOP_K):
        moe = moe + w[:, k:k + 1] * y_refs[k][...]
    acc = h_ref[...] + moe
    o_ref[...] = acc * lax.rsqrt(jnp.mean(acc * acc, axis=-1, keepdims=True) + EPS) * g_ref[...]


def combine(h, y4, w4, g, tm, row0, out_row0, out_rows, out_prev):
    d = h.shape[1]
    nrows = y4.shape[0] // TOP_K
    nb = nrows // tm
    rb0 = row0 // tm
    ob0 = out_row0 // tm
    in_specs = ([pl.BlockSpec((tm, d), lambda i: (rb0 + i, 0))]
                + [pl.BlockSpec((tm, d), functools.partial(lambda i, k: (k * nb + i, 0), k=k)) for k in range(TOP_K)]
                + [pl.BlockSpec((tm, TOP_K), lambda i: (rb0 + i, 0)), pl.BlockSpec((1, d), lambda i: (0, 0))])
    operands = [h] + [y4] * TOP_K + [w4, g]
    aliases = {}
    if out_prev is not None:
        in_specs.append(pl.BlockSpec(memory_space=pl.ANY))
        aliases = {len(operands): 0}
        operands.append(out_prev)
    return pl.pallas_call(
        _combine_kernel,
        grid=(nb,),
        in_specs=in_specs,
        out_specs=pl.BlockSpec((tm, d), lambda i: (ob0 + i, 0)),
        out_shape=jax.ShapeDtypeStruct((out_rows, d), F32),
        input_output_aliases=aliases,
        compiler_params=_cparams(("parallel",)),
        name="combine",
    )(*operands)


def _slot_layout(top_e, pos, counts, tm):
    n = top_e.shape[0]
    nk = n * TOP_K
    padded = (counts + tm - 1) // tm * tm
    pad_end = jnp.cumsum(padded)
    pad_start = pad_end - padded
    grp_start = jnp.cumsum(counts) - counts
    n_tiles = -(-(nk + N_EXPERTS * (tm - 1)) // tm)
    tile_e = jnp.minimum(jnp.sum(pad_end[None, :] <= (jnp.arange(n_tiles) * tm)[:, None], axis=1),
                         N_EXPERTS - 1).astype(jnp.int32)
    hot = top_e[..., None] == jnp.arange(N_EXPERTS)
    inv_slot = (jnp.sum(jnp.where(hot, pad_start, 0), axis=-1) + pos).astype(jnp.int32)
    order = jnp.argsort(top_e.reshape(-1))
    slot = jnp.arange(n_tiles * tm, dtype=jnp.int32)
    slot_e = jnp.repeat(tile_e, tm)
    src = jnp.clip(grp_start[slot_e] + slot - pad_start[slot_e], 0, nk - 1)
    slot_tok = (jnp.take(order, src, mode='clip') // TOP_K).astype(jnp.int32)
    n_used = (pad_end[-1:] // tm).astype(jnp.int32)
    return slot_tok, inv_slot, tile_e, n_used


def _plain_softmax_parts(parts):
    ms = [jnp.where(m, s, NEG_BIG) if m is not None else s for s, m in parts]
    mx = functools.reduce(jnp.maximum, [jnp.max(s, axis=1, keepdims=True) for s in ms])
    ps = [jnp.exp(s - mx) for s in ms]
    den = functools.reduce(lambda a, b: a + b, [jnp.sum(p, axis=1, keepdims=True) for p in ps])
    inv = 1.0 / den
    return [p * inv for p in ps]


def _nsa_sample_kernel(pt_ref, *refs, npages, pps, tlen, page, wlen):
    page_refs = refs[:pps]
    (q_ref, kvn_ref, win_ref, wc_ref, bc_ref, mmat_ref, o_ref,
     stk_sc, stv_sc, kx_sc, v_sc, s_sc) = refs[pps:]
    b = pl.program_id(0)
    j = pl.program_id(1)
    nsteps = npages // pps
    past = npages * page
    nseg = past // CMP_STRIDE
    nblk = past // SLC_BLOCK
    ns = nblk + 1
    rows = N_HEADS * tlen
    grows = N_HPG * tlen

    @pl.when((b == 0) & (j == 0))
    def _onehot():
        r = lax.broadcasted_iota(jnp.int32, (LANES, past), 0)
        c = lax.broadcasted_iota(jnp.int32, (LANES, past), 1)
        kx_sc[LANES:, :] = jnp.where(c // SLC_BLOCK == r, 1.0, 0.0).astype(BF16)

    for k in range(pps):
        r0 = pl.multiple_of((j * pps + k) * page, page)
        pg = page_refs[k]
        stk_sc[pl.ds(r0, page), :] = pg[0, 0:LANES, :].T
        stv_sc[pl.ds(r0, page), :] = pg[0, LANES:2 * LANES, :].T
        kx_sc[0:LANES, pl.ds(r0, page)] = pg[0, 2 * LANES:3 * LANES, :].astype(BF16)
        v_sc[:, pl.ds(r0, page)] = pg[0, 3 * LANES:4 * LANES, :].astype(BF16)

    @pl.when(j == nsteps - 1)
    def _attend():
        lane = lax.broadcasted_iota(jnp.int32, (tlen, LANES), 1)
        lo = lane < N_HD

        def cmp_proj(st_sc, c):
            acc = jnp.zeros((nseg, 2 * LANES), F32)
            for sp in range(CMP_STRIDE // 2):
                xa = st_sc[pl.ds(2 * sp, nseg, stride=CMP_STRIDE), :]
                xb = st_sc[pl.ds(2 * sp + 1, nseg, stride=CMP_STRIDE), :]
                acc = acc + _dot(jnp.concatenate([xa, xb], axis=1).astype(BF16), wc_ref[c, sp])
            blocks = acc[:, :LANES] + pltpu.roll(acc[:, LANES:], nseg - 1, 0) + bc_ref[c:c + 1, :]
            return blocks.astype(BF16)

        ck = cmp_proj(stk_sc, 0)
        cv = cmp_proj(stv_sc, 1)

        qall = q_ref[...] * (N_HD ** -0.5)
        pieces = []
        for h in range(N_HEADS):
            slab = qall[:, (h // 2) * LANES:(h // 2 + 1) * LANES]
            g = h // N_HPG
            if (h % 2) != g:
                slab = pltpu.roll(slab, N_HD, 1)
            pieces.append(jnp.where(lo, slab, 0.0) if g == 0 else jnp.where(lo, 0.0, slab))
        qbd = jnp.concatenate(pieces, axis=0).astype(BF16)
        trow = lax.broadcasted_iota(jnp.int32, (rows, 1), 0) & (tlen - 1)
        qpos = past + trow
        t8 = lax.broadcasted_iota(jnp.int32, (tlen, 1), 0)

        kvn = kvn_ref[...]
        zpad = jnp.zeros((LANES - tlen, LANES), F32)
        newk = lambda off: jnp.concatenate([kvn[:, off * LANES:(off + 1) * LANES], zpad], axis=0).astype(BF16)
        new_lane = lax.broadcasted_iota(jnp.int32, (rows, LANES), 1)
        new_mask = new_lane <= trow

        c_end = lax.broadcasted_iota(jnp.int32, (rows, nseg), 1) * CMP_STRIDE + (CMP_LEN - 1)
        p_cmp = _masked_softmax_rows(_dot_nt(qbd, ck), c_end <= qpos)
        o_cmp = _dot(p_cmp.astype(BF16), cv)

        bias_rows, bias_new = [], []
        for g in range(N_KV):
            psum = sum(p_cmp[g * grows + h * tlen:g * grows + (h + 1) * tlen] for h in range(N_HPG))
            imp = jnp.dot(psum, mmat_ref[...], precision=lax.Precision.HIGHEST, preferred_element_type=F32)
            sel = _select_blocks(imp, past + t8, ns, 1)
            selb = (sel - 1.0) * (-NEG_BIG)
            bias_rows += [selb[:, :LANES]] * N_HPG
            bias_new += [selb[:, nblk:nblk + 1]] * N_HPG
        qx = jnp.concatenate([qbd, jnp.concatenate(bias_rows, axis=0).astype(BF16)], axis=1)
        bias_new = jnp.concatenate(bias_new, axis=0)

        ckeys = min(past, 1024)
        mx = jnp.full((rows, 1), -jnp.inf, F32)
        for c0 in range(0, past, ckeys):
            s = _dot(qx, kx_sc[:, c0:c0 + ckeys])
            s_sc[:, c0:c0 + ckeys] = s
            mx = jnp.maximum(mx, jnp.max(s, axis=1, keepdims=True))
        s_new = jnp.where(new_mask, _dot_nt(qbd, newk(2)) + bias_new, NEG_BIG)
        mx = jnp.maximum(mx, jnp.max(s_new, axis=1, keepdims=True))
        p_new = jnp.exp(s_new - mx)
        den = jnp.sum(p_new, axis=1, keepdims=True)
        acc = _dot(p_new.astype(BF16), newk(3))
        for c0 in range(0, past, ckeys):
            p = jnp.exp(s_sc[:, c0:c0 + ckeys] - mx)
            den = den + jnp.sum(p, axis=1, keepdims=True)
            acc = acc + _dot_nt(p.astype(BF16), v_sc[:, c0:c0 + ckeys])
        o_slc = acc * (1.0 / den)

        wr = lax.broadcasted_iota(jnp.int32, (rows, wlen), 1)
        dpos = trow + wlen - wr
        wk_old = win_ref[0, :, 0:LANES].astype(BF16)
        wv_old = win_ref[0, :, LANES:2 * LANES].astype(BF16)
        pw_old, pw_new = _plain_softmax_parts([(_dot_nt(qbd, wk_old), (dpos >= 0) & (dpos < WINDOW)),
                                               (_dot_nt(qbd, newk(4)), new_mask)])
        o_win = _dot(pw_old.astype(BF16), wv_old) + _dot(pw_new.astype(BF16), newk(5))

        gs = jax.nn.sigmoid(kvn[:, 6 * LANES:7 * LANES])
        for pr in range(N_HEADS // 2):
            halves = []
            for par in range(2):
                h = 2 * pr + par
                g = h // N_HPG
                rs = slice(h * tlen, (h + 1) * tlen)
                c0 = SM_NG + 3 * h
                o = (gs[:, c0:c0 + 1] * o_cmp[rs] + gs[:, c0 + 1:c0 + 2] * o_slc[rs]
                     + gs[:, c0 + 2:c0 + 3] * o_win[rs])
                halves.append(pltpu.roll(o, N_HD, 1) if par != g else o)
            o_ref[0, :, pr * LANES:(pr + 1) * LANES] = jnp.where(lo, halves[0], halves[1])


def nsa_sample(z, row0, page_table, cache_t, layer, win2, wcmp_p, bcmp_p, tlen, pps=8):
    bsz, npages = page_table.shape
    page = cache_t.shape[2]
    wlen = win2.shape[1]
    past = npages * page
    nseg = past // CMP_STRIDE
    nblk = past // SLC_BLOCK
    assert npages % pps == 0 and nblk <= LANES and tlen & (tlen - 1) == 0 and tlen <= SLC_BLOCK and wlen == WINDOW
    mmat = _cmp_to_slc_matrix(nseg - 1, nblk + 1, nseg, 2 * LANES)
    wc2 = wcmp_p.reshape(2, CMP_STRIDE // 2, 2 * KV_W, 2 * KV_W)
    rb0 = row0 // tlen
    kern = functools.partial(_nsa_sample_kernel, npages=npages, pps=pps, tlen=tlen, page=page, wlen=wlen)

    def page_spec(k):
        return pl.BlockSpec((1, 4 * KV_W, page), lambda b, j, pt: (pt[b, j * pps + k], layer, 0))

    grid_spec = pltpu.PrefetchScalarGridSpec(
        num_scalar_prefetch=1,
        grid=(bsz, npages // pps),
        in_specs=[page_spec(k) for k in range(pps)] + [
            pl.BlockSpec((tlen, 1024), lambda b, j, pt: (rb0 + b, PK_NQ // 1024)),
            pl.BlockSpec((tlen, 1024), lambda b, j, pt: (rb0 + b, PK_NKV // 1024)),
            pl.BlockSpec((1, wlen, 2 * KV_W), lambda b, j, pt: (b, 0, 0)),
            pl.BlockSpec((2, CMP_STRIDE // 2, 2 * KV_W, 2 * KV_W), lambda b, j, pt: (0, 0, 0, 0)),
            pl.BlockSpec((2, KV_W), lambda b, j, pt: (0, 0)),
            pl.BlockSpec((nseg, 2 * LANES), lambda b, j, pt: (0, 0)),
        ],
        out_specs=pl.BlockSpec((1, tlen, N_WIDTH), lambda b, j, pt: (b, 0, 0)),
        scratch_shapes=[
            pltpu.VMEM((past, LANES), F32), pltpu.VMEM((past, LANES), F32),
            pltpu.VMEM((2 * LANES, past), BF16), pltpu.VMEM((LANES, past), BF16),
            pltpu.VMEM((N_HEADS * tlen, past), F32),
        ],
    )
    return pl.pallas_call(
        kern,
        grid_spec=grid_spec,
        out_shape=jax.ShapeDtypeStruct((bsz, tlen, N_WIDTH), F32),
        compiler_params=_cparams(("arbitrary", "arbitrary")),
        name="nsa_sample",
    )(page_table, *([cache_t] * pps), z, z, win2, wc2, bcmp_p, mmat)


ROW_TILE = 1024
MERGE_TILE = 512
MOE_TILE = 512
MOE_PARTS = 4
COMBINE_PARTS = 4
PROMPT_CHUNK = 256


def kernel(x_prompt, x_sample, cache_kv, cache_win_kv, state_conv, state_C, state_n, state_m, page_table,
           norm1_g, w_in, b_in, w_conv, b_conv, w_mq, w_mk, g_mnorm, w_cmp, b_cmp,
           w_branch_m, w_branch_n, w_out, norm2_g, w_router, b_router, w_gu, b_gu, w_dn, b_dn, normf_g):
    bp, sp, d = x_prompt.shape
    bs, ts, _ = x_sample.shape
    depth = w_in.shape[0]
    n_p, n_s = bp * sp, bs * ts
    n = n_p + n_s
    assert n_p % ROW_TILE == 0 and n_s % ROW_TILE == 0 and n_p % MERGE_TILE == 0 and n_s % MERGE_TILE == 0
    assert sp % PROMPT_CHUNK == 0 and ts % 8 == 0 and ts >= M_CONV - 1

    assert depth == 1, "only DEPTH == 1 is supported (the final norm is fused into the combine step)"
    xp2, xs2 = x_prompt.reshape(n_p, d), x_sample.reshape(n_s, d)
    st_p, st_s = [], []
    for l in range(depth):
        w_in_p = _pack_cols(w_in[l]).astype(BF16)
        b_in_p = _pack_cols(b_in[l][None])
        z = norm_matmul(xp2, xs2, norm1_g[l][None], w_in_p, b_in_p, ROW_TILE, 1024)

        wq, wk = w_mq[l].astype(BF16), w_mk[l].astype(BF16)
        gn = g_mnorm[l].reshape(1, M_WIDTH)
        bconv = b_conv[l][None]

        def gate_rows(r0, bsz, tlen):
            g = z[r0:r0 + bsz * tlen, PK_SMALL:PK_SMALL + 2 * M_HEADS]
            g = g.reshape(bsz, tlen, 2 * M_HEADS).transpose(0, 2, 1)
            if tlen < LANES:
                g = jnp.pad(g, ((0, 0), (0, 0), (0, LANES - tlen)))
            return g

        zero = lambda *s: jnp.zeros(s, F32)
        hm_p, c_p, nn_p, m_p, tail_p = mlstm(z, gate_rows(0, bp, sp), 0, bp, sp, PROMPT_CHUNK,
                                             zero(bp, 8, M_WIDTH), zero(bp, M_HEADS, M_DQK, M_DV),
                                             zero(bp, M_HEADS, M_DQK), zero(bp, 1, M_HEADS),
                                             w_conv[l], bconv, wq, wk, gn)
        cbuf_s = jnp.pad(state_conv[l], ((0, 0), (8 - (M_CONV - 1), 0), (0, 0)))
        hm_s, c_s, nn_s, m_s, tail_s = mlstm(z, gate_rows(n_p, bs, ts), n_p, bs, ts, ts,
                                             cbuf_s, state_C[l], state_n[l], state_m[l][:, None, :],
                                             w_conv[l], bconv, wq, wk, gn)

        wcmp_p, bcmp_p = _pack_cmp_weights(w_cmp[l], b_cmp[l])
        hn_p, kv_p, win_p = nsa_prompt(z, bp, sp, wcmp_p, bcmp_p)
        wlen_s = cache_win_kv.shape[2]
        win2 = cache_win_kv[l].reshape(bs, wlen_s, 2 * KV_W)
        cache_t = jnp.transpose(cache_kv, (0, 2, 3, 4, 5, 1)).reshape(cache_kv.shape[0], depth * 4 * KV_W,
                                                                      cache_kv.shape[1])
        hn_s = nsa_sample(z, n_p, page_table, cache_t, l, win2, wcmp_p, bcmp_p, ts)

        wr = jnp.pad(w_router[l], ((0, 0), (0, LANES - N_EXPERTS))).astype(BF16)
        br = jnp.pad(b_router[l], (0, LANES - N_EXPERTS))[None]
        h, xn2, top_e, top_w, pos, counts = merge(
            xp2, xs2, hm_p.reshape(n_p, M_WIDTH), hm_s.reshape(n_s, M_WIDTH),
            hn_p, hn_s.reshape(n_s, N_WIDTH).astype(BF16), z,
            w_branch_m[l].astype(BF16), w_branch_n[l].astype(BF16), w_out[l].astype(BF16),
            norm2_g[l][None], wr, br, MERGE_TILE)

        slot_tok, inv_slot, tile_e, n_used = _slot_layout(top_e, pos, counts[0, :N_EXPERTS].astype(jnp.int32),
                                                          MOE_TILE)
        n_tiles = tile_e.shape[0]
        tiles_per = -(-n_tiles // MOE_PARTS)
        y_slots = None
        for t0 in range(0, n_tiles, tiles_per):
            t1 = min(t0 + tiles_per, n_tiles)
            x_part = jnp.take(xn2, slot_tok[t0 * MOE_TILE:t1 * MOE_TILE], axis=0, mode='clip')
            y_slots = expert_ffn(x_part, t0, tile_e, n_used, y_slots, w_gu[l], b_gu[l][:, None, :],
                                 w_dn[l], b_dn[l][:, None, :], MOE_TILE)

        def combine_rows(r0, nrows, out_r0, out_rows, out_prev):
            y4 = jnp.take(y_slots, inv_slot[r0:r0 + nrows].T.reshape(-1), axis=0, mode='clip')
            return combine(h, y4, top_w, normf_g[None], ROW_TILE, r0, out_r0, out_rows, out_prev)

        rows_per = -(-n_p // (COMBINE_PARTS * ROW_TILE)) * ROW_TILE
        y_prompt = None
        for r0 in range(0, n_p, rows_per):
            y_prompt = combine_rows(r0, min(rows_per, n_p - r0), r0, n_p, y_prompt)
        y_prompt = y_prompt.reshape(bp, sp, d)
        y_sample = combine_rows(n_p, n_s, 0, n_s, None).reshape(bs, ts, d)

        kvw_s = z[n_p:, PK_NKV:PK_NKV + 6 * KV_W].reshape(bs, ts, 6, N_KV, N_HD)
        win_s = jnp.concatenate([cache_win_kv[l][:, ts:], kvw_s[:, :, 4:]], axis=1)
        tail0 = 8 - (M_CONV - 1)
        st_p.append((kv_p.reshape(bp, sp, 4, N_KV, N_HD), win_p.reshape(bp, -1, 2, N_KV, N_HD), tail_p[:, tail0:],
                     c_p, nn_p, m_p[:, 0]))
        st_s.append((kvw_s[:, :, :4], win_s, tail_s[:, tail0:], c_s, nn_s, m_s[:, 0]))

    stack = lambda sts, k, axis=0: jnp.stack([s[k] for s in sts], axis=axis)
    return (y_prompt, y_sample,
            stack(st_p, 0, 2), stack(st_p, 1), stack(st_p, 2), stack(st_p, 3), stack(st_p, 4), stack(st_p, 5),
            stack(st_s, 0, 2), stack(st_s, 1), stack(st_s, 2), stack(st_s, 3), stack(st_s, 4), stack(st_s, 5))
```

```python
import functools
import math

import jax
import jax.numpy as jnp
import numpy as np
from jax import lax
from jax.experimental import pallas as pl
from jax.experimental.pallas import tpu as pltpu

F32 = jnp.float32
BF16 = jnp.bfloat16

M_HEADS = 4
M_DV = 256
M_DQK = 128
M_WIDTH = M_HEADS * M_DV
M_CONV = 4
N_HEADS = 16
N_KV = 2
N_HD = 64
N_HPG = N_HEADS // N_KV
N_WIDTH = N_HEADS * N_HD
KV_W = N_KV * N_HD
CMP_LEN = 32
CMP_STRIDE = 16
SLC_BLOCK = 64
N_SELECT = 16
WINDOW = 512
FORCE_BONUS = 1000.0
N_EXPERTS = 32
TOP_K = 4
SWIGLU_LIMIT = 7.0
SWIGLU_ALPHA = 1.702
EPS = 1e-6
TINY = 1e-30
NEG_BIG = -1e30

LANES = 128
PK_MX = 0
PK_MV = 1024
PK_MO = 2048
PK_NQ = 3072
PK_NKV = 4096
PK_SMALL = PK_NKV + 6 * KV_W
PK_BG = 5120
PK_TOTAL = 7168
SM_IG = 0
SM_LF = M_HEADS
SM_NG = 2 * M_HEADS

VMEM_LIMIT = 56 * 1024 * 1024


def _cparams(sem):
    return pltpu.CompilerParams(dimension_semantics=sem, vmem_limit_bytes=VMEM_LIMIT)


def _log_sigmoid(x):
    return jnp.minimum(x, 0.0) - jnp.log1p(jnp.exp(-jnp.abs(x)))


def _dot(a, b):
    return jnp.dot(a, b, preferred_element_type=F32)


def _dot_nt(a, b):
    return lax.dot_general(a, b, (((1,), (1,)), ((), ())), preferred_element_type=F32)


def _dot_tn(a, b):
    return lax.dot_general(a, b, (((0,), (0,)), ((), ())), preferred_element_type=F32)


def _norm_matmul_kernel(xp_ref, xs_ref, g_ref, w_ref, b_ref, o_ref, xn_ref, *, nbp):
    @pl.when(pl.program_id(1) == 0)
    def _():
        x = jnp.where(pl.program_id(0) < nbp, xp_ref[...], xs_ref[...])
        ms = jnp.mean(x * x, axis=-1, keepdims=True)
        xn_ref[...] = (x * lax.rsqrt(ms + EPS) * g_ref[...]).astype(BF16)

    o_ref[...] = _dot(xn_ref[...], w_ref[...]) + b_ref[...]


def norm_matmul(xp, xs, g, w, b, tm, tn):
    n_p, d = xp.shape
    n = n_p + xs.shape[0]
    nbp = n_p // tm
    nc = w.shape[1]
    return pl.pallas_call(
        functools.partial(_norm_matmul_kernel, nbp=nbp),
        grid=(n // tm, nc // tn),
        in_specs=[
            pl.BlockSpec((tm, d), lambda i, j: (jnp.minimum(i, nbp - 1), 0)),
            pl.BlockSpec((tm, d), lambda i, j: (jnp.maximum(i - nbp, 0), 0)),
            pl.BlockSpec((1, d), lambda i, j: (0, 0)),
            pl.BlockSpec((d, tn), lambda i, j: (0, j)),
            pl.BlockSpec((1, tn), lambda i, j: (0, j)),
        ],
        out_specs=pl.BlockSpec((tm, tn), lambda i, j: (i, j)),
        out_shape=jax.ShapeDtypeStruct((n, nc), F32),
        scratch_shapes=[pltpu.VMEM((tm, d), BF16)],
        compiler_params=_cparams(("parallel", "arbitrary")),
        name="norm_matmul",
    )(xp, xs, g, w, b)


def _shift_rows(x, tail, d):
    rows = x.shape[0]
    xd = pltpu.roll(x, d, 0)
    td = pltpu.roll(tail, d, 0)
    head = jnp.where(lax.broadcasted_iota(jnp.int32, td.shape, 0) < d, td, xd[:8])
    if rows == 8:
        return head
    return jnp.concatenate([head, xd[8:]], axis=0)


def _cumsum_rows(x):
    n = x.shape[0]
    idx = lax.broadcasted_iota(jnp.int32, x.shape, 0)
    k = 1
    while k < n:
        x = x + jnp.where(idx >= k, pltpu.roll(x, k, 0), 0.0)
        k *= 2
    return x


def _cumsum_lanes(x, n):
    idx = lax.broadcasted_iota(jnp.int32, x.shape, 1)
    k = 1
    while k < n:
        x = x + jnp.where(idx >= k, pltpu.roll(x, k, 1), 0.0)
        k *= 2
    return x


def _mlstm_kernel(xm_ref, v_ref, o_ref, gc_ref, gr_ref, cbuf_ref, c0_ref, n0_ref, m0_ref,
                  wconv_ref, bconv_ref, wq_ref, wk_ref, gn_ref,
                  h_ref, cout_ref, nout_ref, mout_ref, tout_ref,
                  c_sc, n_sc, m_sc, tail_sc, *, chunk):
    L = chunk
    c = pl.program_id(1)

    @pl.when(c == 0)
    def _init():
        c_sc[...] = c0_ref[0]
        n_sc[...] = n0_ref[0]
        m_sc[...] = m0_ref[0]
        tail_sc[...] = cbuf_ref[0]

    x = xm_ref[...]
    tail = tail_sc[...]
    wc = wconv_ref[...]
    xc = x * wc[M_CONV - 1:M_CONV] + bconv_ref[...]
    for d in range(1, M_CONV):
        xc = xc + _shift_rows(x, tail, d) * wc[M_CONV - 1 - d:M_CONV - d]
    tail_sc[...] = x[L - 8:]
    xc = xc * jax.nn.sigmoid(xc)
    xcb = xc.astype(BF16)

    gc = gc_ref[...]
    gr = gr_ref[0]
    b_col = _cumsum_rows(_log_sigmoid(gc))
    b_row = _cumsum_lanes(_log_sigmoid(gr), L)
    causal = (lax.broadcasted_iota(jnp.int32, (L, L), 1) <= lax.broadcasted_iota(jnp.int32, (L, L), 0))

    vb = v_ref[...].astype(BF16)
    og = jax.nn.sigmoid(o_ref[...])
    gn = gn_ref[...]
    m_all = m_sc[...]
    scale = M_DQK ** -0.5
    m_new_list = []
    for h in range(M_HEADS):
        hs = slice(h * M_DV, (h + 1) * M_DV)
        q = _dot(xcb[:, hs], wq_ref[h])
        k = _dot(xcb[:, hs], wk_ref[h]) * scale
        qb = q.astype(BF16)
        m_prev = m_all[:, h:h + 1]
        bc = b_col[:, SM_LF + h:SM_LF + h + 1]
        igc = gc[:, SM_IG + h:SM_IG + h + 1]
        br = b_row[SM_LF + h:SM_LF + h + 1, :L]
        igr = gr[SM_IG + h:SM_IG + h + 1, :L]
        a = bc + m_prev
        dmat = jnp.where(causal, bc - br + igr, -jnp.inf)
        mt = jnp.maximum(a, jnp.max(dmat, axis=1, keepdims=True))
        w_intra = jnp.exp(dmat - mt)
        w_inter = jnp.exp(a - mt)
        qk = _dot_nt(qb, k.astype(BF16)) * w_intra
        cmat = c_sc[h]
        nrow = n_sc[h:h + 1, :]
        vh = vb[:, hs]
        num = _dot(qk.astype(BF16), vh) + w_inter * _dot(qb, cmat.astype(BF16))
        den = jnp.sum(qk, axis=1, keepdims=True) + w_inter * jnp.sum(q * nrow, axis=1, keepdims=True)
        hh = num * (1.0 / jnp.maximum(jnp.abs(den), jnp.exp(-mt)))
        m_new = mt[L - 1:L, :]
        w_end = jnp.exp(bc[L - 1:L, :] - bc + igc - m_new)
        decay = jnp.exp(a[L - 1:L, :] - m_new)
        kw = k * w_end
        c_sc[h] = decay * cmat + _dot_tn(kw.astype(BF16), vh)
        n_sc[h:h + 1, :] = decay * nrow + jnp.sum(kw, axis=0, keepdims=True)
        m_new_list.append(m_new)
        hn = hh * lax.rsqrt(jnp.mean(hh * hh, axis=-1, keepdims=True) + EPS) * gn[:, hs]
        h_ref[0, :, hs] = (hn * og[:, hs]).astype(h_ref.dtype)
    m_sc[...] = jnp.concatenate(m_new_list, axis=1)

    @pl.when(c == pl.num_programs(1) - 1)
    def _fin():
        cout_ref[0] = c_sc[...]
        nout_ref[0] = n_sc[...]
        mout_ref[0] = m_sc[...]
        tout_ref[0] = tail_sc[...]


def mlstm(z, g_rows, row0, bsz, tlen, chunk, conv_buf8, c0, n0, m0, w_conv, b_conv, wq, wk, gn):
    L = chunk
    nc = tlen // L
    rb0 = row0 // L
    lr = g_rows.shape[2]
    grl = L if nc > 1 else lr

    def zspec(col):
        return pl.BlockSpec((L, 1024), lambda b, c: (rb0 + b * nc + c, col))

    kern = functools.partial(_mlstm_kernel, chunk=L)
    return pl.pallas_call(
        kern,
        grid=(bsz, nc),
        in_specs=[
            zspec(PK_MX // 1024), zspec(PK_MV // 1024), zspec(PK_MO // 1024),
            pl.BlockSpec((L, LANES), lambda b, c: (rb0 + b * nc + c, PK_SMALL // LANES)),
            pl.BlockSpec((1, 8, grl), lambda b, c: (b, 0, c)),
            pl.BlockSpec((1, 8, M_WIDTH), lambda b, c: (b, 0, 0)),
            pl.BlockSpec((1, M_HEADS, M_DQK, M_DV), lambda b, c: (b, 0, 0, 0)),
            pl.BlockSpec((1, M_HEADS, M_DQK), lambda b, c: (b, 0, 0)),
            pl.BlockSpec((1, 1, M_HEADS), lambda b, c: (b, 0, 0)),
            pl.BlockSpec((M_CONV, M_WIDTH), lambda b, c: (0, 0)),
            pl.BlockSpec((1, M_WIDTH), lambda b, c: (0, 0)),
            pl.BlockSpec((M_HEADS, M_DV, M_DQK), lambda b, c: (0, 0, 0)),
            pl.BlockSpec((M_HEADS, M_DV, M_DQK), lambda b, c: (0, 0, 0)),
            pl.BlockSpec((1, M_WIDTH), lambda b, c: (0, 0)),
        ],
        out_specs=[
            pl.BlockSpec((1, L, M_WIDTH), lambda b, c: (b, c, 0)),
            pl.BlockSpec((1, M_HEADS, M_DQK, M_DV), lambda b, c: (b, 0, 0, 0)),
            pl.BlockSpec((1, M_HEADS, M_DQK), lambda b, c: (b, 0, 0)),
            pl.BlockSpec((1, 1, M_HEADS), lambda b, c: (b, 0, 0)),
            pl.BlockSpec((1, 8, M_WIDTH), lambda b, c: (b, 0, 0)),
        ],
        out_shape=[
            jax.ShapeDtypeStruct((bsz, tlen, M_WIDTH), BF16),
            jax.ShapeDtypeStruct((bsz, M_HEADS, M_DQK, M_DV), F32),
            jax.ShapeDtypeStruct((bsz, M_HEADS, M_DQK), F32),
            jax.ShapeDtypeStruct((bsz, 1, M_HEADS), F32),
            jax.ShapeDtypeStruct((bsz, 8, M_WIDTH), F32),
        ],
        scratch_shapes=[
            pltpu.VMEM((M_HEADS, M_DQK, M_DV), F32),
            pltpu.VMEM((M_HEADS, M_DQK), F32),
            pltpu.VMEM((1, M_HEADS), F32),
            pltpu.VMEM((8, M_WIDTH), F32),
        ],
        compiler_params=_cparams(("parallel", "arbitrary")),
        name="mlstm",
    )(z, z, z, z, g_rows, conv_buf8, c0, n0, m0, w_conv, b_conv, wq, wk, gn)


OFF_MX = 0
OFF_MV = OFF_MX + M_WIDTH
OFF_MO = OFF_MV + M_WIDTH
OFF_MI = OFF_MO + M_WIDTH
OFF_MF = OFF_MI + M_HEADS
OFF_NQ = OFF_MF + M_HEADS
OFF_NKV = OFF_NQ + N_WIDTH
OFF_NG = OFF_NKV + 6 * KV_W
OFF_BG = OFF_NG + 3 * N_HEADS


def _pack_cols(a):
    lead = a.shape[:-1]
    d_model = a.shape[-1] - OFF_BG
    small = jnp.concatenate([a[..., OFF_MI:OFF_NQ], a[..., OFF_NG:OFF_BG]], axis=-1)
    small = jnp.concatenate([small, jnp.zeros(lead + (LANES - small.shape[-1],), a.dtype)], axis=-1)
    pad = jnp.zeros(lead + (PK_BG - PK_SMALL - LANES,), a.dtype)
    assert d_model == PK_TOTAL - PK_BG
    return jnp.concatenate([a[..., OFF_MX:OFF_MI], a[..., OFF_NQ:OFF_NKV], a[..., OFF_NKV:OFF_NG],
                            small, pad, a[..., OFF_BG:]], axis=-1)


Q_SCALE = (N_HD ** -0.5) * math.log2(math.e)

def _dup_halves(x, lane):
    r = pltpu.roll(x, N_HD, 1)
    lo = lane < N_HD
    return jnp.where(lo, x, r), jnp.where(lo, r, x)


def _masked_softmax_rows(s, mask):
    s = jnp.where(mask, s, -jnp.inf)
    mx = jnp.max(s, axis=1, keepdims=True)
    mx = jnp.where(mx > -jnp.inf, mx, 0.0)
    p = jnp.exp2(s - mx)
    return p * (1.0 / jnp.maximum(jnp.sum(p, axis=1, keepdims=True), TINY))


def _softmax_pv(s, v):
    m = jnp.max(s, axis=1, keepdims=True)
    p = jnp.exp2(s - m)
    return _dot(p.astype(BF16), v) * (1.0 / jnp.sum(p, axis=1, keepdims=True))


def _slc_attend(qxs, kx_ref, v_ref, g, s_sc, nfull, diag_bias, ck):
    rows = qxs[0].shape[0]
    nl = ck // LANES
    streams = range(len(qxs))
    lane_chunks = lambda s: [s[:, a * LANES:(a + 1) * LANES] for a in range(nl)]

    def scores(c):
        kt = kx_ref[g, pl.ds(pl.multiple_of(c * ck, ck), ck), :]
        return [_dot_nt(qx, kt) for qx in qxs]

    def max_pass(c, mvecs):
        out = []
        for n, s in enumerate(scores(c)):
            s_sc[n, c] = s
            out.append(functools.reduce(jnp.maximum, lane_chunks(s), mvecs[n]))
        return tuple(out)

    mvecs = lax.fori_loop(0, nfull, max_pass, tuple(jnp.full((rows, LANES), -jnp.inf, F32) for _ in streams))
    sds = [s + diag_bias for s in scores(nfull)]
    mbs = [jnp.broadcast_to(jnp.max(functools.reduce(jnp.maximum, lane_chunks(sd), mv), axis=1, keepdims=True),
                            (rows, LANES)) for sd, mv in zip(sds, mvecs)]

    def probs(ss, c):
        vt = v_ref[g, pl.ds(pl.multiple_of(c * ck, ck), ck), :]
        out = []
        for s, mb in zip(ss, mbs):
            ps = [jnp.exp2(x - mb) for x in lane_chunks(s)]
            out.append((functools.reduce(lambda x, y: x + y, ps), _dot(jnp.concatenate(ps, axis=1).astype(BF16), vt)))
        return tuple(out)

    def exp_pass(c, carry):
        new = probs([s_sc[n, c] for n in streams], c)
        return tuple((carry[n][0] + new[n][0], carry[n][1] + new[n][1]) for n in streams)

    res = lax.fori_loop(0, nfull, exp_pass, probs(sds, nfull))
    return [acc * (1.0 / jnp.sum(lvec, axis=1, keepdims=True)) for lvec, acc in res]


def _select_blocks(imp, qpos, ns, axis):
    blk = lax.broadcasted_iota(jnp.int32, imp.shape, axis)
    cur = qpos // SLC_BLOCK
    allowed = blk <= cur
    forced = (blk == 0) | (blk == cur) | (blk == cur - 1)
    score = jnp.where(allowed, imp + jnp.where(forced, FORCE_BONUS, 0.0), -jnp.inf)
    cnt = jnp.zeros(imp.shape, F32)
    for i in range(ns):
        si = score[:, i:i + 1] if axis == 1 else score[i:i + 1, :]
        ahead = (si > score) | ((si == score) & (blk > i))
        cnt = cnt + jnp.where(ahead, 1.0, 0.0)
    return jnp.where((cnt < float(N_SELECT)) & allowed, 1.0, 0.0)


def _stack_q(q_ref, g, rows):
    qs = jnp.concatenate([q_ref[:, (4 * g + p) * LANES:(4 * g + p + 1) * LANES] for p in range(N_HPG // 2)], axis=0)
    qs = qs * Q_SCALE
    lane = lax.broadcasted_iota(jnp.int32, qs.shape, 1)
    qe = jnp.where(lane < N_HD, qs, 0.0).astype(BF16)
    qo = jnp.where(lane < N_HD, 0.0, qs).astype(BF16)
    return qe, qo


def _gate_pair(gs, g, p, j, lane):
    he = g * N_HPG + 2 * p
    ce = SM_NG + 3 * he + j
    co = ce + 3
    return jnp.where(lane < N_HD, gs[:, ce:ce + 1], gs[:, co:co + 1])


def _nsa_prompt_kernel(q_ref, kv_ref, wcmp_ref, bcmp_ref, mmat_ref, o_ref, kvo_ref, wino_ref,
                       ck_sc, cv_sc, kx_sc, sv_sc, wk_sc, wv_sc, stage_sc, s_sc, *, seq, tq, ck, wn):
    i = pl.program_id(1)
    nseg = seq // CMP_STRIDE
    ns = seq // SLC_BLOCK
    npair = N_HPG // 2

    @pl.when(i == 0)
    def _prep():
        kvo_ref[0] = kv_ref[:, 0:4 * KV_W]
        wino_ref[0] = kv_ref[seq - wino_ref.shape[1]:, 4 * KV_W:6 * KV_W]
        kr = lax.broadcasted_iota(jnp.int32, (seq, LANES), 0)
        kc = lax.broadcasted_iota(jnp.int32, (seq, LANES), 1)
        onehot = jnp.where(kr // SLC_BLOCK == kc, 1.0, 0.0).astype(BF16)
        for g in range(N_KV):
            kx_sc[g, :, LANES:] = onehot
        lane = lax.broadcasted_iota(jnp.int32, (nseg, LANES), 1)
        for c, dst in ((0, ck_sc), (1, cv_sc)):
            acc = jnp.zeros((nseg, 2 * LANES), F32)
            stage_sc[...] = kv_ref[:, c * LANES:(c + 1) * LANES]
            for s in range(CMP_STRIDE):
                xs = stage_sc[pl.ds(s, nseg, stride=CMP_STRIDE), :]
                acc = acc + _dot(xs.astype(BF16), wcmp_ref[c, s])
            blocks = acc[:, :LANES] + pltpu.roll(acc[:, LANES:], nseg - 1, 0) + bcmp_ref[c:c + 1, :]
            d0, d1 = _dup_halves(blocks, lane)
            dst[0] = d0.astype(BF16)
            dst[1] = d1.astype(BF16)
        lane_s = lax.broadcasted_iota(jnp.int32, (seq, LANES), 1)
        for off, dst in ((2, kx_sc), (3, sv_sc), (4, wk_sc), (5, wv_sc)):
            d0, d1 = _dup_halves(kv_ref[:, off * LANES:(off + 1) * LANES], lane_s)
            dst[0, :, 0:LANES] = d0.astype(BF16)
            dst[1, :, 0:LANES] = d1.astype(BF16)

    t0 = i * tq
    qpos = t0 + lax.broadcasted_iota(jnp.int32, (tq, 1), 0)
    qpos_row = t0 + lax.broadcasted_iota(jnp.int32, (1, tq), 1)
    nsp = -(-ns // 8) * 8
    lane = lax.broadcasted_iota(jnp.int32, (tq, LANES), 1)
    gs = jax.nn.sigmoid(kv_ref[pl.ds(pl.multiple_of(t0, tq), tq), 6 * LANES:7 * LANES])
    c_end = lax.broadcasted_iota(jnp.int32, (tq, nseg), 1) * CMP_STRIDE + (CMP_LEN - 1)
    cmask = jnp.concatenate([c_end <= qpos] * npair, axis=0)
    lane4 = lax.broadcasted_iota(jnp.int32, (npair * tq, LANES), 1)
    lo4 = lane4 < N_HD

    nfull = t0 // ck
    kpos_d = nfull * ck + lax.broadcasted_iota(jnp.int32, (tq, ck), 1)
    diag_bias = jnp.concatenate([jnp.where(kpos_d <= qpos, 0.0, NEG_BIG)] * npair, axis=0)
    ks = pl.multiple_of(jnp.clip(t0 + tq - wn, 0, seq - wn), LANES)
    dpos = qpos - (ks + lax.broadcasted_iota(jnp.int32, (tq, wn), 1))
    win_bias = jnp.concatenate([jnp.where((dpos >= 0) & (dpos < WINDOW), 0.0, NEG_BIG)] * npair, axis=0)

    for g in range(N_KV):
        qe, qo = _stack_q(q_ref, g, tq)
        ckg = ck_sc[g]
        cvg = cv_sc[g]
        p_e = _masked_softmax_rows(_dot_nt(qe, ckg), cmask)
        p_o = _masked_softmax_rows(_dot_nt(qo, ckg), cmask)
        o_cmp = jnp.where(lo4, _dot(p_e.astype(BF16), cvg), _dot(p_o.astype(BF16), cvg))
        psum = p_e + p_o
        psum = sum(psum[p * tq:(p + 1) * tq] for p in range(npair))
        imp_t = lax.dot_general(mmat_ref[...], psum, (((1,), (1,)), ((), ())),
                                precision=lax.Precision.HIGHEST, preferred_element_type=F32)
        sel_t = _select_blocks(imp_t[:nsp], qpos_row, ns, 0)
        sel = jnp.concatenate([sel_t, jnp.zeros((LANES - nsp, tq), F32)], axis=0).T
        selb = ((sel - 1.0) * (-NEG_BIG)).astype(BF16)
        selb4 = jnp.concatenate([selb] * npair, axis=0)

        o_e, o_o = _slc_attend([jnp.concatenate([qe, selb4], axis=1), jnp.concatenate([qo, selb4], axis=1)],
                               kx_sc, sv_sc, g, s_sc, nfull, diag_bias, ck)
        o_slc = jnp.where(lo4, o_e, o_o)

        kw = wk_sc[g, pl.ds(ks, wn), :]
        vw = wv_sc[g, pl.ds(ks, wn), :]
        o_win = jnp.where(lo4, _softmax_pv(_dot_nt(qe, kw) + win_bias, vw), _softmax_pv(_dot_nt(qo, kw) + win_bias, vw))

        for p in range(npair):
            rs = slice(p * tq, (p + 1) * tq)
            o = (_gate_pair(gs, g, p, 0, lane) * o_cmp[rs] + _gate_pair(gs, g, p, 1, lane) * o_slc[rs]
                 + _gate_pair(gs, g, p, 2, lane) * o_win[rs])
            c0 = (g * npair + p) * LANES
            o_ref[:, c0:c0 + LANES] = o.astype(o_ref.dtype)


def _cmp_to_slc_matrix(nc, ns, rows, cols):
    c0 = np.arange(nc)[:, None] * CMP_STRIDE
    s0 = np.arange(ns)[None, :] * SLC_BLOCK
    ov = np.clip(np.minimum(c0 + CMP_LEN, s0 + SLC_BLOCK) - np.maximum(c0, s0), 0, None) / CMP_LEN
    out = np.zeros((rows, cols), np.float32)
    out[:nc, :ns] = ov
    return jnp.asarray(out)


def _pack_cmp_weights(w_cmp, b_cmp):
    r = CMP_LEN // CMP_STRIDE
    w = w_cmp.reshape(2, r, CMP_STRIDE, N_HD, N_HD)
    eye = jnp.eye(N_KV, dtype=w.dtype)
    bd = jnp.einsum('gh,crsde->crsgdhe', eye, w).reshape(2, r, CMP_STRIDE, KV_W, KV_W)
    wp = jnp.concatenate([bd[:, k] for k in range(r)], axis=-1)
    bp = jnp.concatenate([b_cmp] * N_KV, axis=-1)
    return wp.astype(BF16), bp


def nsa_prompt(z, bsz, seq, wcmp_p, bcmp_p, tq=256, ck=256):
    nseg = seq // CMP_STRIDE
    ns = seq // SLC_BLOCK
    nq = seq // tq
    ck = min(ck, seq)
    wn = min(WINDOW + tq, seq)
    assert seq % ck == 0 and ck % tq == 0 and ns <= LANES and seq % LANES == 0
    mmat = _cmp_to_slc_matrix(nseg - 1, ns, nseg, LANES).T
    wlen = min(WINDOW, seq)
    once = pl.Buffered(1)
    kern = functools.partial(_nsa_prompt_kernel, seq=seq, tq=tq, ck=ck, wn=wn)
    return pl.pallas_call(
        kern,
        grid=(bsz, nq),
        in_specs=[
            pl.BlockSpec((tq, 1024), lambda b, i: (b * nq + i, PK_NQ // 1024)),
            pl.BlockSpec((seq, 1024), lambda b, i: (b, PK_NKV // 1024), pipeline_mode=once),
            pl.BlockSpec((2, CMP_STRIDE, KV_W, 2 * KV_W), lambda b, i: (0, 0, 0, 0)),
            pl.BlockSpec((2, KV_W), lambda b, i: (0, 0)),
            pl.BlockSpec((LANES, nseg), lambda b, i: (0, 0)),
        ],
        out_specs=[pl.BlockSpec((tq, N_WIDTH), lambda b, i: (b * nq + i, 0)),
                   pl.BlockSpec((1, seq, 4 * KV_W), lambda b, i: (b, 0, 0), pipeline_mode=once),
                   pl.BlockSpec((1, wlen, 2 * KV_W), lambda b, i: (b, 0, 0), pipeline_mode=once)],
        out_shape=[jax.ShapeDtypeStruct((bsz * seq, N_WIDTH), BF16),
                   jax.ShapeDtypeStruct((bsz, seq, 4 * KV_W), F32),
                   jax.ShapeDtypeStruct((bsz, wlen, 2 * KV_W), F32)],
        scratch_shapes=[
            pltpu.VMEM((N_KV, nseg, LANES), BF16), pltpu.VMEM((N_KV, nseg, LANES), BF16),
            pltpu.VMEM((N_KV, seq, 2 * LANES), BF16), pltpu.VMEM((N_KV, seq, LANES), BF16),
            pltpu.VMEM((N_KV, seq, LANES), BF16), pltpu.VMEM((N_KV, seq, LANES), BF16),
            pltpu.VMEM((seq, LANES), F32),
            pltpu.VMEM((2, seq // ck, (N_HPG // 2) * tq, ck), F32),
        ],
        compiler_params=_cparams(("parallel", "arbitrary")),
        name="nsa_prompt",
    )(z, z, wcmp_p, bcmp_p, mmat)


def _merge_kernel(xp_ref, xs_ref, hmp_ref, hms_ref, hnp_ref, hns_ref, bgm_ref, bgn_ref,
                  wbm_ref, wbn_ref, wout_ref, n2_ref, wr_ref, br_ref,
                  h_ref, xn2_ref, te_ref, tw_ref, pos_ref, cnt_ref, run_sc, *, nbp):
    i = pl.program_id(0)
    is_p = i < nbp
    x = jnp.where(is_p, xp_ref[...], xs_ref[...])
    hm = jnp.where(is_p, hmp_ref[...], hms_ref[...])
    hn = jnp.where(is_p, hnp_ref[...], hns_ref[...])
    t = (jax.nn.sigmoid(bgm_ref[...]) * _dot(hm, wbm_ref[...])
         + jax.nn.sigmoid(bgn_ref[...]) * _dot(hn, wbn_ref[...]))
    h = x + _dot(t.astype(BF16), wout_ref[...])
    h_ref[...] = h
    xn2 = (h * lax.rsqrt(jnp.mean(h * h, axis=-1, keepdims=True) + EPS) * n2_ref[...]).astype(BF16)
    xn2_ref[...] = xn2

    @pl.when(i == 0)
    def _():
        run_sc[...] = jnp.zeros_like(run_sc)

    logits = _dot(xn2, wr_ref[...]) + br_ref[...]
    lane = lax.broadcasted_iota(jnp.int32, logits.shape, 1)
    cur = jnp.where(lane < N_EXPERTS, logits, -jnp.inf)
    vals, idxs, hots = [], [], []
    for _ in range(TOP_K):
        m = jnp.max(cur, axis=1, keepdims=True)
        idx = jnp.min(jnp.where(cur == m, lane, LANES), axis=1, keepdims=True)
        hot = lane == idx
        vals.append(m)
        idxs.append(idx)
        hots.append(hot)
        cur = jnp.where(hot, -jnp.inf, cur)
    es = [jnp.exp(v - vals[0]) for v in vals]
    inv = 1.0 / functools.reduce(lambda a, b: a + b, es)
    tw_ref[...] = jnp.concatenate([e * inv for e in es], axis=1)
    te_ref[...] = jnp.concatenate(idxs, axis=1)
    cnt = functools.reduce(lambda a, b: a + b, [jnp.where(hot, 1.0, 0.0) for hot in hots])
    incl = _cumsum_rows(cnt)
    before = incl - cnt + run_sc[...]
    pos = [jnp.sum(jnp.where(hot, before, 0.0), axis=1, keepdims=True) for hot in hots]
    pos_ref[...] = jnp.concatenate(pos, axis=1).astype(jnp.int32)
    run_sc[...] = run_sc[...] + incl[incl.shape[0] - 1:, :]
    cnt_ref[...] = run_sc[...]


def merge(xp, xs, hmp, hms, hnp, hns, z, wbm, wbn, wout, n2, wr, br, tm):
    n_p, d = xp.shape
    n = n_p + xs.shape[0]
    nbp = n_p // tm
    row = lambda i: (i, 0)
    prow = lambda i: (jnp.minimum(i, nbp - 1), 0)
    srow = lambda i: (jnp.maximum(i - nbp, 0), 0)
    fixed = lambda i: (0, 0)
    pair = [pl.BlockSpec((tm, d), prow), pl.BlockSpec((tm, d), srow)]
    return pl.pallas_call(
        functools.partial(_merge_kernel, nbp=nbp),
        grid=(n // tm,),
        in_specs=pair * 3 + [
            pl.BlockSpec((tm, d), lambda i: (i, PK_BG // 1024)),
            pl.BlockSpec((tm, d), lambda i: (i, PK_BG // 1024 + 1)),
            pl.BlockSpec((d, d), fixed), pl.BlockSpec((d, d), fixed), pl.BlockSpec((d, d), fixed),
            pl.BlockSpec((1, d), fixed), pl.BlockSpec((d, LANES), fixed), pl.BlockSpec((1, LANES), fixed),
        ],
        out_specs=[pl.BlockSpec((tm, d), row), pl.BlockSpec((tm, d), row),
                   pl.BlockSpec((tm, TOP_K), row), pl.BlockSpec((tm, TOP_K), row), pl.BlockSpec((tm, TOP_K), row),
                   pl.BlockSpec((1, LANES), fixed)],
        out_shape=[jax.ShapeDtypeStruct((n, d), F32), jax.ShapeDtypeStruct((n, d), BF16),
                   jax.ShapeDtypeStruct((n, TOP_K), jnp.int32), jax.ShapeDtypeStruct((n, TOP_K), F32),
                   jax.ShapeDtypeStruct((n, TOP_K), jnp.int32), jax.ShapeDtypeStruct((1, LANES), F32)],
        scratch_shapes=[pltpu.VMEM((1, LANES), F32)],
        compiler_params=_cparams(("arbitrary",)),
        name="merge",
    )(xp, xs, hmp, hms, hnp, hns, z, z, wbm, wbn, wout, n2, wr, br)


def _expert_kernel(te_ref, nu_ref, x_ref, wgu_ref, bgu_ref, wdn_ref, bdn_ref, *rest, tile0):
    y_ref, wgu_sc, wdn_sc = rest[-3:]
    t = pl.program_id(0)
    tg = tile0 + t

    @pl.when(tg < nu_ref[0])
    def _():
        @pl.when((t == 0) | (te_ref[tg] != te_ref[jnp.maximum(tg - 1, 0)]))
        def _cast():
            wgu_sc[...] = wgu_ref[0].astype(BF16)
            wdn_sc[...] = wdn_ref[0].astype(BF16)

        de = wdn_sc.shape[0]
        gu = _dot(x_ref[...], wgu_sc[...]) + bgu_ref[0]
        gate = jnp.minimum(gu[:, :de], SWIGLU_LIMIT)
        up = jnp.clip(gu[:, de:], -SWIGLU_LIMIT, SWIGLU_LIMIT)
        glu = gate * jax.nn.sigmoid(gate * SWIGLU_ALPHA)
        act = ((up + 1.0) * glu).astype(BF16)
        y_ref[...] = _dot(act, wdn_sc[...]) + bdn_ref[0]


def expert_ffn(x_part, tile0, tile_e, n_used, y_prev, wgu, bgu, wdn, bdn, tm):
    d = x_part.shape[1]
    de = wdn.shape[1]
    n_part = x_part.shape[0] // tm
    n_all = tile_e.shape[0]

    def local(t, nu):
        return jnp.clip(jnp.minimum(tile0 + t, nu[0] - 1) - tile0, 0, n_part - 1)

    per_e = lambda t, te, nu: (te[tile0 + t], 0, 0)
    in_specs = [
        pl.BlockSpec((tm, d), lambda t, te, nu: (local(t, nu), 0)),
        pl.BlockSpec((1, d, 2 * de), per_e), pl.BlockSpec((1, 1, 2 * de), per_e),
        pl.BlockSpec((1, de, d), per_e), pl.BlockSpec((1, 1, d), per_e),
    ]
    operands = [tile_e, n_used, x_part, wgu, bgu, wdn, bdn]
    aliases = {}
    if y_prev is not None:
        in_specs.append(pl.BlockSpec(memory_space=pl.ANY))
        aliases = {len(operands): 0}
        operands.append(y_prev)
    grid_spec = pltpu.PrefetchScalarGridSpec(
        num_scalar_prefetch=2,
        grid=(n_part,),
        in_specs=in_specs,
        out_specs=pl.BlockSpec((tm, d), lambda t, te, nu: (tile0 + local(t, nu), 0)),
        scratch_shapes=[pltpu.VMEM((d, 2 * de), BF16), pltpu.VMEM((de, d), BF16)],
    )
    return pl.pallas_call(
        functools.partial(_expert_kernel, tile0=tile0),
        grid_spec=grid_spec,
        out_shape=jax.ShapeDtypeStruct((n_all * tm, d), F32),
        input_output_aliases=aliases,
        compiler_params=_cparams(("arbitrary",)),
        name="expert_ffn",
    )(*operands)


def _combine_kernel(h_ref, *refs):
    y_refs = refs[:TOP_K]
    w_ref, g_ref = refs[TOP_K:TOP_K + 2]
    o_ref = refs[-1]
    w = w_ref[...]
    moe = w[:, 0:1] * y_refs[0][...]
    for k in range(1, TOP_K):
        moe = moe + w[:, k:k + 1] * y_refs[k][...]
    acc = h_ref[...] + moe
    o_ref[...] = acc * lax.rsqrt(jnp.mean(acc * acc, axis=-1, keepdims=True) + EPS) * g_ref[...]


def combine(h, y4, w4, g, tm, row0, out_row0, out_rows, out_prev):
    d = h.shape[1]
    nrows = y4.shape[0] // TOP_K
    nb = nrows // tm
    rb0 = row0 // tm
    ob0 = out_row0 // tm
    in_specs = ([pl.BlockSpec((tm, d), lambda i: (rb0 + i, 0))]
                + [pl.BlockSpec((tm, d), functools.partial(lambda i, k: (k * nb + i, 0), k=k)) for k in range(TOP_K)]
                + [pl.BlockSpec((tm, TOP_K), lambda i: (rb0 + i, 0)), pl.BlockSpec((1, d), lambda i: (0, 0))])
    operands = [h] + [y4] * TOP_K + [w4, g]
    aliases = {}
    if out_prev is not None:
        in_specs.append(pl.BlockSpec(memory_space=pl.ANY))
        aliases = {len(operands): 0}
        operands.append(out_prev)
    return pl.pallas_call(
        _combine_kernel,
        grid=(nb,),
        in_specs=in_specs,
        out_specs=pl.BlockSpec((tm, d), lambda i: (ob0 + i, 0)),
        out_shape=jax.ShapeDtypeStruct((out_rows, d), F32),
        input_output_aliases=aliases,
        compiler_params=_cparams(("parallel",)),
        name="combine",
    )(*operands)


def _slot_layout(top_e, pos, counts, tm):
    n = top_e.shape[0]
    nk = n * TOP_K
    padded = (counts + tm - 1) // tm * tm
    pad_end = jnp.cumsum(padded)
    pad_start = pad_end - padded
    grp_start = jnp.cumsum(counts) - counts
    n_tiles = -(-(nk + N_EXPERTS * (tm - 1)) // tm)
    tile_e = jnp.minimum(jnp.sum(pad_end[None, :] <= (jnp.arange(n_tiles) * tm)[:, None], axis=1),
                         N_EXPERTS - 1).astype(jnp.int32)
    hot = top_e[..., None] == jnp.arange(N_EXPERTS)
    inv_slot = (jnp.sum(jnp.where(hot, pad_start, 0), axis=-1) + pos).astype(jnp.int32)
    order = jnp.argsort(top_e.reshape(-1))
    slot = jnp.arange(n_tiles * tm, dtype=jnp.int32)
    slot_e = jnp.repeat(tile_e, tm)
    src = jnp.clip(grp_start[slot_e] + slot - pad_start[slot_e], 0, nk - 1)
    slot_tok = (jnp.take(order, src, mode='clip') // TOP_K).astype(jnp.int32)
    n_used = (pad_end[-1:] // tm).astype(jnp.int32)
    return slot_tok, inv_slot, tile_e, n_used


def _plain_softmax_parts(parts):
    ms = [jnp.where(m, s, NEG_BIG) if m is not None else s for s, m in parts]
    mx = functools.reduce(jnp.maximum, [jnp.max(s, axis=1, keepdims=True) for s in ms])
    ps = [jnp.exp2(s - mx) for s in ms]
    den = functools.reduce(lambda a, b: a + b, [jnp.sum(p, axis=1, keepdims=True) for p in ps])
    inv = 1.0 / den
    return [p * inv for p in ps]


def _nsa_sample_kernel(pt_ref, *refs, npages, pps, tlen, page, wlen):
    page_refs = refs[:pps]
    (q_ref, kvn_ref, win_ref, wc_ref, bc_ref, mmat_ref, o_ref,
     stk_sc, stv_sc, kx_sc, v_sc, s_sc) = refs[pps:]
    b = pl.program_id(0)
    j = pl.program_id(1)
    nsteps = npages // pps
    past = npages * page
    nseg = past // CMP_STRIDE
    nblk = past // SLC_BLOCK
    ns = nblk + 1
    rows = N_HEADS * tlen
    grows = N_HPG * tlen

    @pl.when((b == 0) & (j == 0))
    def _onehot():
        r = lax.broadcasted_iota(jnp.int32, (LANES, past), 0)
        c = lax.broadcasted_iota(jnp.int32, (LANES, past), 1)
        kx_sc[LANES:, :] = jnp.where(c // SLC_BLOCK == r, 1.0, 0.0).astype(BF16)

    for k in range(pps):
        r0 = pl.multiple_of((j * pps + k) * page, page)
        pg = page_refs[k]
        stk_sc[pl.ds(r0, page), :] = pg[0, 0:LANES, :].T
        stv_sc[pl.ds(r0, page), :] = pg[0, LANES:2 * LANES, :].T
        kx_sc[0:LANES, pl.ds(r0, page)] = pg[0, 2 * LANES:3 * LANES, :].astype(BF16)
        v_sc[:, pl.ds(r0, page)] = pg[0, 3 * LANES:4 * LANES, :].astype(BF16)

    @pl.when(j == nsteps - 1)
    def _attend():
        lane = lax.broadcasted_iota(jnp.int32, (tlen, LANES), 1)
        lo = lane < N_HD

        def cmp_proj(st_sc, c):
            acc = jnp.zeros((nseg, 2 * LANES), F32)
            for sp in range(CMP_STRIDE // 2):
                xa = st_sc[pl.ds(2 * sp, nseg, stride=CMP_STRIDE), :]
                xb = st_sc[pl.ds(2 * sp + 1, nseg, stride=CMP_STRIDE), :]
                acc = acc + _dot(jnp.concatenate([xa, xb], axis=1).astype(BF16), wc_ref[c, sp])
            blocks = acc[:, :LANES] + pltpu.roll(acc[:, LANES:], nseg - 1, 0) + bc_ref[c:c + 1, :]
            return blocks.astype(BF16)

        ck = cmp_proj(stk_sc, 0)
        cv = cmp_proj(stv_sc, 1)

        qall = q_ref[...] * Q_SCALE
        pieces = []
        for h in range(N_HEADS):
            slab = qall[:, (h // 2) * LANES:(h // 2 + 1) * LANES]
            g = h // N_HPG
            if (h % 2) != g:
                slab = pltpu.roll(slab, N_HD, 1)
            pieces.append(jnp.where(lo, slab, 0.0) if g == 0 else jnp.where(lo, 0.0, slab))
        qbd = jnp.concatenate(pieces, axis=0).astype(BF16)
        trow = lax.broadcasted_iota(jnp.int32, (rows, 1), 0) & (tlen - 1)
        qpos = past + trow
        t8 = lax.broadcasted_iota(jnp.int32, (tlen, 1), 0)

        kvn = kvn_ref[...]
        zpad = jnp.zeros((LANES - tlen, LANES), F32)
        newk = lambda off: jnp.concatenate([kvn[:, off * LANES:(off + 1) * LANES], zpad], axis=0).astype(BF16)
        new_lane = lax.broadcasted_iota(jnp.int32, (rows, LANES), 1)
        new_mask = new_lane <= trow

        c_end = lax.broadcasted_iota(jnp.int32, (rows, nseg), 1) * CMP_STRIDE + (CMP_LEN - 1)
        p_cmp = _masked_softmax_rows(_dot_nt(qbd, ck), c_end <= qpos)
        o_cmp = _dot(p_cmp.astype(BF16), cv)

        bias_rows, bias_new = [], []
        for g in range(N_KV):
            psum = sum(p_cmp[g * grows + h * tlen:g * grows + (h + 1) * tlen] for h in range(N_HPG))
            imp = jnp.dot(psum, mmat_ref[...], precision=lax.Precision.HIGHEST, preferred_element_type=F32)
            sel = _select_blocks(imp, past + t8, ns, 1)
            selb = (sel - 1.0) * (-NEG_BIG)
            bias_rows += [selb[:, :LANES]] * N_HPG
            bias_new += [selb[:, nblk:nblk + 1]] * N_HPG
        qx = jnp.concatenate([qbd, jnp.concatenate(bias_rows, axis=0).astype(BF16)], axis=1)
        bias_new = jnp.concatenate(bias_new, axis=0)

        ckeys = min(past, 1024)
        mx = jnp.full((rows, 1), -jnp.inf, F32)
        for c0 in range(0, past, ckeys):
            s = _dot(qx, kx_sc[:, c0:c0 + ckeys])
            s_sc[:, c0:c0 + ckeys] = s
            mx = jnp.maximum(mx, jnp.max(s, axis=1, keepdims=True))
        s_new = jnp.where(new_mask, _dot_nt(qbd, newk(2)) + bias_new, NEG_BIG)
        mx = jnp.maximum(mx, jnp.max(s_new, axis=1, keepdims=True))
        p_new = jnp.exp2(s_new - mx)
        den = jnp.sum(p_new, axis=1, keepdims=True)
        acc = _dot(p_new.astype(BF16), newk(3))
        for c0 in range(0, past, ckeys):
            p = jnp.exp2(s_sc[:, c0:c0 + ckeys] - mx)
            den = den + jnp.sum(p, axis=1, keepdims=True)
            acc = acc + _dot_nt(p.astype(BF16), v_sc[:, c0:c0 + ckeys])
        o_slc = acc * (1.0 / den)

        wr = lax.broadcasted_iota(jnp.int32, (rows, wlen), 1)
        dpos = trow + wlen - wr
        wk_old = win_ref[0, :, 0:LANES].astype(BF16)
        wv_old = win_ref[0, :, LANES:2 * LANES].astype(BF16)
        pw_old, pw_new = _plain_softmax_parts([(_dot_nt(qbd, wk_old), (dpos >= 0) & (dpos < WINDOW)),
                                               (_dot_nt(qbd, newk(4)), new_mask)])
        o_win = _dot(pw_old.astype(BF16), wv_old) + _dot(pw_new.astype(BF16), newk(5))

        gs = jax.nn.sigmoid(kvn[:, 6 * LANES:7 * LANES])
        for pr in range(N_HEADS // 2):
            halves = []
            for par in range(2):
                h = 2 * pr + par
                g = h // N_HPG
                rs = slice(h * tlen, (h + 1) * tlen)
                c0 = SM_NG + 3 * h
                o = (gs[:, c0:c0 + 1] * o_cmp[rs] + gs[:, c0 + 1:c0 + 2] * o_slc[rs]
                     + gs[:, c0 + 2:c0 + 3] * o_win[rs])
                halves.append(pltpu.roll(o, N_HD, 1) if par != g else o)
            o_ref[0, :, pr * LANES:(pr + 1) * LANES] = jnp.where(lo, halves[0], halves[1])


def nsa_sample(z, row0, page_table, cache_t, layer, win2, wcmp_p, bcmp_p, tlen, pps=16):
    bsz, npages = page_table.shape
    pps = min(pps, npages)
    page = cache_t.shape[2]
    wlen = win2.shape[1]
    past = npages * page
    nseg = past // CMP_STRIDE
    nblk = past // SLC_BLOCK
    assert npages % pps == 0 and nblk <= LANES and tlen & (tlen - 1) == 0 and tlen <= SLC_BLOCK and wlen == WINDOW
    mmat = _cmp_to_slc_matrix(nseg - 1, nblk + 1, nseg, 2 * LANES)
    wc2 = wcmp_p.reshape(2, CMP_STRIDE // 2, 2 * KV_W, 2 * KV_W)
    rb0 = row0 // tlen
    kern = functools.partial(_nsa_sample_kernel, npages=npages, pps=pps, tlen=tlen, page=page, wlen=wlen)

    def page_spec(k):
        return pl.BlockSpec((1, 4 * KV_W, page), lambda b, j, pt: (pt[b, j * pps + k], layer, 0))

    grid_spec = pltpu.PrefetchScalarGridSpec(
        num_scalar_prefetch=1,
        grid=(bsz, npages // pps),
        in_specs=[page_spec(k) for k in range(pps)] + [
            pl.BlockSpec((tlen, 1024), lambda b, j, pt: (rb0 + b, PK_NQ // 1024)),
            pl.BlockSpec((tlen, 1024), lambda b, j, pt: (rb0 + b, PK_NKV // 1024)),
            pl.BlockSpec((1, wlen, 2 * KV_W), lambda b, j, pt: (b, 0, 0)),
            pl.BlockSpec((2, CMP_STRIDE // 2, 2 * KV_W, 2 * KV_W), lambda b, j, pt: (0, 0, 0, 0)),
            pl.BlockSpec((2, KV_W), lambda b, j, pt: (0, 0)),
            pl.BlockSpec((nseg, 2 * LANES), lambda b, j, pt: (0, 0)),
        ],
        out_specs=pl.BlockSpec((1, tlen, N_WIDTH), lambda b, j, pt: (b, 0, 0)),
        scratch_shapes=[
            pltpu.VMEM((past, LANES), F32), pltpu.VMEM((past, LANES), F32),
            pltpu.VMEM((2 * LANES, past), BF16), pltpu.VMEM((LANES, past), BF16),
            pltpu.VMEM((N_HEADS * tlen, past), F32),
        ],
    )
    return pl.pallas_call(
        kern,
        grid_spec=grid_spec,
        out_shape=jax.ShapeDtypeStruct((bsz, tlen, N_WIDTH), F32),
        compiler_params=_cparams(("arbitrary", "arbitrary")),
        name="nsa_sample",
    )(page_table, *([cache_t] * pps), z, z, win2, wc2, bcmp_p, mmat)


ROW_TILE = 1024
MERGE_TILE = 512
MOE_TILE = 512
INPROJ_COLS = PK_TOTAL // 4
MOE_PARTS = 4
COMBINE_PARTS = 4
PROMPT_CHUNK = 256


def kernel(x_prompt, x_sample, cache_kv, cache_win_kv, state_conv, state_C, state_n, state_m, page_table,
           norm1_g, w_in, b_in, w_conv, b_conv, w_mq, w_mk, g_mnorm, w_cmp, b_cmp,
           w_branch_m, w_branch_n, w_out, norm2_g, w_router, b_router, w_gu, b_gu, w_dn, b_dn, normf_g):
    bp, sp, d = x_prompt.shape
    bs, ts, _ = x_sample.shape
    depth = w_in.shape[0]
    n_p, n_s = bp * sp, bs * ts
    n = n_p + n_s
    assert n_p % ROW_TILE == 0 and n_s % ROW_TILE == 0 and n_p % MERGE_TILE == 0 and n_s % MERGE_TILE == 0
    assert sp % PROMPT_CHUNK == 0 and ts % 8 == 0 and ts >= M_CONV - 1

    assert depth == 1, "only DEPTH == 1 is supported (the final norm is fused into the combine step)"
    xp2, xs2 = x_prompt.reshape(n_p, d), x_sample.reshape(n_s, d)
    st_p, st_s = [], []
    for l in range(depth):
        w_in_p = _pack_cols(w_in[l]).astype(BF16)
        b_in_p = _pack_cols(b_in[l][None])
        z = norm_matmul(xp2, xs2, norm1_g[l][None], w_in_p, b_in_p, ROW_TILE, INPROJ_COLS)

        wq, wk = w_mq[l].astype(BF16), w_mk[l].astype(BF16)
        gn = g_mnorm[l].reshape(1, M_WIDTH)
        bconv = b_conv[l][None]

        def gate_rows(r0, bsz, tlen):
            g = z[r0:r0 + bsz * tlen, PK_SMALL:PK_SMALL + 2 * M_HEADS]
            g = g.reshape(bsz, tlen, 2 * M_HEADS).transpose(0, 2, 1)
            if tlen < LANES:
                g = jnp.pad(g, ((0, 0), (0, 0), (0, LANES - tlen)))
            return g

        zero = lambda *s: jnp.zeros(s, F32)
        hm_p, c_p, nn_p, m_p, tail_p = mlstm(z, gate_rows(0, bp, sp), 0, bp, sp, PROMPT_CHUNK,
                                             zero(bp, 8, M_WIDTH), zero(bp, M_HEADS, M_DQK, M_DV),
                                             zero(bp, M_HEADS, M_DQK), zero(bp, 1, M_HEADS),
                                             w_conv[l], bconv, wq, wk, gn)
        cbuf_s = jnp.pad(state_conv[l], ((0, 0), (8 - (M_CONV - 1), 0), (0, 0)))
        hm_s, c_s, nn_s, m_s, tail_s = mlstm(z, gate_rows(n_p, bs, ts), n_p, bs, ts, ts,
                                             cbuf_s, state_C[l], state_n[l], state_m[l][:, None, :],
                                             w_conv[l], bconv, wq, wk, gn)

        wcmp_p, bcmp_p = _pack_cmp_weights(w_cmp[l], b_cmp[l])
        hn_p, kv_p, win_p = nsa_prompt(z, bp, sp, wcmp_p, bcmp_p)
        wlen_s = cache_win_kv.shape[2]
        win2 = cache_win_kv[l].reshape(bs, wlen_s, 2 * KV_W)
        cache_t = jnp.transpose(cache_kv, (0, 2, 3, 4, 5, 1)).reshape(cache_kv.shape[0], depth * 4 * KV_W,
                                                                      cache_kv.shape[1])
        hn_s = nsa_sample(z, n_p, page_table, cache_t, l, win2, wcmp_p, bcmp_p, ts)

        wr = jnp.pad(w_router[l], ((0, 0), (0, LANES - N_EXPERTS))).astype(BF16)
        br = jnp.pad(b_router[l], (0, LANES - N_EXPERTS))[None]
        h, xn2, top_e, top_w, pos, counts = merge(
            xp2, xs2, hm_p.reshape(n_p, M_WIDTH), hm_s.reshape(n_s, M_WIDTH),
            hn_p, hn_s.reshape(n_s, N_WIDTH).astype(BF16), z,
            w_branch_m[l].astype(BF16), w_branch_n[l].astype(BF16), w_out[l].astype(BF16),
            norm2_g[l][None], wr, br, MERGE_TILE)

        slot_tok, inv_slot, tile_e, n_used = _slot_layout(top_e, pos, counts[0, :N_EXPERTS].astype(jnp.int32),
                                                          MOE_TILE)
        n_tiles = tile_e.shape[0]
        tiles_per = -(-n_tiles // MOE_PARTS)
        y_slots = None
        for t0 in range(0, n_tiles, tiles_per):
            t1 = min(t0 + tiles_per, n_tiles)
            x_part = jnp.take(xn2, slot_tok[t0 * MOE_TILE:t1 * MOE_TILE], axis=0, mode='clip')
            y_slots = expert_ffn(x_part, t0, tile_e, n_used, y_slots, w_gu[l], b_gu[l][:, None, :],
                                 w_dn[l], b_dn[l][:, None, :], MOE_TILE)

        def combine_rows(r0, nrows, out_r0, out_rows, out_prev):
            y4 = jnp.take(y_slots, inv_slot[r0:r0 + nrows].T.reshape(-1), axis=0, mode='clip')
            return combine(h, y4, top_w, normf_g[None], ROW_TILE, r0, out_r0, out_rows, out_prev)

        rows_per = -(-n_p // (COMBINE_PARTS * ROW_TILE)) * ROW_TILE
        y_prompt = None
        for r0 in range(0, n_p, rows_per):
            y_prompt = combine_rows(r0, min(rows_per, n_p - r0), r0, n_p, y_prompt)
        y_prompt = y_prompt.reshape(bp, sp, d)
        y_sample = combine_rows(n_p, n_s, 0, n_s, None).reshape(bs, ts, d)

        kvw_s = z[n_p:, PK_NKV:PK_NKV + 6 * KV_W].reshape(bs, ts, 6, N_KV, N_HD)
        win_s = jnp.concatenate([cache_win_kv[l][:, ts:], kvw_s[:, :, 4:]], axis=1)
        tail0 = 8 - (M_CONV - 1)
        st_p.append((kv_p.reshape(bp, sp, 4, N_KV, N_HD), win_p.reshape(bp, -1, 2, N_KV, N_HD), tail_p[:, tail0:],
                     c_p, nn_p, m_p[:, 0]))
        st_s.append((kvw_s[:, :, :4], win_s, tail_s[:, tail0:], c_s, nn_s, m_s[:, 0]))

    stack = lambda sts, k, axis=0: jnp.stack([s[k] for s in sts], axis=axis)
    return (y_prompt, y_sample,
            stack(st_p, 0, 2), stack(st_p, 1), stack(st_p, 2), stack(st_p, 3), stack(st_p, 4), stack(st_p, 5),
            stack(st_s, 0, 2), stack(st_s, 1), stack(st_s, 2), stack(st_s, 3), stack(st_s, 4), stack(st_s, 5))
```

```python
import functools
import math

import jax
import jax.numpy as jnp
import numpy as np
from jax import lax
from jax.experimental import pallas as pl
from jax.experimental.pallas import tpu as pltpu

F32 = jnp.float32
BF16 = jnp.bfloat16

M_HEADS = 4
M_DV = 256
M_DQK = 128
M_WIDTH = M_HEADS * M_DV
M_CONV = 4
N_HEADS = 16
N_KV = 2
N_HD = 64
N_HPG = N_HEADS // N_KV
N_WIDTH = N_HEADS * N_HD
KV_W = N_KV * N_HD
CMP_LEN = 32
CMP_STRIDE = 16
SLC_BLOCK = 64
N_SELECT = 16
WINDOW = 512
FORCE_BONUS = 1000.0
N_EXPERTS = 32
TOP_K = 4
SWIGLU_LIMIT = 7.0
SWIGLU_ALPHA = 1.702
EPS = 1e-6
TINY = 1e-30
NEG_BIG = -1e30

LANES = 128
PK_MX = 0
PK_MV = 1024
PK_MO = 2048
PK_NQ = 3072
PK_NKV = 4096
PK_SMALL = PK_NKV + 6 * KV_W
PK_BG = 5120
PK_TOTAL = 7168
SM_IG = 0
SM_LF = M_HEADS
SM_NG = 2 * M_HEADS

VMEM_LIMIT = 56 * 1024 * 1024


def _cparams(sem):
    return pltpu.CompilerParams(dimension_semantics=sem, vmem_limit_bytes=VMEM_LIMIT)


def _log_sigmoid(x):
    return jnp.minimum(x, 0.0) - jnp.log1p(jnp.exp(-jnp.abs(x)))


def _dot(a, b):
    return jnp.dot(a, b, preferred_element_type=F32)


def _dot_nt(a, b):
    return lax.dot_general(a, b, (((1,), (1,)), ((), ())), preferred_element_type=F32)


def _dot_tn(a, b):
    return lax.dot_general(a, b, (((0,), (0,)), ((), ())), preferred_element_type=F32)


def _norm_matmul_kernel(xp_ref, xs_ref, g_ref, w_ref, b_ref, o_ref, xn_ref, *, nbp):
    @pl.when(pl.program_id(1) == 0)
    def _():
        x = jnp.where(pl.program_id(0) < nbp, xp_ref[...], xs_ref[...])
        ms = jnp.mean(x * x, axis=-1, keepdims=True)
        xn_ref[...] = (x * lax.rsqrt(ms + EPS) * g_ref[...]).astype(BF16)

    o_ref[...] = _dot(xn_ref[...], w_ref[...]) + b_ref[...]


def norm_matmul(xp, xs, g, w, b, tm, tn):
    n_p, d = xp.shape
    n = n_p + xs.shape[0]
    nbp = n_p // tm
    nc = w.shape[1]
    return pl.pallas_call(
        functools.partial(_norm_matmul_kernel, nbp=nbp),
        grid=(n // tm, nc // tn),
        in_specs=[
            pl.BlockSpec((tm, d), lambda i, j: (jnp.minimum(i, nbp - 1), 0)),
            pl.BlockSpec((tm, d), lambda i, j: (jnp.maximum(i - nbp, 0), 0)),
            pl.BlockSpec((1, d), lambda i, j: (0, 0)),
            pl.BlockSpec((d, tn), lambda i, j: (0, j)),
            pl.BlockSpec((1, tn), lambda i, j: (0, j)),
        ],
        out_specs=pl.BlockSpec((tm, tn), lambda i, j: (i, j)),
        out_shape=jax.ShapeDtypeStruct((n, nc), F32),
        scratch_shapes=[pltpu.VMEM((tm, d), BF16)],
        compiler_params=_cparams(("parallel", "arbitrary")),
        name="norm_matmul",
    )(xp, xs, g, w, b)


def _shift_rows(x, tail, d):
    rows = x.shape[0]
    xd = pltpu.roll(x, d, 0)
    td = pltpu.roll(tail, d, 0)
    head = jnp.where(lax.broadcasted_iota(jnp.int32, td.shape, 0) < d, td, xd[:8])
    if rows == 8:
        return head
    return jnp.concatenate([head, xd[8:]], axis=0)


def _cumsum_rows(x):
    n = x.shape[0]
    idx = lax.broadcasted_iota(jnp.int32, x.shape, 0)
    k = 1
    while k < n:
        x = x + jnp.where(idx >= k, pltpu.roll(x, k, 0), 0.0)
        k *= 2
    return x


def _cumsum_lanes(x, n):
    idx = lax.broadcasted_iota(jnp.int32, x.shape, 1)
    k = 1
    while k < n:
        x = x + jnp.where(idx >= k, pltpu.roll(x, k, 1), 0.0)
        k *= 2
    return x


def _mlstm_kernel(xm_ref, v_ref, o_ref, gc_ref, gr_ref, cbuf_ref, c0_ref, n0_ref, m0_ref,
                  wconv_ref, bconv_ref, wq_ref, wk_ref, gn_ref,
                  h_ref, cout_ref, nout_ref, mout_ref, tout_ref,
                  c_sc, n_sc, m_sc, tail_sc, *, chunk):
    L = chunk
    c = pl.program_id(1)

    @pl.when(c == 0)
    def _init():
        c_sc[...] = c0_ref[0]
        n_sc[...] = n0_ref[0]
        m_sc[...] = m0_ref[0]
        tail_sc[...] = cbuf_ref[0]

    x = xm_ref[...]
    tail = tail_sc[...]
    wc = wconv_ref[...]
    xc = x * wc[M_CONV - 1:M_CONV] + bconv_ref[...]
    for d in range(1, M_CONV):
        xc = xc + _shift_rows(x, tail, d) * wc[M_CONV - 1 - d:M_CONV - d]
    tail_sc[...] = x[L - 8:]
    xc = xc * jax.nn.sigmoid(xc)
    xcb = xc.astype(BF16)

    gc = gc_ref[...]
    gr = gr_ref[0]
    b_col = _cumsum_rows(_log_sigmoid(gc))
    b_row = _cumsum_lanes(_log_sigmoid(gr), L)
    causal = (lax.broadcasted_iota(jnp.int32, (L, L), 1) <= lax.broadcasted_iota(jnp.int32, (L, L), 0))

    vb = v_ref[...].astype(BF16)
    og = jax.nn.sigmoid(o_ref[...])
    gn = gn_ref[...]
    m_all = m_sc[...]
    scale = M_DQK ** -0.5
    m_new_list = []
    for h in range(M_HEADS):
        hs = slice(h * M_DV, (h + 1) * M_DV)
        q = _dot(xcb[:, hs], wq_ref[h])
        k = _dot(xcb[:, hs], wk_ref[h]) * scale
        qb = q.astype(BF16)
        m_prev = m_all[:, h:h + 1]
        bc = b_col[:, SM_LF + h:SM_LF + h + 1]
        igc = gc[:, SM_IG + h:SM_IG + h + 1]
        br = b_row[SM_LF + h:SM_LF + h + 1, :L]
        igr = gr[SM_IG + h:SM_IG + h + 1, :L]
        a = bc + m_prev
        dmat = jnp.where(causal, bc - br + igr, -jnp.inf)
        mt = jnp.maximum(a, jnp.max(dmat, axis=1, keepdims=True))
        w_intra = jnp.exp(dmat - mt)
        w_inter = jnp.exp(a - mt)
        qk = _dot_nt(qb, k.astype(BF16)) * w_intra
        cmat = c_sc[h]
        nrow = n_sc[h:h + 1, :]
        vh = vb[:, hs]
        num = _dot(qk.astype(BF16), vh) + w_inter * _dot(qb, cmat.astype(BF16))
        den = jnp.sum(qk, axis=1, keepdims=True) + w_inter * jnp.sum(q * nrow, axis=1, keepdims=True)
        hh = num * (1.0 / jnp.maximum(jnp.abs(den), jnp.exp(-mt)))
        m_new = mt[L - 1:L, :]
        w_end = jnp.exp(bc[L - 1:L, :] - bc + igc - m_new)
        decay = jnp.exp(a[L - 1:L, :] - m_new)
        kw = k * w_end
        c_sc[h] = decay * cmat + _dot_tn(kw.astype(BF16), vh)
        n_sc[h:h + 1, :] = decay * nrow + jnp.sum(kw, axis=0, keepdims=True)
        m_new_list.append(m_new)
        hn = hh * lax.rsqrt(jnp.mean(hh * hh, axis=-1, keepdims=True) + EPS) * gn[:, hs]
        h_ref[0, :, hs] = (hn * og[:, hs]).astype(h_ref.dtype)
    m_sc[...] = jnp.concatenate(m_new_list, axis=1)

    @pl.when(c == pl.num_programs(1) - 1)
    def _fin():
        cout_ref[0] = c_sc[...]
        nout_ref[0] = n_sc[...]
        mout_ref[0] = m_sc[...]
        tout_ref[0] = tail_sc[...]


def mlstm(z, g_rows, row0, bsz, tlen, chunk, conv_buf8, c0, n0, m0, w_conv, b_conv, wq, wk, gn):
    L = chunk
    nc = tlen // L
    rb0 = row0 // L
    lr = g_rows.shape[2]
    grl = L if nc > 1 else lr

    def zspec(col):
        return pl.BlockSpec((L, 1024), lambda b, c: (rb0 + b * nc + c, col))

    kern = functools.partial(_mlstm_kernel, chunk=L)
    return pl.pallas_call(
        kern,
        grid=(bsz, nc),
        in_specs=[
            zspec(PK_MX // 1024), zspec(PK_MV // 1024), zspec(PK_MO // 1024),
            pl.BlockSpec((L, LANES), lambda b, c: (rb0 + b * nc + c, PK_SMALL // LANES)),
            pl.BlockSpec((1, 8, grl), lambda b, c: (b, 0, c)),
            pl.BlockSpec((1, 8, M_WIDTH), lambda b, c: (b, 0, 0)),
            pl.BlockSpec((1, M_HEADS, M_DQK, M_DV), lambda b, c: (b, 0, 0, 0)),
            pl.BlockSpec((1, M_HEADS, M_DQK), lambda b, c: (b, 0, 0)),
            pl.BlockSpec((1, 1, M_HEADS), lambda b, c: (b, 0, 0)),
            pl.BlockSpec((M_CONV, M_WIDTH), lambda b, c: (0, 0)),
            pl.BlockSpec((1, M_WIDTH), lambda b, c: (0, 0)),
            pl.BlockSpec((M_HEADS, M_DV, M_DQK), lambda b, c: (0, 0, 0)),
            pl.BlockSpec((M_HEADS, M_DV, M_DQK), lambda b, c: (0, 0, 0)),
            pl.BlockSpec((1, M_WIDTH), lambda b, c: (0, 0)),
        ],
        out_specs=[
            pl.BlockSpec((1, L, M_WIDTH), lambda b, c: (b, c, 0)),
            pl.BlockSpec((1, M_HEADS, M_DQK, M_DV), lambda b, c: (b, 0, 0, 0)),
            pl.BlockSpec((1, M_HEADS, M_DQK), lambda b, c: (b, 0, 0)),
            pl.BlockSpec((1, 1, M_HEADS), lambda b, c: (b, 0, 0)),
            pl.BlockSpec((1, 8, M_WIDTH), lambda b, c: (b, 0, 0)),
        ],
        out_shape=[
            jax.ShapeDtypeStruct((bsz, tlen, M_WIDTH), BF16),
            jax.ShapeDtypeStruct((bsz, M_HEADS, M_DQK, M_DV), F32),
            jax.ShapeDtypeStruct((bsz, M_HEADS, M_DQK), F32),
            jax.ShapeDtypeStruct((bsz, 1, M_HEADS), F32),
            jax.ShapeDtypeStruct((bsz, 8, M_WIDTH), F32),
        ],
        scratch_shapes=[
            pltpu.VMEM((M_HEADS, M_DQK, M_DV), F32),
            pltpu.VMEM((M_HEADS, M_DQK), F32),
            pltpu.VMEM((1, M_HEADS), F32),
            pltpu.VMEM((8, M_WIDTH), F32),
        ],
        compiler_params=_cparams(("parallel", "arbitrary")),
        name="mlstm",
    )(z, z, z, z, g_rows, conv_buf8, c0, n0, m0, w_conv, b_conv, wq, wk, gn)


OFF_MX = 0
OFF_MV = OFF_MX + M_WIDTH
OFF_MO = OFF_MV + M_WIDTH
OFF_MI = OFF_MO + M_WIDTH
OFF_MF = OFF_MI + M_HEADS
OFF_NQ = OFF_MF + M_HEADS
OFF_NKV = OFF_NQ + N_WIDTH
OFF_NG = OFF_NKV + 6 * KV_W
OFF_BG = OFF_NG + 3 * N_HEADS


def _pack_cols(a):
    lead = a.shape[:-1]
    d_model = a.shape[-1] - OFF_BG
    small = jnp.concatenate([a[..., OFF_MI:OFF_NQ], a[..., OFF_NG:OFF_BG]], axis=-1)
    small = jnp.concatenate([small, jnp.zeros(lead + (LANES - small.shape[-1],), a.dtype)], axis=-1)
    pad = jnp.zeros(lead + (PK_BG - PK_SMALL - LANES,), a.dtype)
    assert d_model == PK_TOTAL - PK_BG
    return jnp.concatenate([a[..., OFF_MX:OFF_MI], a[..., OFF_NQ:OFF_NKV], a[..., OFF_NKV:OFF_NG],
                            small, pad, a[..., OFF_BG:]], axis=-1)


Q_SCALE = (N_HD ** -0.5) * math.log2(math.e)

def _dup_halves(x, lane):
    r = pltpu.roll(x, N_HD, 1)
    lo = lane < N_HD
    return jnp.where(lo, x, r), jnp.where(lo, r, x)


def _masked_softmax_rows(s, mask):
    s = jnp.where(mask, s, -jnp.inf)
    mx = jnp.max(s, axis=1, keepdims=True)
    mx = jnp.where(mx > -jnp.inf, mx, 0.0)
    p = jnp.exp2(s - mx)
    return p * (1.0 / jnp.maximum(jnp.sum(p, axis=1, keepdims=True), TINY))


def _softmax_pv(s, v):
    m = jnp.max(s, axis=1, keepdims=True)
    p = jnp.exp2(s - m)
    return _dot(p.astype(BF16), v) * (1.0 / jnp.sum(p, axis=1, keepdims=True))


def _slc_attend(qxs, kx_ref, v_ref, g, s_sc, nfull, diag_bias, ck):
    rows = qxs[0].shape[0]
    nl = ck // LANES
    streams = range(len(qxs))
    lane_chunks = lambda s: [s[:, a * LANES:(a + 1) * LANES] for a in range(nl)]

    def scores(c):
        kt = kx_ref[g, pl.ds(pl.multiple_of(c * ck, ck), ck), :]
        return [_dot_nt(qx, kt) for qx in qxs]

    def max_pass(c, mvecs):
        out = []
        for n, s in enumerate(scores(c)):
            s_sc[n, c] = s
            out.append(functools.reduce(jnp.maximum, lane_chunks(s), mvecs[n]))
        return tuple(out)

    mvecs = lax.fori_loop(0, nfull, max_pass, tuple(jnp.full((rows, LANES), -jnp.inf, F32) for _ in streams))
    sds = [s + diag_bias for s in scores(nfull)]
    mbs = [jnp.broadcast_to(jnp.max(functools.reduce(jnp.maximum, lane_chunks(sd), mv), axis=1, keepdims=True),
                            (rows, LANES)) for sd, mv in zip(sds, mvecs)]

    def probs(ss, c):
        vt = v_ref[g, pl.ds(pl.multiple_of(c * ck, ck), ck), :]
        out = []
        for s, mb in zip(ss, mbs):
            ps = [jnp.exp2(x - mb) for x in lane_chunks(s)]
            out.append((functools.reduce(lambda x, y: x + y, ps), _dot(jnp.concatenate(ps, axis=1).astype(BF16), vt)))
        return tuple(out)

    def exp_pass(c, carry):
        new = probs([s_sc[n, c] for n in streams], c)
        return tuple((carry[n][0] + new[n][0], carry[n][1] + new[n][1]) for n in streams)

    res = lax.fori_loop(0, nfull, exp_pass, probs(sds, nfull))
    return [acc * (1.0 / jnp.sum(lvec, axis=1, keepdims=True)) for lvec, acc in res]


def _select_blocks(imp, qpos, ns, axis):
    blk = lax.broadcasted_iota(jnp.int32, imp.shape, axis)
    cur = qpos // SLC_BLOCK
    allowed = blk <= cur
    forced = (blk == 0) | (blk == cur) | (blk == cur - 1)
    score = jnp.where(allowed, imp + jnp.where(forced, FORCE_BONUS, 0.0), -jnp.inf)
    cnt = jnp.zeros(imp.shape, F32)
    for i in range(ns):
        si = score[:, i:i + 1] if axis == 1 else score[i:i + 1, :]
        ahead = (si > score) | ((si == score) & (blk > i))
        cnt = cnt + jnp.where(ahead, 1.0, 0.0)
    return jnp.where((cnt < float(N_SELECT)) & allowed, 1.0, 0.0)


def _stack_q(q_ref, g, rows):
    qs = jnp.concatenate([q_ref[:, (4 * g + p) * LANES:(4 * g + p + 1) * LANES] for p in range(N_HPG // 2)], axis=0)
    qs = qs * Q_SCALE
    lane = lax.broadcasted_iota(jnp.int32, qs.shape, 1)
    qe = jnp.where(lane < N_HD, qs, 0.0).astype(BF16)
    qo = jnp.where(lane < N_HD, 0.0, qs).astype(BF16)
    return qe, qo


def _gate_pair(gs, g, p, j, lane):
    he = g * N_HPG + 2 * p
    ce = SM_NG + 3 * he + j
    co = ce + 3
    return jnp.where(lane < N_HD, gs[:, ce:ce + 1], gs[:, co:co + 1])


def _nsa_prompt_kernel(q_ref, kv_ref, wcmp_ref, bcmp_ref, mmat_ref, o_ref, kvo_ref, wino_ref,
                       ck_sc, cv_sc, kx_sc, sv_sc, wk_sc, wv_sc, stage_sc, s_sc, *, seq, tq, ck, wn):
    i = pl.program_id(1)
    nseg = seq // CMP_STRIDE
    ns = seq // SLC_BLOCK
    npair = N_HPG // 2

    @pl.when(i == 0)
    def _prep():
        kvo_ref[0] = kv_ref[:, 0:4 * KV_W]
        wino_ref[0] = kv_ref[seq - wino_ref.shape[1]:, 4 * KV_W:6 * KV_W]
        kr = lax.broadcasted_iota(jnp.int32, (seq, LANES), 0)
        kc = lax.broadcasted_iota(jnp.int32, (seq, LANES), 1)
        onehot = jnp.where(kr // SLC_BLOCK == kc, 1.0, 0.0).astype(BF16)
        for g in range(N_KV):
            kx_sc[g, :, LANES:] = onehot
        lane = lax.broadcasted_iota(jnp.int32, (nseg, LANES), 1)
        for c, dst in ((0, ck_sc), (1, cv_sc)):
            acc = jnp.zeros((nseg, 2 * LANES), F32)
            stage_sc[...] = kv_ref[:, c * LANES:(c + 1) * LANES]
            for s in range(CMP_STRIDE):
                xs = stage_sc[pl.ds(s, nseg, stride=CMP_STRIDE), :]
                acc = acc + _dot(xs.astype(BF16), wcmp_ref[c, s])
            blocks = acc[:, :LANES] + pltpu.roll(acc[:, LANES:], nseg - 1, 0) + bcmp_ref[c:c + 1, :]
            d0, d1 = _dup_halves(blocks, lane)
            dst[0] = d0.astype(BF16)
            dst[1] = d1.astype(BF16)
        lane_s = lax.broadcasted_iota(jnp.int32, (seq, LANES), 1)
        for off, dst in ((2, kx_sc), (3, sv_sc), (4, wk_sc), (5, wv_sc)):
            d0, d1 = _dup_halves(kv_ref[:, off * LANES:(off + 1) * LANES], lane_s)
            dst[0, :, 0:LANES] = d0.astype(BF16)
            dst[1, :, 0:LANES] = d1.astype(BF16)

    t0 = i * tq
    qpos = t0 + lax.broadcasted_iota(jnp.int32, (tq, 1), 0)
    qpos_row = t0 + lax.broadcasted_iota(jnp.int32, (1, tq), 1)
    nsp = -(-ns // 8) * 8
    lane = lax.broadcasted_iota(jnp.int32, (tq, LANES), 1)
    gs = jax.nn.sigmoid(kv_ref[pl.ds(pl.multiple_of(t0, tq), tq), 6 * LANES:7 * LANES])
    c_end = lax.broadcasted_iota(jnp.int32, (tq, nseg), 1) * CMP_STRIDE + (CMP_LEN - 1)
    cmask = jnp.concatenate([c_end <= qpos] * npair, axis=0)
    lane4 = lax.broadcasted_iota(jnp.int32, (npair * tq, LANES), 1)
    lo4 = lane4 < N_HD

    nfull = t0 // ck
    kpos_d = nfull * ck + lax.broadcasted_iota(jnp.int32, (tq, ck), 1)
    diag_bias = jnp.concatenate([jnp.where(kpos_d <= qpos, 0.0, NEG_BIG)] * npair, axis=0)
    ks = pl.multiple_of(jnp.clip(t0 + tq - wn, 0, seq - wn), LANES)
    dpos = qpos - (ks + lax.broadcasted_iota(jnp.int32, (tq, wn), 1))
    win_bias = jnp.concatenate([jnp.where((dpos >= 0) & (dpos < WINDOW), 0.0, NEG_BIG)] * npair, axis=0)

    for g in range(N_KV):
        qe, qo = _stack_q(q_ref, g, tq)
        ckg = ck_sc[g]
        cvg = cv_sc[g]
        p_e = _masked_softmax_rows(_dot_nt(qe, ckg), cmask)
        p_o = _masked_softmax_rows(_dot_nt(qo, ckg), cmask)
        o_cmp = jnp.where(lo4, _dot(p_e.astype(BF16), cvg), _dot(p_o.astype(BF16), cvg))
        psum = p_e + p_o
        psum = sum(psum[p * tq:(p + 1) * tq] for p in range(npair))
        imp_t = lax.dot_general(mmat_ref[...], psum, (((1,), (1,)), ((), ())),
                                precision=lax.Precision.HIGHEST, preferred_element_type=F32)
        sel_t = _select_blocks(imp_t[:nsp], qpos_row, ns, 0)
        sel = jnp.concatenate([sel_t, jnp.zeros((LANES - nsp, tq), F32)], axis=0).T
        selb = ((sel - 1.0) * (-NEG_BIG)).astype(BF16)
        selb4 = jnp.concatenate([selb] * npair, axis=0)

        o_e, o_o = _slc_attend([jnp.concatenate([qe, selb4], axis=1), jnp.concatenate([qo, selb4], axis=1)],
                               kx_sc, sv_sc, g, s_sc, nfull, diag_bias, ck)
        o_slc = jnp.where(lo4, o_e, o_o)

        kw = wk_sc[g, pl.ds(ks, wn), :]
        vw = wv_sc[g, pl.ds(ks, wn), :]
        o_win = jnp.where(lo4, _softmax_pv(_dot_nt(qe, kw) + win_bias, vw), _softmax_pv(_dot_nt(qo, kw) + win_bias, vw))

        for p in range(npair):
            rs = slice(p * tq, (p + 1) * tq)
            o = (_gate_pair(gs, g, p, 0, lane) * o_cmp[rs] + _gate_pair(gs, g, p, 1, lane) * o_slc[rs]
                 + _gate_pair(gs, g, p, 2, lane) * o_win[rs])
            c0 = (g * npair + p) * LANES
            o_ref[:, c0:c0 + LANES] = o.astype(o_ref.dtype)


def _cmp_to_slc_matrix(nc, ns, rows, cols):
    c0 = np.arange(nc)[:, None] * CMP_STRIDE
    s0 = np.arange(ns)[None, :] * SLC_BLOCK
    ov = np.clip(np.minimum(c0 + CMP_LEN, s0 + SLC_BLOCK) - np.maximum(c0, s0), 0, None) / CMP_LEN
    out = np.zeros((rows, cols), np.float32)
    out[:nc, :ns] = ov
    return jnp.asarray(out)


def _pack_cmp_weights(w_cmp, b_cmp):
    r = CMP_LEN // CMP_STRIDE
    w = w_cmp.reshape(2, r, CMP_STRIDE, N_HD, N_HD)
    eye = jnp.eye(N_KV, dtype=w.dtype)
    bd = jnp.einsum('gh,crsde->crsgdhe', eye, w).reshape(2, r, CMP_STRIDE, KV_W, KV_W)
    wp = jnp.concatenate([bd[:, k] for k in range(r)], axis=-1)
    bp = jnp.concatenate([b_cmp] * N_KV, axis=-1)
    return wp.astype(BF16), bp


def nsa_prompt(z, bsz, seq, wcmp_p, bcmp_p, tq=256, ck=256):
    nseg = seq // CMP_STRIDE
    ns = seq // SLC_BLOCK
    nq = seq // tq
    ck = min(ck, seq)
    wn = min(WINDOW + tq, seq)
    assert seq % ck == 0 and ck % tq == 0 and ns <= LANES and seq % LANES == 0
    mmat = _cmp_to_slc_matrix(nseg - 1, ns, nseg, LANES).T
    wlen = min(WINDOW, seq)
    once = pl.Buffered(1)
    kern = functools.partial(_nsa_prompt_kernel, seq=seq, tq=tq, ck=ck, wn=wn)
    return pl.pallas_call(
        kern,
        grid=(bsz, nq),
        in_specs=[
            pl.BlockSpec((tq, 1024), lambda b, i: (b * nq + i, PK_NQ // 1024)),
            pl.BlockSpec((seq, 1024), lambda b, i: (b, PK_NKV // 1024), pipeline_mode=once),
            pl.BlockSpec((2, CMP_STRIDE, KV_W, 2 * KV_W), lambda b, i: (0, 0, 0, 0)),
            pl.BlockSpec((2, KV_W), lambda b, i: (0, 0)),
            pl.BlockSpec((LANES, nseg), lambda b, i: (0, 0)),
        ],
        out_specs=[pl.BlockSpec((tq, N_WIDTH), lambda b, i: (b * nq + i, 0)),
                   pl.BlockSpec((1, seq, 4 * KV_W), lambda b, i: (b, 0, 0), pipeline_mode=once),
                   pl.BlockSpec((1, wlen, 2 * KV_W), lambda b, i: (b, 0, 0), pipeline_mode=once)],
        out_shape=[jax.ShapeDtypeStruct((bsz * seq, N_WIDTH), BF16),
                   jax.ShapeDtypeStruct((bsz, seq, 4 * KV_W), F32),
                   jax.ShapeDtypeStruct((bsz, wlen, 2 * KV_W), F32)],
        scratch_shapes=[
            pltpu.VMEM((N_KV, nseg, LANES), BF16), pltpu.VMEM((N_KV, nseg, LANES), BF16),
            pltpu.VMEM((N_KV, seq, 2 * LANES), BF16), pltpu.VMEM((N_KV, seq, LANES), BF16),
            pltpu.VMEM((N_KV, seq, LANES), BF16), pltpu.VMEM((N_KV, seq, LANES), BF16),
            pltpu.VMEM((seq, LANES), F32),
            pltpu.VMEM((2, seq // ck, (N_HPG // 2) * tq, ck), F32),
        ],
        compiler_params=_cparams(("parallel", "arbitrary")),
        name="nsa_prompt",
    )(z, z, wcmp_p, bcmp_p, mmat)


def _merge_kernel(xp_ref, xs_ref, hmp_ref, hms_ref, hnp_ref, hns_ref, bgm_ref, bgn_ref,
                  wbm_ref, wbn_ref, wout_ref, n2_ref, wr_ref, br_ref,
                  h_ref, xn2_ref, te_ref, tw_ref, pos_ref, cnt_ref, run_sc, *, nbp):
    i = pl.program_id(0)
    is_p = i < nbp
    x = jnp.where(is_p, xp_ref[...], xs_ref[...])
    hm = jnp.where(is_p, hmp_ref[...], hms_ref[...])
    hn = jnp.where(is_p, hnp_ref[...], hns_ref[...])
    t = (jax.nn.sigmoid(bgm_ref[...]) * _dot(hm, wbm_ref[...])
         + jax.nn.sigmoid(bgn_ref[...]) * _dot(hn, wbn_ref[...]))
    h = x + _dot(t.astype(BF16), wout_ref[...])
    h_ref[...] = h
    xn2 = (h * lax.rsqrt(jnp.mean(h * h, axis=-1, keepdims=True) + EPS) * n2_ref[...]).astype(BF16)
    xn2_ref[...] = xn2

    @pl.when(i == 0)
    def _():
        run_sc[...] = jnp.zeros_like(run_sc)

    logits = _dot(xn2, wr_ref[...]) + br_ref[...]
    lane = lax.broadcasted_iota(jnp.int32, logits.shape, 1)
    cur = jnp.where(lane < N_EXPERTS, logits, -jnp.inf)
    vals, idxs, hots = [], [], []
    for _ in range(TOP_K):
        m = jnp.max(cur, axis=1, keepdims=True)
        idx = jnp.min(jnp.where(cur == m, lane, LANES), axis=1, keepdims=True)
        hot = lane == idx
        vals.append(m)
        idxs.append(idx)
        hots.append(hot)
        cur = jnp.where(hot, -jnp.inf, cur)
    es = [jnp.exp(v - vals[0]) for v in vals]
    inv = 1.0 / functools.reduce(lambda a, b: a + b, es)
    tw_ref[...] = jnp.concatenate([e * inv for e in es], axis=1)
    te_ref[...] = jnp.concatenate(idxs, axis=1)
    cnt = functools.reduce(lambda a, b: a + b, [jnp.where(hot, 1.0, 0.0) for hot in hots])
    incl = _cumsum_rows(cnt)
    before = incl - cnt + run_sc[...]
    pos = [jnp.sum(jnp.where(hot, before, 0.0), axis=1, keepdims=True) for hot in hots]
    pos_ref[...] = jnp.concatenate(pos, axis=1).astype(jnp.int32)
    run_sc[...] = run_sc[...] + incl[incl.shape[0] - 1:, :]
    cnt_ref[...] = run_sc[...]


def merge(xp, xs, hmp, hms, hnp, hns, z, wbm, wbn, wout, n2, wr, br, tm):
    n_p, d = xp.shape
    n = n_p + xs.shape[0]
    nbp = n_p // tm
    row = lambda i: (i, 0)
    prow = lambda i: (jnp.minimum(i, nbp - 1), 0)
    srow = lambda i: (jnp.maximum(i - nbp, 0), 0)
    fixed = lambda i: (0, 0)
    pair = [pl.BlockSpec((tm, d), prow), pl.BlockSpec((tm, d), srow)]
    return pl.pallas_call(
        functools.partial(_merge_kernel, nbp=nbp),
        grid=(n // tm,),
        in_specs=pair * 3 + [
            pl.BlockSpec((tm, d), lambda i: (i, PK_BG // 1024)),
            pl.BlockSpec((tm, d), lambda i: (i, PK_BG // 1024 + 1)),
            pl.BlockSpec((d, d), fixed), pl.BlockSpec((d, d), fixed), pl.BlockSpec((d, d), fixed),
            pl.BlockSpec((1, d), fixed), pl.BlockSpec((d, LANES), fixed), pl.BlockSpec((1, LANES), fixed),
        ],
        out_specs=[pl.BlockSpec((tm, d), row), pl.BlockSpec((tm, d), row),
                   pl.BlockSpec((tm, TOP_K), row), pl.BlockSpec((tm, TOP_K), row), pl.BlockSpec((tm, TOP_K), row),
                   pl.BlockSpec((1, LANES), fixed)],
        out_shape=[jax.ShapeDtypeStruct((n, d), F32), jax.ShapeDtypeStruct((n, d), BF16),
                   jax.ShapeDtypeStruct((n, TOP_K), jnp.int32), jax.ShapeDtypeStruct((n, TOP_K), F32),
                   jax.ShapeDtypeStruct((n, TOP_K), jnp.int32), jax.ShapeDtypeStruct((1, LANES), F32)],
        scratch_shapes=[pltpu.VMEM((1, LANES), F32)],
        compiler_params=_cparams(("arbitrary",)),
        name="merge",
    )(xp, xs, hmp, hms, hnp, hns, z, z, wbm, wbn, wout, n2, wr, br)


def _expert_kernel(te_ref, nu_ref, x_ref, wgu_ref, bgu_ref, wdn_ref, bdn_ref, *rest, tile0):
    y_ref, wgu_sc, wdn_sc = rest[-3:]
    t = pl.program_id(0)
    tg = tile0 + t

    @pl.when(tg < nu_ref[0])
    def _():
        @pl.when((t == 0) | (te_ref[tg] != te_ref[jnp.maximum(tg - 1, 0)]))
        def _cast():
            wgu_sc[...] = wgu_ref[0].astype(BF16)
            wdn_sc[...] = wdn_ref[0].astype(BF16)

        de = wdn_sc.shape[0]
        gu = _dot(x_ref[...], wgu_sc[...]) + bgu_ref[0]
        gate = jnp.minimum(gu[:, :de], SWIGLU_LIMIT)
        up = jnp.clip(gu[:, de:], -SWIGLU_LIMIT, SWIGLU_LIMIT)
        glu = gate * jax.nn.sigmoid(gate * SWIGLU_ALPHA)
        act = ((up + 1.0) * glu).astype(BF16)
        y_ref[...] = _dot(act, wdn_sc[...]) + bdn_ref[0]


def expert_ffn(x_part, tile0, tile_e, n_used, y_prev, wgu, bgu, wdn, bdn, tm):
    d = x_part.shape[1]
    de = wdn.shape[1]
    n_part = x_part.shape[0] // tm
    n_all = tile_e.shape[0]

    def local(t, nu):
        return jnp.clip(jnp.minimum(tile0 + t, nu[0] - 1) - tile0, 0, n_part - 1)

    per_e = lambda t, te, nu: (te[tile0 + t], 0, 0)
    in_specs = [
        pl.BlockSpec((tm, d), lambda t, te, nu: (local(t, nu), 0)),
        pl.BlockSpec((1, d, 2 * de), per_e), pl.BlockSpec((1, 1, 2 * de), per_e),
        pl.BlockSpec((1, de, d), per_e), pl.BlockSpec((1, 1, d), per_e),
    ]
    operands = [tile_e, n_used, x_part, wgu, bgu, wdn, bdn]
    aliases = {}
    if y_prev is not None:
        in_specs.append(pl.BlockSpec(memory_space=pl.ANY))
        aliases = {len(operands): 0}
        operands.append(y_prev)
    grid_spec = pltpu.PrefetchScalarGridSpec(
        num_scalar_prefetch=2,
        grid=(n_part,),
        in_specs=in_specs,
        out_specs=pl.BlockSpec((tm, d), lambda t, te, nu: (tile0 + local(t, nu), 0)),
        scratch_shapes=[pltpu.VMEM((d, 2 * de), BF16), pltpu.VMEM((de, d), BF16)],
    )
    return pl.pallas_call(
        functools.partial(_expert_kernel, tile0=tile0),
        grid_spec=grid_spec,
        out_shape=jax.ShapeDtypeStruct((n_all * tm, d), F32),
        input_output_aliases=aliases,
        compiler_params=_cparams(("arbitrary",)),
        name="expert_ffn",
    )(*operands)


def _combine_kernel(h_ref, *refs):
    y_refs = refs[:TOP_K]
    w_ref, g_ref = refs[TOP_K:TOP_K + 2]
    o_ref = refs[-1]
    w = w_ref[...]
    moe = w[:, 0:1] * y_refs[0][...]
    for k in range(1, TOP_K):
        moe = moe + w[:, k:k + 1] * y_refs[k][...]
    acc = h_ref[...] + moe
    o_ref[...] = acc * lax.rsqrt(jnp.mean(acc * acc, axis=-1, keepdims=True) + EPS) * g_ref[...]


def combine(h, y4, w4, g, tm, row0, out_row0, out_rows, out_prev):
    d = h.shape[1]
    nrows = y4.shape[0] // TOP_K
    nb = nrows // tm
    rb0 = row0 // tm
    ob0 = out_row0 // tm
    in_specs = ([pl.BlockSpec((tm, d), lambda i: (rb0 + i, 0))]
                + [pl.BlockSpec((tm, d), functools.partial(lambda i, k: (k * nb + i, 0), k=k)) for k in range(TOP_K)]
                + [pl.BlockSpec((tm, TOP_K), lambda i: (rb0 + i, 0)), pl.BlockSpec((1, d), lambda i: (0, 0))])
    operands = [h] + [y4] * TOP_K + [w4, g]
    aliases = {}
    if out_prev is not None:
        in_specs.append(pl.BlockSpec(memory_space=pl.ANY))
        aliases = {len(operands): 0}
        operands.append(out_prev)
    return pl.pallas_call(
        _combine_kernel,
        grid=(nb,),
        in_specs=in_specs,
        out_specs=pl.BlockSpec((tm, d), lambda i: (ob0 + i, 0)),
        out_shape=jax.ShapeDtypeStruct((out_rows, d), F32),
        input_output_aliases=aliases,
        compiler_params=_cparams(("parallel",)),
        name="combine",
    )(*operands)


def _slot_layout(top_e, pos, counts, tm):
    n = top_e.shape[0]
    nk = n * TOP_K
    padded = (counts + tm - 1) // tm * tm
    pad_end = jnp.cumsum(padded)
    pad_start = pad_end - padded
    grp_start = jnp.cumsum(counts) - counts
    n_tiles = -(-(nk + N_EXPERTS * (tm - 1)) // tm)
    tile_e = jnp.minimum(jnp.sum(pad_end[None, :] <= (jnp.arange(n_tiles) * tm)[:, None], axis=1),
                         N_EXPERTS - 1).astype(jnp.int32)
    hot = top_e[..., None] == jnp.arange(N_EXPERTS)
    inv_slot = (jnp.sum(jnp.where(hot, pad_start, 0), axis=-1) + pos).astype(jnp.int32)
    order = jnp.argsort(top_e.reshape(-1))
    slot = jnp.arange(n_tiles * tm, dtype=jnp.int32)
    slot_e = jnp.repeat(tile_e, tm)
    src = jnp.clip(grp_start[slot_e] + slot - pad_start[slot_e], 0, nk - 1)
    slot_tok = (jnp.take(order, src, mode='clip') // TOP_K).astype(jnp.int32)
    n_used = (pad_end[-1:] // tm).astype(jnp.int32)
    return slot_tok, inv_slot, tile_e, n_used


def _plain_softmax_parts(parts):
    ms = [jnp.where(m, s, NEG_BIG) if m is not None else s for s, m in parts]
    mx = functools.reduce(jnp.maximum, [jnp.max(s, axis=1, keepdims=True) for s in ms])
    ps = [jnp.exp2(s - mx) for s in ms]
    den = functools.reduce(lambda a, b: a + b, [jnp.sum(p, axis=1, keepdims=True) for p in ps])
    inv = 1.0 / den
    return [p * inv for p in ps]


def _nsa_sample_kernel(pt_ref, *refs, npages, pps, tlen, page, wlen):
    page_refs = refs[:pps]
    (q_ref, kvn_ref, win_ref, wc_ref, bc_ref, mmat_ref, o_ref,
     stk_sc, stv_sc, kx_sc, v_sc, s_sc) = refs[pps:]
    b = pl.program_id(0)
    j = pl.program_id(1)
    nsteps = npages // pps
    past = npages * page
    nseg = past // CMP_STRIDE
    nblk = past // SLC_BLOCK
    ns = nblk + 1
    rows = N_HEADS * tlen
    grows = N_HPG * tlen

    @pl.when((b == 0) & (j == 0))
    def _onehot():
        r = lax.broadcasted_iota(jnp.int32, (LANES, past), 0)
        c = lax.broadcasted_iota(jnp.int32, (LANES, past), 1)
        kx_sc[LANES:, :] = jnp.where(c // SLC_BLOCK == r, 1.0, 0.0).astype(BF16)

    for k in range(pps):
        r0 = pl.multiple_of((j * pps + k) * page, page)
        pg = page_refs[k]
        stk_sc[pl.ds(r0, page), :] = pg[0, 0:LANES, :].T
        stv_sc[pl.ds(r0, page), :] = pg[0, LANES:2 * LANES, :].T
        kx_sc[0:LANES, pl.ds(r0, page)] = pg[0, 2 * LANES:3 * LANES, :].astype(BF16)
        v_sc[:, pl.ds(r0, page)] = pg[0, 3 * LANES:4 * LANES, :].astype(BF16)

    @pl.when(j == nsteps - 1)
    def _attend():
        lane = lax.broadcasted_iota(jnp.int32, (tlen, LANES), 1)
        lo = lane < N_HD

        def cmp_proj(st_sc, c):
            acc = jnp.zeros((nseg, 2 * LANES), F32)
            for sp in range(CMP_STRIDE // 2):
                xa = st_sc[pl.ds(2 * sp, nseg, stride=CMP_STRIDE), :]
                xb = st_sc[pl.ds(2 * sp + 1, nseg, stride=CMP_STRIDE), :]
                acc = acc + _dot(jnp.concatenate([xa, xb], axis=1).astype(BF16), wc_ref[c, sp])
            blocks = acc[:, :LANES] + pltpu.roll(acc[:, LANES:], nseg - 1, 0) + bc_ref[c:c + 1, :]
            return blocks.astype(BF16)

        ck = cmp_proj(stk_sc, 0)
        cv = cmp_proj(stv_sc, 1)

        qall = q_ref[...] * Q_SCALE
        pieces = []
        for h in range(N_HEADS):
            slab = qall[:, (h // 2) * LANES:(h // 2 + 1) * LANES]
            g = h // N_HPG
            if (h % 2) != g:
                slab = pltpu.roll(slab, N_HD, 1)
            pieces.append(jnp.where(lo, slab, 0.0) if g == 0 else jnp.where(lo, 0.0, slab))
        qbd = jnp.concatenate(pieces, axis=0).astype(BF16)
        trow = lax.broadcasted_iota(jnp.int32, (rows, 1), 0) & (tlen - 1)
        qpos = past + trow
        t8 = lax.broadcasted_iota(jnp.int32, (tlen, 1), 0)

        kvn = kvn_ref[...]
        zpad = jnp.zeros((LANES - tlen, LANES), F32)
        newk = lambda off: jnp.concatenate([kvn[:, off * LANES:(off + 1) * LANES], zpad], axis=0).astype(BF16)
        new_lane = lax.broadcasted_iota(jnp.int32, (rows, LANES), 1)
        new_mask = new_lane <= trow

        c_end = lax.broadcasted_iota(jnp.int32, (rows, nseg), 1) * CMP_STRIDE + (CMP_LEN - 1)
        p_cmp = _masked_softmax_rows(_dot_nt(qbd, ck), c_end <= qpos)
        o_cmp = _dot(p_cmp.astype(BF16), cv)

        bias_rows, bias_new = [], []
        for g in range(N_KV):
            psum = sum(p_cmp[g * grows + h * tlen:g * grows + (h + 1) * tlen] for h in range(N_HPG))
            imp = jnp.dot(psum, mmat_ref[...], precision=lax.Precision.HIGHEST, preferred_element_type=F32)
            sel = _select_blocks(imp, past + t8, ns, 1)
            selb = (sel - 1.0) * (-NEG_BIG)
            bias_rows += [selb[:, :LANES]] * N_HPG
            bias_new += [selb[:, nblk:nblk + 1]] * N_HPG
        qx = jnp.concatenate([qbd, jnp.concatenate(bias_rows, axis=0).astype(BF16)], axis=1)
        bias_new = jnp.concatenate(bias_new, axis=0)

        ckeys = min(past, 1024)
        mx = jnp.full((rows, 1), -jnp.inf, F32)
        for c0 in range(0, past, ckeys):
            s = _dot(qx, kx_sc[:, c0:c0 + ckeys])
            s_sc[:, c0:c0 + ckeys] = s
            mx = jnp.maximum(mx, jnp.max(s, axis=1, keepdims=True))
        s_new = jnp.where(new_mask, _dot_nt(qbd, newk(2)) + bias_new, NEG_BIG)
        mx = jnp.maximum(mx, jnp.max(s_new, axis=1, keepdims=True))
        p_new = jnp.exp2(s_new - mx)
        den = jnp.sum(p_new, axis=1, keepdims=True)
        acc = _dot(p_new.astype(BF16), newk(3))
        for c0 in range(0, past, ckeys):
            p = jnp.exp2(s_sc[:, c0:c0 + ckeys] - mx)
            den = den + jnp.sum(p, axis=1, keepdims=True)
            acc = acc + _dot_nt(p.astype(BF16), v_sc[:, c0:c0 + ckeys])
        o_slc = acc * (1.0 / den)

        wr = lax.broadcasted_iota(jnp.int32, (rows, wlen), 1)
        dpos = trow + wlen - wr
        wk_old = win_ref[0, :, 0:LANES].astype(BF16)
        wv_old = win_ref[0, :, LANES:2 * LANES].astype(BF16)
        pw_old, pw_new = _plain_softmax_parts([(_dot_nt(qbd, wk_old), (dpos >= 0) & (dpos < WINDOW)),
                                               (_dot_nt(qbd, newk(4)), new_mask)])
        o_win = _dot(pw_old.astype(BF16), wv_old) + _dot(pw_new.astype(BF16), newk(5))

        gs = jax.nn.sigmoid(kvn[:, 6 * LANES:7 * LANES])
        for pr in range(N_HEADS // 2):
            halves = []
            for par in range(2):
                h = 2 * pr + par
                g = h // N_HPG
                rs = slice(h * tlen, (h + 1) * tlen)
                c0 = SM_NG + 3 * h
                o = (gs[:, c0:c0 + 1] * o_cmp[rs] + gs[:, c0 + 1:c0 + 2] * o_slc[rs]
                     + gs[:, c0 + 2:c0 + 3] * o_win[rs])
                halves.append(pltpu.roll(o, N_HD, 1) if par != g else o)
            o_ref[0, :, pr * LANES:(pr + 1) * LANES] = jnp.where(lo, halves[0], halves[1])


def nsa_sample(z, row0, page_table, cache_t, layer, win2, wcmp_p, bcmp_p, tlen, pps=32):
    bsz, npages = page_table.shape
    pps = min(pps, npages)
    page = cache_t.shape[2]
    wlen = win2.shape[1]
    past = npages * page
    nseg = past // CMP_STRIDE
    nblk = past // SLC_BLOCK
    assert npages % pps == 0 and nblk <= LANES and tlen & (tlen - 1) == 0 and tlen <= SLC_BLOCK and wlen == WINDOW
    mmat = _cmp_to_slc_matrix(nseg - 1, nblk + 1, nseg, 2 * LANES)
    wc2 = wcmp_p.reshape(2, CMP_STRIDE // 2, 2 * KV_W, 2 * KV_W)
    rb0 = row0 // tlen
    kern = functools.partial(_nsa_sample_kernel, npages=npages, pps=pps, tlen=tlen, page=page, wlen=wlen)

    def page_spec(k):
        return pl.BlockSpec((1, 4 * KV_W, page), lambda b, j, pt: (pt[b, j * pps + k], layer, 0))

    grid_spec = pltpu.PrefetchScalarGridSpec(
        num_scalar_prefetch=1,
        grid=(bsz, npages // pps),
        in_specs=[page_spec(k) for k in range(pps)] + [
            pl.BlockSpec((tlen, 1024), lambda b, j, pt: (rb0 + b, PK_NQ // 1024)),
            pl.BlockSpec((tlen, 1024), lambda b, j, pt: (rb0 + b, PK_NKV // 1024)),
            pl.BlockSpec((1, wlen, 2 * KV_W), lambda b, j, pt: (b, 0, 0)),
            pl.BlockSpec((2, CMP_STRIDE // 2, 2 * KV_W, 2 * KV_W), lambda b, j, pt: (0, 0, 0, 0)),
            pl.BlockSpec((2, KV_W), lambda b, j, pt: (0, 0)),
            pl.BlockSpec((nseg, 2 * LANES), lambda b, j, pt: (0, 0)),
        ],
        out_specs=pl.BlockSpec((1, tlen, N_WIDTH), lambda b, j, pt: (b, 0, 0)),
        scratch_shapes=[
            pltpu.VMEM((past, LANES), F32), pltpu.VMEM((past, LANES), F32),
            pltpu.VMEM((2 * LANES, past), BF16), pltpu.VMEM((LANES, past), BF16),
            pltpu.VMEM((N_HEADS * tlen, past), F32),
        ],
    )
    return pl.pallas_call(
        kern,
        grid_spec=grid_spec,
        out_shape=jax.ShapeDtypeStruct((bsz, tlen, N_WIDTH), F32),
        compiler_params=_cparams(("arbitrary", "arbitrary")),
        name="nsa_sample",
    )(page_table, *([cache_t] * pps), z, z, win2, wc2, bcmp_p, mmat)


ROW_TILE = 1024
MERGE_TILE = 512
MOE_TILE = 512
INPROJ_COLS = PK_TOTAL // 4
MOE_PARTS = 4
COMBINE_PARTS = 4
PROMPT_CHUNK = 512


def kernel(x_prompt, x_sample, cache_kv, cache_win_kv, state_conv, state_C, state_n, state_m, page_table,
           norm1_g, w_in, b_in, w_conv, b_conv, w_mq, w_mk, g_mnorm, w_cmp, b_cmp,
           w_branch_m, w_branch_n, w_out, norm2_g, w_router, b_router, w_gu, b_gu, w_dn, b_dn, normf_g):
    bp, sp, d = x_prompt.shape
    bs, ts, _ = x_sample.shape
    depth = w_in.shape[0]
    n_p, n_s = bp * sp, bs * ts
    n = n_p + n_s
    assert n_p % ROW_TILE == 0 and n_s % ROW_TILE == 0 and n_p % MERGE_TILE == 0 and n_s % MERGE_TILE == 0
    assert sp % PROMPT_CHUNK == 0 and ts % 8 == 0 and ts >= M_CONV - 1

    assert depth == 1, "only DEPTH == 1 is supported (the final norm is fused into the combine step)"
    xp2, xs2 = x_prompt.reshape(n_p, d), x_sample.reshape(n_s, d)
    st_p, st_s = [], []
    for l in range(depth):
        w_in_p = _pack_cols(w_in[l]).astype(BF16)
        b_in_p = _pack_cols(b_in[l][None])
        z = norm_matmul(xp2, xs2, norm1_g[l][None], w_in_p, b_in_p, ROW_TILE, INPROJ_COLS)

        wq, wk = w_mq[l].astype(BF16), w_mk[l].astype(BF16)
        gn = g_mnorm[l].reshape(1, M_WIDTH)
        bconv = b_conv[l][None]

        def gate_rows(r0, bsz, tlen):
            g = z[r0:r0 + bsz * tlen, PK_SMALL:PK_SMALL + 2 * M_HEADS]
            g = g.reshape(bsz, tlen, 2 * M_HEADS).transpose(0, 2, 1)
            if tlen < LANES:
                g = jnp.pad(g, ((0, 0), (0, 0), (0, LANES - tlen)))
            return g

        zero = lambda *s: jnp.zeros(s, F32)
        hm_p, c_p, nn_p, m_p, tail_p = mlstm(z, gate_rows(0, bp, sp), 0, bp, sp, PROMPT_CHUNK,
                                             zero(bp, 8, M_WIDTH), zero(bp, M_HEADS, M_DQK, M_DV),
                                             zero(bp, M_HEADS, M_DQK), zero(bp, 1, M_HEADS),
                                             w_conv[l], bconv, wq, wk, gn)
        cbuf_s = jnp.pad(state_conv[l], ((0, 0), (8 - (M_CONV - 1), 0), (0, 0)))
        hm_s, c_s, nn_s, m_s, tail_s = mlstm(z, gate_rows(n_p, bs, ts), n_p, bs, ts, ts,
                                             cbuf_s, state_C[l], state_n[l], state_m[l][:, None, :],
                                             w_conv[l], bconv, wq, wk, gn)

        wcmp_p, bcmp_p = _pack_cmp_weights(w_cmp[l], b_cmp[l])
        hn_p, kv_p, win_p = nsa_prompt(z, bp, sp, wcmp_p, bcmp_p)
        wlen_s = cache_win_kv.shape[2]
        win2 = cache_win_kv[l].reshape(bs, wlen_s, 2 * KV_W)
        cache_t = jnp.transpose(cache_kv, (0, 2, 3, 4, 5, 1)).reshape(cache_kv.shape[0], depth * 4 * KV_W,
                                                                      cache_kv.shape[1])
        hn_s = nsa_sample(z, n_p, page_table, cache_t, l, win2, wcmp_p, bcmp_p, ts)

        wr = jnp.pad(w_router[l], ((0, 0), (0, LANES - N_EXPERTS))).astype(BF16)
        br = jnp.pad(b_router[l], (0, LANES - N_EXPERTS))[None]
        h, xn2, top_e, top_w, pos, counts = merge(
            xp2, xs2, hm_p.reshape(n_p, M_WIDTH), hm_s.reshape(n_s, M_WIDTH),
            hn_p, hn_s.reshape(n_s, N_WIDTH).astype(BF16), z,
            w_branch_m[l].astype(BF16), w_branch_n[l].astype(BF16), w_out[l].astype(BF16),
            norm2_g[l][None], wr, br, MERGE_TILE)

        slot_tok, inv_slot, tile_e, n_used = _slot_layout(top_e, pos, counts[0, :N_EXPERTS].astype(jnp.int32),
                                                          MOE_TILE)
        n_tiles = tile_e.shape[0]
        tiles_per = -(-n_tiles // MOE_PARTS)
        y_slots = None
        for t0 in range(0, n_tiles, tiles_per):
            t1 = min(t0 + tiles_per, n_tiles)
            x_part = jnp.take(xn2, slot_tok[t0 * MOE_TILE:t1 * MOE_TILE], axis=0, mode='clip')
            y_slots = expert_ffn(x_part, t0, tile_e, n_used, y_slots, w_gu[l], b_gu[l][:, None, :],
                                 w_dn[l], b_dn[l][:, None, :], MOE_TILE)

        def combine_rows(r0, nrows, out_r0, out_rows, out_prev):
            y4 = jnp.take(y_slots, inv_slot[r0:r0 + nrows].T.reshape(-1), axis=0, mode='clip')
            return combine(h, y4, top_w, normf_g[None], ROW_TILE, r0, out_r0, out_rows, out_prev)

        rows_per = -(-n_p // (COMBINE_PARTS * ROW_TILE)) * ROW_TILE
        y_prompt = None
        for r0 in range(0, n_p, rows_per):
            y_prompt = combine_rows(r0, min(rows_per, n_p - r0), r0, n_p, y_prompt)
        y_prompt = y_prompt.reshape(bp, sp, d)
        y_sample = combine_rows(n_p, n_s, 0, n_s, None).reshape(bs, ts, d)

        kvw_s = z[n_p:, PK_NKV:PK_NKV + 6 * KV_W].reshape(bs, ts, 6, N_KV, N_HD)
        win_s = jnp.concatenate([cache_win_kv[l][:, ts:], kvw_s[:, :, 4:]], axis=1)
        tail0 = 8 - (M_CONV - 1)
        st_p.append((kv_p.reshape(bp, sp, 4, N_KV, N_HD), win_p.reshape(bp, -1, 2, N_KV, N_HD), tail_p[:, tail0:],
                     c_p, nn_p, m_p[:, 0]))
        st_s.append((kvw_s[:, :, :4], win_s, tail_s[:, tail0:], c_s, nn_s, m_s[:, 0]))

    stack = lambda sts, k, axis=0: jnp.stack([s[k] for s in sts], axis=axis)
    return (y_prompt, y_sample,
            stack(st_p, 0, 2), stack(st_p, 1), stack(st_p, 2), stack(st_p, 3), stack(st_p, 4), stack(st_p, 5),
            stack(st_s, 0, 2), stack(st_s, 1), stack(st_s, 2), stack(st_s, 3), stack(st_s, 4), stack(st_s, 5))
```

```python
import functools
import math

import jax
import jax.numpy as jnp
import numpy as np
from jax import lax
from jax.experimental import pallas as pl
from jax.experimental.pallas import tpu as pltpu

F32 = jnp.float32
BF16 = jnp.bfloat16

M_HEADS = 4
M_DV = 256
M_DQK = 128
M_WIDTH = M_HEADS * M_DV
M_CONV = 4
N_HEADS = 16
N_KV = 2
N_HD = 64
N_HPG = N_HEADS // N_KV
N_WIDTH = N_HEADS * N_HD
KV_W = N_KV * N_HD
CMP_LEN = 32
CMP_STRIDE = 16
SLC_BLOCK = 64
N_SELECT = 16
WINDOW = 512
FORCE_BONUS = 1000.0
N_EXPERTS = 32
TOP_K = 4
SWIGLU_LIMIT = 7.0
SWIGLU_ALPHA = 1.702
EPS = 1e-6
TINY = 1e-30
NEG_BIG = -1e30

LANES = 128
PK_MX = 0
PK_MV = 1024
PK_MO = 2048
PK_NQ = 3072
PK_NKV = 4096
PK_SMALL = PK_NKV + 6 * KV_W
PK_BG = 5120
PK_TOTAL = 7168
SM_IG = 0
SM_LF = M_HEADS
SM_NG = 2 * M_HEADS

VMEM_LIMIT = 56 * 1024 * 1024


def _cparams(sem):
    return pltpu.CompilerParams(dimension_semantics=sem, vmem_limit_bytes=VMEM_LIMIT)


def _log_sigmoid(x):
    return jnp.minimum(x, 0.0) - jnp.log1p(jnp.exp(-jnp.abs(x)))


def _dot(a, b):
    return jnp.dot(a, b, preferred_element_type=F32)


def _dot_nt(a, b):
    return lax.dot_general(a, b, (((1,), (1,)), ((), ())), preferred_element_type=F32)


def _dot_tn(a, b):
    return lax.dot_general(a, b, (((0,), (0,)), ((), ())), preferred_element_type=F32)


def _norm_matmul_kernel(xp_ref, xs_ref, g_ref, w_ref, b_ref, o_ref, xn_ref, *, nbp):
    @pl.when(pl.program_id(1) == 0)
    def _():
        x = jnp.where(pl.program_id(0) < nbp, xp_ref[...], xs_ref[...])
        ms = jnp.mean(x * x, axis=-1, keepdims=True)
        xn_ref[...] = (x * lax.rsqrt(ms + EPS) * g_ref[...]).astype(BF16)

    o_ref[...] = _dot(xn_ref[...], w_ref[...]) + b_ref[...]


def norm_matmul(xp, xs, g, w, b, tm, tn):
    n_p, d = xp.shape
    n = n_p + xs.shape[0]
    nbp = n_p // tm
    nc = w.shape[1]
    return pl.pallas_call(
        functools.partial(_norm_matmul_kernel, nbp=nbp),
        grid=(n // tm, nc // tn),
        in_specs=[
            pl.BlockSpec((tm, d), lambda i, j: (jnp.minimum(i, nbp - 1), 0)),
            pl.BlockSpec((tm, d), lambda i, j: (jnp.maximum(i - nbp, 0), 0)),
            pl.BlockSpec((1, d), lambda i, j: (0, 0)),
            pl.BlockSpec((d, tn), lambda i, j: (0, j)),
            pl.BlockSpec((1, tn), lambda i, j: (0, j)),
        ],
        out_specs=pl.BlockSpec((tm, tn), lambda i, j: (i, j)),
        out_shape=jax.ShapeDtypeStruct((n, nc), F32),
        scratch_shapes=[pltpu.VMEM((tm, d), BF16)],
        compiler_params=_cparams(("parallel", "arbitrary")),
        name="norm_matmul",
    )(xp, xs, g, w, b)


def _shift_rows(x, tail, d):
    rows = x.shape[0]
    xd = pltpu.roll(x, d, 0)
    td = pltpu.roll(tail, d, 0)
    head = jnp.where(lax.broadcasted_iota(jnp.int32, td.shape, 0) < d, td, xd[:8])
    if rows == 8:
        return head
    return jnp.concatenate([head, xd[8:]], axis=0)


def _cumsum_rows(x):
    n = x.shape[0]
    idx = lax.broadcasted_iota(jnp.int32, x.shape, 0)
    k = 1
    while k < n:
        x = x + jnp.where(idx >= k, pltpu.roll(x, k, 0), 0.0)
        k *= 2
    return x


def _cumsum_lanes(x, n):
    idx = lax.broadcasted_iota(jnp.int32, x.shape, 1)
    k = 1
    while k < n:
        x = x + jnp.where(idx >= k, pltpu.roll(x, k, 1), 0.0)
        k *= 2
    return x


def _mlstm_kernel(xm_ref, v_ref, o_ref, gc_ref, gr_ref, cbuf_ref, c0_ref, n0_ref, m0_ref,
                  wconv_ref, bconv_ref, wq_ref, wk_ref, gn_ref,
                  h_ref, cout_ref, nout_ref, mout_ref, tout_ref,
                  c_sc, n_sc, m_sc, tail_sc, *, chunk):
    L = chunk
    c = pl.program_id(1)

    @pl.when(c == 0)
    def _init():
        c_sc[...] = c0_ref[0]
        n_sc[...] = n0_ref[0]
        m_sc[...] = m0_ref[0]
        tail_sc[...] = cbuf_ref[0]

    x = xm_ref[...]
    tail = tail_sc[...]
    wc = wconv_ref[...]
    xc = x * wc[M_CONV - 1:M_CONV] + bconv_ref[...]
    for d in range(1, M_CONV):
        xc = xc + _shift_rows(x, tail, d) * wc[M_CONV - 1 - d:M_CONV - d]
    tail_sc[...] = x[L - 8:]
    xc = xc * jax.nn.sigmoid(xc)
    xcb = xc.astype(BF16)

    gc = gc_ref[...]
    gr = gr_ref[0]
    b_col = _cumsum_rows(_log_sigmoid(gc))
    b_row = _cumsum_lanes(_log_sigmoid(gr), L)
    causal = (lax.broadcasted_iota(jnp.int32, (L, L), 1) <= lax.broadcasted_iota(jnp.int32, (L, L), 0))

    vb = v_ref[...].astype(BF16)
    og = jax.nn.sigmoid(o_ref[...])
    gn = gn_ref[...]
    m_all = m_sc[...]
    scale = M_DQK ** -0.5
    m_new_list = []
    for h in range(M_HEADS):
        hs = slice(h * M_DV, (h + 1) * M_DV)
        q = _dot(xcb[:, hs], wq_ref[h])
        k = _dot(xcb[:, hs], wk_ref[h]) * scale
        qb = q.astype(BF16)
        m_prev = m_all[:, h:h + 1]
        bc = b_col[:, SM_LF + h:SM_LF + h + 1]
        igc = gc[:, SM_IG + h:SM_IG + h + 1]
        br = b_row[SM_LF + h:SM_LF + h + 1, :L]
        igr = gr[SM_IG + h:SM_IG + h + 1, :L]
        a = bc + m_prev
        dmat = jnp.where(causal, bc - br + igr, -jnp.inf)
        mt = jnp.maximum(a, jnp.max(dmat, axis=1, keepdims=True))
        w_intra = jnp.exp(dmat - mt)
        w_inter = jnp.exp(a - mt)
        qk = _dot_nt(qb, k.astype(BF16)) * w_intra
        cmat = c_sc[h]
        nrow = n_sc[h:h + 1, :]
        vh = vb[:, hs]
        num = _dot(qk.astype(BF16), vh) + w_inter * _dot(qb, cmat.astype(BF16))
        den = jnp.sum(qk, axis=1, keepdims=True) + w_inter * jnp.sum(q * nrow, axis=1, keepdims=True)
        hh = num * (1.0 / jnp.maximum(jnp.abs(den), jnp.exp(-mt)))
        m_new = mt[L - 1:L, :]
        w_end = jnp.exp(bc[L - 1:L, :] - bc + igc - m_new)
        decay = jnp.exp(a[L - 1:L, :] - m_new)
        kw = k * w_end
        c_sc[h] = decay * cmat + _dot_tn(kw.astype(BF16), vh)
        n_sc[h:h + 1, :] = decay * nrow + jnp.sum(kw, axis=0, keepdims=True)
        m_new_list.append(m_new)
        hn = hh * lax.rsqrt(jnp.mean(hh * hh, axis=-1, keepdims=True) + EPS) * gn[:, hs]
        h_ref[0, :, hs] = (hn * og[:, hs]).astype(h_ref.dtype)
    m_sc[...] = jnp.concatenate(m_new_list, axis=1)

    @pl.when(c == pl.num_programs(1) - 1)
    def _fin():
        cout_ref[0] = c_sc[...]
        nout_ref[0] = n_sc[...]
        mout_ref[0] = m_sc[...]
        tout_ref[0] = tail_sc[...]


def mlstm(z, g_rows, row0, bsz, tlen, chunk, conv_buf8, c0, n0, m0, w_conv, b_conv, wq, wk, gn):
    L = chunk
    nc = tlen // L
    rb0 = row0 // L
    lr = g_rows.shape[2]
    grl = L if nc > 1 else lr

    def zspec(col):
        return pl.BlockSpec((L, 1024), lambda b, c: (rb0 + b * nc + c, col))

    kern = functools.partial(_mlstm_kernel, chunk=L)
    return pl.pallas_call(
        kern,
        grid=(bsz, nc),
        in_specs=[
            zspec(PK_MX // 1024), zspec(PK_MV // 1024), zspec(PK_MO // 1024),
            pl.BlockSpec((L, LANES), lambda b, c: (rb0 + b * nc + c, PK_SMALL // LANES)),
            pl.BlockSpec((1, 8, grl), lambda b, c: (b, 0, c)),
            pl.BlockSpec((1, 8, M_WIDTH), lambda b, c: (b, 0, 0)),
            pl.BlockSpec((1, M_HEADS, M_DQK, M_DV), lambda b, c: (b, 0, 0, 0)),
            pl.BlockSpec((1, M_HEADS, M_DQK), lambda b, c: (b, 0, 0)),
            pl.BlockSpec((1, 1, M_HEADS), lambda b, c: (b, 0, 0)),
            pl.BlockSpec((M_CONV, M_WIDTH), lambda b, c: (0, 0)),
            pl.BlockSpec((1, M_WIDTH), lambda b, c: (0, 0)),
            pl.BlockSpec((M_HEADS, M_DV, M_DQK), lambda b, c: (0, 0, 0)),
            pl.BlockSpec((M_HEADS, M_DV, M_DQK), lambda b, c: (0, 0, 0)),
            pl.BlockSpec((1, M_WIDTH), lambda b, c: (0, 0)),
        ],
        out_specs=[
            pl.BlockSpec((1, L, M_WIDTH), lambda b, c: (b, c, 0)),
            pl.BlockSpec((1, M_HEADS, M_DQK, M_DV), lambda b, c: (b, 0, 0, 0)),
            pl.BlockSpec((1, M_HEADS, M_DQK), lambda b, c: (b, 0, 0)),
            pl.BlockSpec((1, 1, M_HEADS), lambda b, c: (b, 0, 0)),
            pl.BlockSpec((1, 8, M_WIDTH), lambda b, c: (b, 0, 0)),
        ],
        out_shape=[
            jax.ShapeDtypeStruct((bsz, tlen, M_WIDTH), BF16),
            jax.ShapeDtypeStruct((bsz, M_HEADS, M_DQK, M_DV), F32),
            jax.ShapeDtypeStruct((bsz, M_HEADS, M_DQK), F32),
            jax.ShapeDtypeStruct((bsz, 1, M_HEADS), F32),
            jax.ShapeDtypeStruct((bsz, 8, M_WIDTH), F32),
        ],
        scratch_shapes=[
            pltpu.VMEM((M_HEADS, M_DQK, M_DV), F32),
            pltpu.VMEM((M_HEADS, M_DQK), F32),
            pltpu.VMEM((1, M_HEADS), F32),
            pltpu.VMEM((8, M_WIDTH), F32),
        ],
        compiler_params=_cparams(("parallel", "arbitrary")),
        name="mlstm",
    )(z, z, z, z, g_rows, conv_buf8, c0, n0, m0, w_conv, b_conv, wq, wk, gn)


OFF_MX = 0
OFF_MV = OFF_MX + M_WIDTH
OFF_MO = OFF_MV + M_WIDTH
OFF_MI = OFF_MO + M_WIDTH
OFF_MF = OFF_MI + M_HEADS
OFF_NQ = OFF_MF + M_HEADS
OFF_NKV = OFF_NQ + N_WIDTH
OFF_NG = OFF_NKV + 6 * KV_W
OFF_BG = OFF_NG + 3 * N_HEADS


def _pack_cols(a):
    lead = a.shape[:-1]
    d_model = a.shape[-1] - OFF_BG
    small = jnp.concatenate([a[..., OFF_MI:OFF_NQ], a[..., OFF_NG:OFF_BG]], axis=-1)
    small = jnp.concatenate([small, jnp.zeros(lead + (LANES - small.shape[-1],), a.dtype)], axis=-1)
    pad = jnp.zeros(lead + (PK_BG - PK_SMALL - LANES,), a.dtype)
    assert d_model == PK_TOTAL - PK_BG
    return jnp.concatenate([a[..., OFF_MX:OFF_MI], a[..., OFF_NQ:OFF_NKV], a[..., OFF_NKV:OFF_NG],
                            small, pad, a[..., OFF_BG:]], axis=-1)


Q_SCALE = (N_HD ** -0.5) * math.log2(math.e)

def _dup_halves(x, lane):
    r = pltpu.roll(x, N_HD, 1)
    lo = lane < N_HD
    return jnp.where(lo, x, r), jnp.where(lo, r, x)


def _masked_softmax_rows(s, mask):
    s = jnp.where(mask, s, -jnp.inf)
    mx = jnp.max(s, axis=1, keepdims=True)
    mx = jnp.where(mx > -jnp.inf, mx, 0.0)
    p = jnp.exp2(s - mx)
    return p * (1.0 / jnp.maximum(jnp.sum(p, axis=1, keepdims=True), TINY))


def _softmax_pv(s, v):
    m = jnp.max(s, axis=1, keepdims=True)
    p = jnp.exp2(s - m)
    return _dot(p.astype(BF16), v) * (1.0 / jnp.sum(p, axis=1, keepdims=True))


def _slc_attend(qxs, kx_ref, v_ref, g, s_sc, nfull, diag_bias, ck):
    rows = qxs[0].shape[0]
    nl = ck // LANES
    streams = range(len(qxs))
    lane_chunks = lambda s: [s[:, a * LANES:(a + 1) * LANES] for a in range(nl)]

    def scores(c):
        kt = kx_ref[g, pl.ds(pl.multiple_of(c * ck, ck), ck), :]
        return [_dot_nt(qx, kt) for qx in qxs]

    def max_pass(c, mvecs):
        out = []
        for n, s in enumerate(scores(c)):
            s_sc[n, c] = s
            out.append(functools.reduce(jnp.maximum, lane_chunks(s), mvecs[n]))
        return tuple(out)

    mvecs = lax.fori_loop(0, nfull, max_pass, tuple(jnp.full((rows, LANES), -jnp.inf, F32) for _ in streams))
    sds = [s + diag_bias for s in scores(nfull)]
    mbs = [jnp.broadcast_to(jnp.max(functools.reduce(jnp.maximum, lane_chunks(sd), mv), axis=1, keepdims=True),
                            (rows, LANES)) for sd, mv in zip(sds, mvecs)]

    def probs(ss, c):
        vt = v_ref[g, pl.ds(pl.multiple_of(c * ck, ck), ck), :]
        out = []
        for s, mb in zip(ss, mbs):
            ps = [jnp.exp2(x - mb) for x in lane_chunks(s)]
            out.append((functools.reduce(lambda x, y: x + y, ps), _dot(jnp.concatenate(ps, axis=1).astype(BF16), vt)))
        return tuple(out)

    def exp_pass(c, carry):
        new = probs([s_sc[n, c] for n in streams], c)
        return tuple((carry[n][0] + new[n][0], carry[n][1] + new[n][1]) for n in streams)

    res = lax.fori_loop(0, nfull, exp_pass, probs(sds, nfull))
    return [acc * (1.0 / jnp.sum(lvec, axis=1, keepdims=True)) for lvec, acc in res]


def _select_blocks(imp, qpos, ns, axis):
    blk = lax.broadcasted_iota(jnp.int32, imp.shape, axis)
    cur = qpos // SLC_BLOCK
    allowed = blk <= cur
    forced = (blk == 0) | (blk == cur) | (blk == cur - 1)
    score = jnp.where(allowed, imp + jnp.where(forced, FORCE_BONUS, 0.0), -jnp.inf)
    cnt = jnp.zeros(imp.shape, F32)
    for i in range(ns):
        si = score[:, i:i + 1] if axis == 1 else score[i:i + 1, :]
        ahead = (si > score) | ((si == score) & (blk > i))
        cnt = cnt + jnp.where(ahead, 1.0, 0.0)
    return jnp.where((cnt < float(N_SELECT)) & allowed, 1.0, 0.0)


def _stack_q(q_ref, g, rows):
    qs = jnp.concatenate([q_ref[:, (4 * g + p) * LANES:(4 * g + p + 1) * LANES] for p in range(N_HPG // 2)], axis=0)
    qs = qs * Q_SCALE
    lane = lax.broadcasted_iota(jnp.int32, qs.shape, 1)
    qe = jnp.where(lane < N_HD, qs, 0.0).astype(BF16)
    qo = jnp.where(lane < N_HD, 0.0, qs).astype(BF16)
    return qe, qo


def _gate_pair(gs, g, p, j, lane):
    he = g * N_HPG + 2 * p
    ce = SM_NG + 3 * he + j
    co = ce + 3
    return jnp.where(lane < N_HD, gs[:, ce:ce + 1], gs[:, co:co + 1])


def _nsa_prompt_kernel(q_ref, kv_ref, wcmp_ref, bcmp_ref, mmat_ref, o_ref, kvo_ref, wino_ref,
                       ck_sc, cv_sc, kx_sc, sv_sc, wk_sc, wv_sc, stage_sc, s_sc, *, seq, tq, ck, wn):
    i = pl.program_id(1)
    nseg = seq // CMP_STRIDE
    ns = seq // SLC_BLOCK
    npair = N_HPG // 2

    @pl.when(i == 0)
    def _prep():
        kvo_ref[0] = kv_ref[:, 0:4 * KV_W]
        wino_ref[0] = kv_ref[seq - wino_ref.shape[1]:, 4 * KV_W:6 * KV_W]
        kr = lax.broadcasted_iota(jnp.int32, (seq, LANES), 0)
        kc = lax.broadcasted_iota(jnp.int32, (seq, LANES), 1)
        onehot = jnp.where(kr // SLC_BLOCK == kc, 1.0, 0.0).astype(BF16)
        for g in range(N_KV):
            kx_sc[g, :, LANES:] = onehot
        lane = lax.broadcasted_iota(jnp.int32, (nseg, LANES), 1)
        for c, dst in ((0, ck_sc), (1, cv_sc)):
            acc = jnp.zeros((nseg, 2 * LANES), F32)
            stage_sc[...] = kv_ref[:, c * LANES:(c + 1) * LANES]
            for s in range(CMP_STRIDE):
                xs = stage_sc[pl.ds(s, nseg, stride=CMP_STRIDE), :]
                acc = acc + _dot(xs.astype(BF16), wcmp_ref[c, s])
            blocks = acc[:, :LANES] + pltpu.roll(acc[:, LANES:], nseg - 1, 0) + bcmp_ref[c:c + 1, :]
            d0, d1 = _dup_halves(blocks, lane)
            dst[0] = d0.astype(BF16)
            dst[1] = d1.astype(BF16)
        lane_s = lax.broadcasted_iota(jnp.int32, (seq, LANES), 1)
        for off, dst in ((2, kx_sc), (3, sv_sc), (4, wk_sc), (5, wv_sc)):
            d0, d1 = _dup_halves(kv_ref[:, off * LANES:(off + 1) * LANES], lane_s)
            dst[0, :, 0:LANES] = d0.astype(BF16)
            dst[1, :, 0:LANES] = d1.astype(BF16)

    t0 = i * tq
    qpos = t0 + lax.broadcasted_iota(jnp.int32, (tq, 1), 0)
    qpos_row = t0 + lax.broadcasted_iota(jnp.int32, (1, tq), 1)
    nsp = -(-ns // 8) * 8
    lane = lax.broadcasted_iota(jnp.int32, (tq, LANES), 1)
    gs = jax.nn.sigmoid(kv_ref[pl.ds(pl.multiple_of(t0, tq), tq), 6 * LANES:7 * LANES])
    c_end = lax.broadcasted_iota(jnp.int32, (tq, nseg), 1) * CMP_STRIDE + (CMP_LEN - 1)
    cmask = jnp.concatenate([c_end <= qpos] * npair, axis=0)
    lane4 = lax.broadcasted_iota(jnp.int32, (npair * tq, LANES), 1)
    lo4 = lane4 < N_HD

    nfull = t0 // ck
    kpos_d = nfull * ck + lax.broadcasted_iota(jnp.int32, (tq, ck), 1)
    diag_bias = jnp.concatenate([jnp.where(kpos_d <= qpos, 0.0, NEG_BIG)] * npair, axis=0)
    ks = pl.multiple_of(jnp.clip(t0 + tq - wn, 0, seq - wn), LANES)
    dpos = qpos - (ks + lax.broadcasted_iota(jnp.int32, (tq, wn), 1))
    win_bias = jnp.concatenate([jnp.where((dpos >= 0) & (dpos < WINDOW), 0.0, NEG_BIG)] * npair, axis=0)

    staged = []
    for g in range(N_KV):
        qe, qo = _stack_q(q_ref, g, tq)
        ckg = ck_sc[g]
        cvg = cv_sc[g]
        p_e = _masked_softmax_rows(_dot_nt(qe, ckg), cmask)
        p_o = _masked_softmax_rows(_dot_nt(qo, ckg), cmask)
        o_cmp = jnp.where(lo4, _dot(p_e.astype(BF16), cvg), _dot(p_o.astype(BF16), cvg))
        psum = p_e + p_o
        psum = sum(psum[p * tq:(p + 1) * tq] for p in range(npair))
        imp_t = lax.dot_general(mmat_ref[...], psum, (((1,), (1,)), ((), ())),
                                precision=lax.Precision.HIGHEST, preferred_element_type=F32)
        sel_t = _select_blocks(imp_t[:nsp], qpos_row, ns, 0)
        sel = jnp.concatenate([sel_t, jnp.zeros((LANES - nsp, tq), F32)], axis=0).T
        selb = ((sel - 1.0) * (-NEG_BIG)).astype(BF16)
        selb4 = jnp.concatenate([selb] * npair, axis=0)
        kw = wk_sc[g, pl.ds(ks, wn), :]
        vw = wv_sc[g, pl.ds(ks, wn), :]
        o_win = jnp.where(lo4, _softmax_pv(_dot_nt(qe, kw) + win_bias, vw), _softmax_pv(_dot_nt(qo, kw) + win_bias, vw))
        staged.append((jnp.concatenate([qe, selb4], axis=1), jnp.concatenate([qo, selb4], axis=1), o_cmp, o_win))

    for g in range(N_KV):
        qxe, qxo, o_cmp, o_win = staged[g]
        o_e, o_o = _slc_attend([qxe, qxo], kx_sc, sv_sc, g, s_sc, nfull, diag_bias, ck)
        o_slc = jnp.where(lo4, o_e, o_o)

        for p in range(npair):
            rs = slice(p * tq, (p + 1) * tq)
            o = (_gate_pair(gs, g, p, 0, lane) * o_cmp[rs] + _gate_pair(gs, g, p, 1, lane) * o_slc[rs]
                 + _gate_pair(gs, g, p, 2, lane) * o_win[rs])
            c0 = (g * npair + p) * LANES
            o_ref[:, c0:c0 + LANES] = o.astype(o_ref.dtype)


def _cmp_to_slc_matrix(nc, ns, rows, cols):
    c0 = np.arange(nc)[:, None] * CMP_STRIDE
    s0 = np.arange(ns)[None, :] * SLC_BLOCK
    ov = np.clip(np.minimum(c0 + CMP_LEN, s0 + SLC_BLOCK) - np.maximum(c0, s0), 0, None) / CMP_LEN
    out = np.zeros((rows, cols), np.float32)
    out[:nc, :ns] = ov
    return jnp.asarray(out)


def _pack_cmp_weights(w_cmp, b_cmp):
    r = CMP_LEN // CMP_STRIDE
    w = w_cmp.reshape(2, r, CMP_STRIDE, N_HD, N_HD)
    eye = jnp.eye(N_KV, dtype=w.dtype)
    bd = jnp.einsum('gh,crsde->crsgdhe', eye, w).reshape(2, r, CMP_STRIDE, KV_W, KV_W)
    wp = jnp.concatenate([bd[:, k] for k in range(r)], axis=-1)
    bp = jnp.concatenate([b_cmp] * N_KV, axis=-1)
    return wp.astype(BF16), bp


def nsa_prompt(z, bsz, seq, wcmp_p, bcmp_p, tq=256, ck=256):
    nseg = seq // CMP_STRIDE
    ns = seq // SLC_BLOCK
    nq = seq // tq
    ck = min(ck, seq)
    wn = min(WINDOW + tq, seq)
    assert seq % ck == 0 and ck % tq == 0 and ns <= LANES and seq % LANES == 0
    mmat = _cmp_to_slc_matrix(nseg - 1, ns, nseg, LANES).T
    wlen = min(WINDOW, seq)
    once = pl.Buffered(1)
    kern = functools.partial(_nsa_prompt_kernel, seq=seq, tq=tq, ck=ck, wn=wn)
    return pl.pallas_call(
        kern,
        grid=(bsz, nq),
        in_specs=[
            pl.BlockSpec((tq, 1024), lambda b, i: (b * nq + i, PK_NQ // 1024)),
            pl.BlockSpec((seq, 1024), lambda b, i: (b, PK_NKV // 1024), pipeline_mode=once),
            pl.BlockSpec((2, CMP_STRIDE, KV_W, 2 * KV_W), lambda b, i: (0, 0, 0, 0)),
            pl.BlockSpec((2, KV_W), lambda b, i: (0, 0)),
            pl.BlockSpec((LANES, nseg), lambda b, i: (0, 0)),
        ],
        out_specs=[pl.BlockSpec((tq, N_WIDTH), lambda b, i: (b * nq + i, 0)),
                   pl.BlockSpec((1, seq, 4 * KV_W), lambda b, i: (b, 0, 0), pipeline_mode=once),
                   pl.BlockSpec((1, wlen, 2 * KV_W), lambda b, i: (b, 0, 0), pipeline_mode=once)],
        out_shape=[jax.ShapeDtypeStruct((bsz * seq, N_WIDTH), BF16),
                   jax.ShapeDtypeStruct((bsz, seq, 4 * KV_W), F32),
                   jax.ShapeDtypeStruct((bsz, wlen, 2 * KV_W), F32)],
        scratch_shapes=[
            pltpu.VMEM((N_KV, nseg, LANES), BF16), pltpu.VMEM((N_KV, nseg, LANES), BF16),
            pltpu.VMEM((N_KV, seq, 2 * LANES), BF16), pltpu.VMEM((N_KV, seq, LANES), BF16),
            pltpu.VMEM((N_KV, seq, LANES), BF16), pltpu.VMEM((N_KV, seq, LANES), BF16),
            pltpu.VMEM((seq, LANES), F32),
            pltpu.VMEM((2, seq // ck, (N_HPG // 2) * tq, ck), F32),
        ],
        compiler_params=_cparams(("parallel", "arbitrary")),
        name="nsa_prompt",
    )(z, z, wcmp_p, bcmp_p, mmat)


def _merge_kernel(xp_ref, xs_ref, hmp_ref, hms_ref, hnp_ref, hns_ref, bgm_ref, bgn_ref,
                  wbm_ref, wbn_ref, wout_ref, n2_ref, wr_ref, br_ref,
                  h_ref, xn2_ref, te_ref, tw_ref, pos_ref, cnt_ref, run_sc, *, nbp):
    i = pl.program_id(0)
    is_p = i < nbp

    @pl.when(i == 0)
    def _():
        run_sc[...] = jnp.zeros_like(run_sc)

    part = h_ref.shape[0] // MERGE_SPLIT
    run = run_sc[...]
    for k in range(MERGE_SPLIT):
        rs = slice(k * part, (k + 1) * part)
        x = jnp.where(is_p, xp_ref[rs, :], xs_ref[rs, :])
        hm = jnp.where(is_p, hmp_ref[rs, :], hms_ref[rs, :])
        hn = jnp.where(is_p, hnp_ref[rs, :], hns_ref[rs, :])
        t = (jax.nn.sigmoid(bgm_ref[rs, :]) * _dot(hm, wbm_ref[...])
             + jax.nn.sigmoid(bgn_ref[rs, :]) * _dot(hn, wbn_ref[...]))
        h = x + _dot(t.astype(BF16), wout_ref[...])
        h_ref[rs, :] = h
        xn2 = (h * lax.rsqrt(jnp.mean(h * h, axis=-1, keepdims=True) + EPS) * n2_ref[...]).astype(BF16)
        xn2_ref[rs, :] = xn2

        logits = _dot(xn2, wr_ref[...]) + br_ref[...]
        lane = lax.broadcasted_iota(jnp.int32, logits.shape, 1)
        cur = jnp.where(lane < N_EXPERTS, logits, -jnp.inf)
        vals, idxs, hots = [], [], []
        for _ in range(TOP_K):
            m = jnp.max(cur, axis=1, keepdims=True)
            idx = jnp.min(jnp.where(cur == m, lane, LANES), axis=1, keepdims=True)
            hot = lane == idx
            vals.append(m)
            idxs.append(idx)
            hots.append(hot)
            cur = jnp.where(hot, -jnp.inf, cur)
        es = [jnp.exp(v - vals[0]) for v in vals]
        inv = 1.0 / functools.reduce(lambda a, b: a + b, es)
        tw_ref[rs, :] = jnp.concatenate([e * inv for e in es], axis=1)
        te_ref[rs, :] = jnp.concatenate(idxs, axis=1)
        cnt = functools.reduce(lambda a, b: a + b, [jnp.where(hot, 1.0, 0.0) for hot in hots])
        incl = _cumsum_rows(cnt)
        before = incl - cnt + run
        pos = [jnp.sum(jnp.where(hot, before, 0.0), axis=1, keepdims=True) for hot in hots]
        pos_ref[rs, :] = jnp.concatenate(pos, axis=1).astype(jnp.int32)
        run = run + incl[part - 1:, :]
    run_sc[...] = run
    cnt_ref[...] = run


def merge(xp, xs, hmp, hms, hnp, hns, z, wbm, wbn, wout, n2, wr, br, tm):
    n_p, d = xp.shape
    n = n_p + xs.shape[0]
    nbp = n_p // tm
    row = lambda i: (i, 0)
    prow = lambda i: (jnp.minimum(i, nbp - 1), 0)
    srow = lambda i: (jnp.maximum(i - nbp, 0), 0)
    fixed = lambda i: (0, 0)
    pair = [pl.BlockSpec((tm, d), prow), pl.BlockSpec((tm, d), srow)]
    return pl.pallas_call(
        functools.partial(_merge_kernel, nbp=nbp),
        grid=(n // tm,),
        in_specs=pair * 3 + [
            pl.BlockSpec((tm, d), lambda i: (i, PK_BG // 1024)),
            pl.BlockSpec((tm, d), lambda i: (i, PK_BG // 1024 + 1)),
            pl.BlockSpec((d, d), fixed), pl.BlockSpec((d, d), fixed), pl.BlockSpec((d, d), fixed),
            pl.BlockSpec((1, d), fixed), pl.BlockSpec((d, LANES), fixed), pl.BlockSpec((1, LANES), fixed),
        ],
        out_specs=[pl.BlockSpec((tm, d), row), pl.BlockSpec((tm, d), row),
                   pl.BlockSpec((tm, TOP_K), row), pl.BlockSpec((tm, TOP_K), row), pl.BlockSpec((tm, TOP_K), row),
                   pl.BlockSpec((1, LANES), fixed)],
        out_shape=[jax.ShapeDtypeStruct((n, d), F32), jax.ShapeDtypeStruct((n, d), BF16),
                   jax.ShapeDtypeStruct((n, TOP_K), jnp.int32), jax.ShapeDtypeStruct((n, TOP_K), F32),
                   jax.ShapeDtypeStruct((n, TOP_K), jnp.int32), jax.ShapeDtypeStruct((1, LANES), F32)],
        scratch_shapes=[pltpu.VMEM((1, LANES), F32)],
        compiler_params=_cparams(("arbitrary",)),
        name="merge",
    )(xp, xs, hmp, hms, hnp, hns, z, z, wbm, wbn, wout, n2, wr, br)


def _expert_kernel(te_ref, nu_ref, x_ref, wgu_ref, bgu_ref, wdn_ref, bdn_ref, *rest, tile0):
    y_ref, wgu_sc, wdn_sc = rest[-3:]
    t = pl.program_id(0)
    tg = tile0 + t

    @pl.when(tg < nu_ref[0])
    def _():
        @pl.when((t == 0) | (te_ref[tg] != te_ref[jnp.maximum(tg - 1, 0)]))
        def _cast():
            wgu_sc[...] = wgu_ref[0].astype(BF16)
            wdn_sc[...] = wdn_ref[0].astype(BF16)

        de = wdn_sc.shape[0]
        gu = _dot(x_ref[...], wgu_sc[...]) + bgu_ref[0]
        gate = jnp.minimum(gu[:, :de], SWIGLU_LIMIT)
        up = jnp.clip(gu[:, de:], -SWIGLU_LIMIT, SWIGLU_LIMIT)
        glu = gate * jax.nn.sigmoid(gate * SWIGLU_ALPHA)
        act = ((up + 1.0) * glu).astype(BF16)
        y_ref[...] = _dot(act, wdn_sc[...]) + bdn_ref[0]


def expert_ffn(x_part, tile0, tile_e, n_used, y_prev, wgu, bgu, wdn, bdn, tm):
    d = x_part.shape[1]
    de = wdn.shape[1]
    n_part = x_part.shape[0] // tm
    n_all = tile_e.shape[0]

    def local(t, nu):
        return jnp.clip(jnp.minimum(tile0 + t, nu[0] - 1) - tile0, 0, n_part - 1)

    per_e = lambda t, te, nu: (te[tile0 + t], 0, 0)
    in_specs = [
        pl.BlockSpec((tm, d), lambda t, te, nu: (local(t, nu), 0)),
        pl.BlockSpec((1, d, 2 * de), per_e), pl.BlockSpec((1, 1, 2 * de), per_e),
        pl.BlockSpec((1, de, d), per_e), pl.BlockSpec((1, 1, d), per_e),
    ]
    operands = [tile_e, n_used, x_part, wgu, bgu, wdn, bdn]
    aliases = {}
    if y_prev is not None:
        in_specs.append(pl.BlockSpec(memory_space=pl.ANY))
        aliases = {len(operands): 0}
        operands.append(y_prev)
    grid_spec = pltpu.PrefetchScalarGridSpec(
        num_scalar_prefetch=2,
        grid=(n_part,),
        in_specs=in_specs,
        out_specs=pl.BlockSpec((tm, d), lambda t, te, nu: (tile0 + local(t, nu), 0)),
        scratch_shapes=[pltpu.VMEM((d, 2 * de), BF16), pltpu.VMEM((de, d), BF16)],
    )
    return pl.pallas_call(
        functools.partial(_expert_kernel, tile0=tile0),
        grid_spec=grid_spec,
        out_shape=jax.ShapeDtypeStruct((n_all * tm, d), F32),
        input_output_aliases=aliases,
        compiler_params=_cparams(("arbitrary",)),
        name="expert_ffn",
    )(*operands)


def _combine_kernel(h_ref, *refs):
    y_refs = refs[:TOP_K]
    w_ref, g_ref = refs[TOP_K:TOP_K + 2]
    o_ref = refs[-1]
    w = w_ref[...]
    moe = w[:, 0:1] * y_refs[0][...]
    for k in range(1, TOP_K):
        moe = moe + w[:, k:k + 1] * y_refs[k][...]
    acc = h_ref[...] + moe
    o_ref[...] = acc * lax.rsqrt(jnp.mean(acc * acc, axis=-1, keepdims=True) + EPS) * g_ref[...]


def combine(h, y4, w4, g, tm, row0, out_row0, out_rows, out_prev):
    d = h.shape[1]
    nrows = y4.shape[0] // TOP_K
    nb = nrows // tm
    rb0 = row0 // tm
    ob0 = out_row0 // tm
    in_specs = ([pl.BlockSpec((tm, d), lambda i: (rb0 + i, 0))]
                + [pl.BlockSpec((tm, d), functools.partial(lambda i, k: (k * nb + i, 0), k=k)) for k in range(TOP_K)]
                + [pl.BlockSpec((tm, TOP_K), lambda i: (rb0 + i, 0)), pl.BlockSpec((1, d), lambda i: (0, 0))])
    operands = [h] + [y4] * TOP_K + [w4, g]
    aliases = {}
    if out_prev is not None:
        in_specs.append(pl.BlockSpec(memory_space=pl.ANY))
        aliases = {len(operands): 0}
        operands.append(out_prev)
    return pl.pallas_call(
        _combine_kernel,
        grid=(nb,),
        in_specs=in_specs,
        out_specs=pl.BlockSpec((tm, d), lambda i: (ob0 + i, 0)),
        out_shape=jax.ShapeDtypeStruct((out_rows, d), F32),
        input_output_aliases=aliases,
        compiler_params=_cparams(("parallel",)),
        name="combine",
    )(*operands)


def _slot_layout(top_e, pos, counts, tm):
    n = top_e.shape[0]
    nk = n * TOP_K
    padded = (counts + tm - 1) // tm * tm
    pad_end = jnp.cumsum(padded)
    pad_start = pad_end - padded
    grp_start = jnp.cumsum(counts) - counts
    n_tiles = -(-(nk + N_EXPERTS * (tm - 1)) // tm)
    tile_e = jnp.minimum(jnp.sum(pad_end[None, :] <= (jnp.arange(n_tiles) * tm)[:, None], axis=1),
                         N_EXPERTS - 1).astype(jnp.int32)
    hot = top_e[..., None] == jnp.arange(N_EXPERTS)
    inv_slot = (jnp.sum(jnp.where(hot, pad_start, 0), axis=-1) + pos).astype(jnp.int32)
    order = jnp.argsort(top_e.reshape(-1))
    slot = jnp.arange(n_tiles * tm, dtype=jnp.int32)
    slot_e = jnp.repeat(tile_e, tm)
    src = jnp.clip(grp_start[slot_e] + slot - pad_start[slot_e], 0, nk - 1)
    slot_tok = (jnp.take(order, src, mode='clip') // TOP_K).astype(jnp.int32)
    n_used = (pad_end[-1:] // tm).astype(jnp.int32)
    return slot_tok, inv_slot, tile_e, n_used


def _plain_softmax_parts(parts):
    ms = [jnp.where(m, s, NEG_BIG) if m is not None else s for s, m in parts]
    mx = functools.reduce(jnp.maximum, [jnp.max(s, axis=1, keepdims=True) for s in ms])
    ps = [jnp.exp2(s - mx) for s in ms]
    den = functools.reduce(lambda a, b: a + b, [jnp.sum(p, axis=1, keepdims=True) for p in ps])
    inv = 1.0 / den
    return [p * inv for p in ps]


def _nsa_sample_kernel(pt_ref, *refs, npages, pps, tlen, page, wlen):
    page_refs = refs[:pps]
    (q_ref, kvn_ref, win_ref, wc_ref, bc_ref, mmat_ref, o_ref,
     stk_sc, stv_sc, kx_sc, v_sc, s_sc) = refs[pps:]
    b = pl.program_id(0)
    j = pl.program_id(1)
    nsteps = npages // pps
    past = npages * page
    nseg = past // CMP_STRIDE
    nblk = past // SLC_BLOCK
    ns = nblk + 1
    rows = N_HEADS * tlen
    grows = N_HPG * tlen

    @pl.when((b == 0) & (j == 0))
    def _onehot():
        r = lax.broadcasted_iota(jnp.int32, (LANES, past), 0)
        c = lax.broadcasted_iota(jnp.int32, (LANES, past), 1)
        kx_sc[LANES:, :] = jnp.where(c // SLC_BLOCK == r, 1.0, 0.0).astype(BF16)

    for k in range(pps):
        r0 = pl.multiple_of((j * pps + k) * page, page)
        pg = page_refs[k]
        stk_sc[pl.ds(r0, page), :] = pg[0, 0:LANES, :].T
        stv_sc[pl.ds(r0, page), :] = pg[0, LANES:2 * LANES, :].T
        kx_sc[0:LANES, pl.ds(r0, page)] = pg[0, 2 * LANES:3 * LANES, :].astype(BF16)
        v_sc[:, pl.ds(r0, page)] = pg[0, 3 * LANES:4 * LANES, :].astype(BF16)

    @pl.when(j == nsteps - 1)
    def _attend():
        lane = lax.broadcasted_iota(jnp.int32, (tlen, LANES), 1)
        lo = lane < N_HD

        def cmp_proj(st_sc, c):
            acc = jnp.zeros((nseg, 2 * LANES), F32)
            for sp in range(CMP_STRIDE // 2):
                xa = st_sc[pl.ds(2 * sp, nseg, stride=CMP_STRIDE), :]
                xb = st_sc[pl.ds(2 * sp + 1, nseg, stride=CMP_STRIDE), :]
                acc = acc + _dot(jnp.concatenate([xa, xb], axis=1).astype(BF16), wc_ref[c, sp])
            blocks = acc[:, :LANES] + pltpu.roll(acc[:, LANES:], nseg - 1, 0) + bc_ref[c:c + 1, :]
            return blocks.astype(BF16)

        ck = cmp_proj(stk_sc, 0)
        cv = cmp_proj(stv_sc, 1)

        qall = q_ref[...] * Q_SCALE
        pieces = []
        for h in range(N_HEADS):
            slab = qall[:, (h // 2) * LANES:(h // 2 + 1) * LANES]
            g = h // N_HPG
            if (h % 2) != g:
                slab = pltpu.roll(slab, N_HD, 1)
            pieces.append(jnp.where(lo, slab, 0.0) if g == 0 else jnp.where(lo, 0.0, slab))
        qbd = jnp.concatenate(pieces, axis=0).astype(BF16)
        trow = lax.broadcasted_iota(jnp.int32, (rows, 1), 0) & (tlen - 1)
        qpos = past + trow
        t8 = lax.broadcasted_iota(jnp.int32, (tlen, 1), 0)

        kvn = kvn_ref[...]
        zpad = jnp.zeros((LANES - tlen, LANES), F32)
        newk = lambda off: jnp.concatenate([kvn[:, off * LANES:(off + 1) * LANES], zpad], axis=0).astype(BF16)
        new_lane = lax.broadcasted_iota(jnp.int32, (rows, LANES), 1)
        new_mask = new_lane <= trow

        c_end = lax.broadcasted_iota(jnp.int32, (rows, nseg), 1) * CMP_STRIDE + (CMP_LEN - 1)
        p_cmp = _masked_softmax_rows(_dot_nt(qbd, ck), c_end <= qpos)
        o_cmp = _dot(p_cmp.astype(BF16), cv)

        bias_rows, bias_new = [], []
        for g in range(N_KV):
            psum = sum(p_cmp[g * grows + h * tlen:g * grows + (h + 1) * tlen] for h in range(N_HPG))
            imp = jnp.dot(psum, mmat_ref[...], precision=lax.Precision.HIGHEST, preferred_element_type=F32)
            sel = _select_blocks(imp, past + t8, ns, 1)
            selb = (sel - 1.0) * (-NEG_BIG)
            bias_rows += [selb[:, :LANES]] * N_HPG
            bias_new += [selb[:, nblk:nblk + 1]] * N_HPG
        qx = jnp.concatenate([qbd, jnp.concatenate(bias_rows, axis=0).astype(BF16)], axis=1)
        bias_new = jnp.concatenate(bias_new, axis=0)

        ckeys = min(past, 1024)
        mx = jnp.full((rows, 1), -jnp.inf, F32)
        for c0 in range(0, past, ckeys):
            s = _dot(qx, kx_sc[:, c0:c0 + ckeys])
            s_sc[:, c0:c0 + ckeys] = s
            mx = jnp.maximum(mx, jnp.max(s, axis=1, keepdims=True))
        s_new = jnp.where(new_mask, _dot_nt(qbd, newk(2)) + bias_new, NEG_BIG)
        mx = jnp.maximum(mx, jnp.max(s_new, axis=1, keepdims=True))
        p_new = jnp.exp2(s_new - mx)
        den = jnp.sum(p_new, axis=1, keepdims=True)
        acc = _dot(p_new.astype(BF16), newk(3))
        for c0 in range(0, past, ckeys):
            p = jnp.exp2(s_sc[:, c0:c0 + ckeys] - mx)
            den = den + jnp.sum(p, axis=1, keepdims=True)
            acc = acc + _dot_nt(p.astype(BF16), v_sc[:, c0:c0 + ckeys])
        o_slc = acc * (1.0 / den)

        wr = lax.broadcasted_iota(jnp.int32, (rows, wlen), 1)
        dpos = trow + wlen - wr
        wk_old = win_ref[0, :, 0:LANES].astype(BF16)
        wv_old = win_ref[0, :, LANES:2 * LANES].astype(BF16)
        pw_old, pw_new = _plain_softmax_parts([(_dot_nt(qbd, wk_old), (dpos >= 0) & (dpos < WINDOW)),
                                               (_dot_nt(qbd, newk(4)), new_mask)])
        o_win = _dot(pw_old.astype(BF16), wv_old) + _dot(pw_new.astype(BF16), newk(5))

        gs = jax.nn.sigmoid(kvn[:, 6 * LANES:7 * LANES])
        for pr in range(N_HEADS // 2):
            halves = []
            for par in range(2):
                h = 2 * pr + par
                g = h // N_HPG
                rs = slice(h * tlen, (h + 1) * tlen)
                c0 = SM_NG + 3 * h
                o = (gs[:, c0:c0 + 1] * o_cmp[rs] + gs[:, c0 + 1:c0 + 2] * o_slc[rs]
                     + gs[:, c0 + 2:c0 + 3] * o_win[rs])
                halves.append(pltpu.roll(o, N_HD, 1) if par != g else o)
            o_ref[0, :, pr * LANES:(pr + 1) * LANES] = jnp.where(lo, halves[0], halves[1])


def nsa_sample(z, row0, page_table, cache_t, layer, win2, wcmp_p, bcmp_p, tlen, pps=32):
    bsz, npages = page_table.shape
    pps = min(pps, npages)
    page = cache_t.shape[2]
    wlen = win2.shape[1]
    past = npages * page
    nseg = past // CMP_STRIDE
    nblk = past // SLC_BLOCK
    assert npages % pps == 0 and nblk <= LANES and tlen & (tlen - 1) == 0 and tlen <= SLC_BLOCK and wlen == WINDOW
    mmat = _cmp_to_slc_matrix(nseg - 1, nblk + 1, nseg, 2 * LANES)
    wc2 = wcmp_p.reshape(2, CMP_STRIDE // 2, 2 * KV_W, 2 * KV_W)
    rb0 = row0 // tlen
    kern = functools.partial(_nsa_sample_kernel, npages=npages, pps=pps, tlen=tlen, page=page, wlen=wlen)

    def page_spec(k):
        return pl.BlockSpec((1, 4 * KV_W, page), lambda b, j, pt: (pt[b, j * pps + k], layer, 0))

    grid_spec = pltpu.PrefetchScalarGridSpec(
        num_scalar_prefetch=1,
        grid=(bsz, npages // pps),
        in_specs=[page_spec(k) for k in range(pps)] + [
            pl.BlockSpec((tlen, 1024), lambda b, j, pt: (rb0 + b, PK_NQ // 1024)),
            pl.BlockSpec((tlen, 1024), lambda b, j, pt: (rb0 + b, PK_NKV // 1024)),
            pl.BlockSpec((1, wlen, 2 * KV_W), lambda b, j, pt: (b, 0, 0)),
            pl.BlockSpec((2, CMP_STRIDE // 2, 2 * KV_W, 2 * KV_W), lambda b, j, pt: (0, 0, 0, 0)),
            pl.BlockSpec((2, KV_W), lambda b, j, pt: (0, 0)),
            pl.BlockSpec((nseg, 2 * LANES), lambda b, j, pt: (0, 0)),
        ],
        out_specs=pl.BlockSpec((1, tlen, N_WIDTH), lambda b, j, pt: (b, 0, 0)),
        scratch_shapes=[
            pltpu.VMEM((past, LANES), F32), pltpu.VMEM((past, LANES), F32),
            pltpu.VMEM((2 * LANES, past), BF16), pltpu.VMEM((LANES, past), BF16),
            pltpu.VMEM((N_HEADS * tlen, past), F32),
        ],
    )
    return pl.pallas_call(
        kern,
        grid_spec=grid_spec,
        out_shape=jax.ShapeDtypeStruct((bsz, tlen, N_WIDTH), F32),
        compiler_params=_cparams(("arbitrary", "arbitrary")),
        name="nsa_sample",
    )(page_table, *([cache_t] * pps), z, z, win2, wc2, bcmp_p, mmat)


ROW_TILE = 1024
MERGE_TILE = 512
MERGE_SPLIT = 2
MOE_TILE = 512
INPROJ_COLS = PK_TOTAL // 4
MOE_PARTS = 4
COMBINE_PARTS = 4
PROMPT_CHUNK = 512


def kernel(x_prompt, x_sample, cache_kv, cache_win_kv, state_conv, state_C, state_n, state_m, page_table,
           norm1_g, w_in, b_in, w_conv, b_conv, w_mq, w_mk, g_mnorm, w_cmp, b_cmp,
           w_branch_m, w_branch_n, w_out, norm2_g, w_router, b_router, w_gu, b_gu, w_dn, b_dn, normf_g):
    bp, sp, d = x_prompt.shape
    bs, ts, _ = x_sample.shape
    depth = w_in.shape[0]
    n_p, n_s = bp * sp, bs * ts
    n = n_p + n_s
    assert n_p % ROW_TILE == 0 and n_s % ROW_TILE == 0 and n_p % MERGE_TILE == 0 and n_s % MERGE_TILE == 0
    assert sp % PROMPT_CHUNK == 0 and ts % 8 == 0 and ts >= M_CONV - 1

    assert depth == 1, "only DEPTH == 1 is supported (the final norm is fused into the combine step)"
    xp2, xs2 = x_prompt.reshape(n_p, d), x_sample.reshape(n_s, d)
    st_p, st_s = [], []
    for l in range(depth):
        w_in_p = _pack_cols(w_in[l]).astype(BF16)
        b_in_p = _pack_cols(b_in[l][None])
        z = norm_matmul(xp2, xs2, norm1_g[l][None], w_in_p, b_in_p, ROW_TILE, INPROJ_COLS)

        wq, wk = w_mq[l].astype(BF16), w_mk[l].astype(BF16)
        gn = g_mnorm[l].reshape(1, M_WIDTH)
        bconv = b_conv[l][None]

        def gate_rows(r0, bsz, tlen):
            g = z[r0:r0 + bsz * tlen, PK_SMALL:PK_SMALL + 2 * M_HEADS]
            g = g.reshape(bsz, tlen, 2 * M_HEADS).transpose(0, 2, 1)
            if tlen < LANES:
                g = jnp.pad(g, ((0, 0), (0, 0), (0, LANES - tlen)))
            return g

        zero = lambda *s: jnp.zeros(s, F32)
        hm_p, c_p, nn_p, m_p, tail_p = mlstm(z, gate_rows(0, bp, sp), 0, bp, sp, PROMPT_CHUNK,
                                             zero(bp, 8, M_WIDTH), zero(bp, M_HEADS, M_DQK, M_DV),
                                             zero(bp, M_HEADS, M_DQK), zero(bp, 1, M_HEADS),
                                             w_conv[l], bconv, wq, wk, gn)
        cbuf_s = jnp.pad(state_conv[l], ((0, 0), (8 - (M_CONV - 1), 0), (0, 0)))
        hm_s, c_s, nn_s, m_s, tail_s = mlstm(z, gate_rows(n_p, bs, ts), n_p, bs, ts, ts,
                                             cbuf_s, state_C[l], state_n[l], state_m[l][:, None, :],
                                             w_conv[l], bconv, wq, wk, gn)

        wcmp_p, bcmp_p = _pack_cmp_weights(w_cmp[l], b_cmp[l])
        hn_p, kv_p, win_p = nsa_prompt(z, bp, sp, wcmp_p, bcmp_p)
        wlen_s = cache_win_kv.shape[2]
        win2 = cache_win_kv[l].reshape(bs, wlen_s, 2 * KV_W)
        cache_t = jnp.transpose(cache_kv, (0, 2, 3, 4, 5, 1)).reshape(cache_kv.shape[0], depth * 4 * KV_W,
                                                                      cache_kv.shape[1])
        hn_s = nsa_sample(z, n_p, page_table, cache_t, l, win2, wcmp_p, bcmp_p, ts)

        wr = jnp.pad(w_router[l], ((0, 0), (0, LANES - N_EXPERTS))).astype(BF16)
        br = jnp.pad(b_router[l], (0, LANES - N_EXPERTS))[None]
        h, xn2, top_e, top_w, pos, counts = merge(
            xp2, xs2, hm_p.reshape(n_p, M_WIDTH), hm_s.reshape(n_s, M_WIDTH),
            hn_p, hn_s.reshape(n_s, N_WIDTH).astype(BF16), z,
            w_branch_m[l].astype(BF16), w_branch_n[l].astype(BF16), w_out[l].astype(BF16),
            norm2_g[l][None], wr, br, MERGE_TILE)

        slot_tok, inv_slot, tile_e, n_used = _slot_layout(top_e, pos, counts[0, :N_EXPERTS].astype(jnp.int32),
                                                          MOE_TILE)
        n_tiles = tile_e.shape[0]
        tiles_per = -(-n_tiles // MOE_PARTS)
        y_slots = None
        for t0 in range(0, n_tiles, tiles_per):
            t1 = min(t0 + tiles_per, n_tiles)
            x_part = jnp.take(xn2, slot_tok[t0 * MOE_TILE:t1 * MOE_TILE], axis=0, mode='clip')
            y_slots = expert_ffn(x_part, t0, tile_e, n_used, y_slots, w_gu[l], b_gu[l][:, None, :],
                                 w_dn[l], b_dn[l][:, None, :], MOE_TILE)

        def combine_rows(r0, nrows, out_r0, out_rows, out_prev):
            y4 = jnp.take(y_slots, inv_slot[r0:r0 + nrows].T.reshape(-1), axis=0, mode='clip')
            return combine(h, y4, top_w, normf_g[None], ROW_TILE, r0, out_r0, out_rows, out_prev)

        rows_per = -(-n_p // (COMBINE_PARTS * ROW_TILE)) * ROW_TILE
        y_prompt = None
        for r0 in range(0, n_p, rows_per):
            y_prompt = combine_rows(r0, min(rows_per, n_p - r0), r0, n_p, y_prompt)
        y_prompt = y_prompt.reshape(bp, sp, d)
        y_sample = combine_rows(n_p, n_s, 0, n_s, None).reshape(bs, ts, d)

        kvw_s = z[n_p:, PK_NKV:PK_NKV + 6 * KV_W].reshape(bs, ts, 6, N_KV, N_HD)
        win_s = jnp.concatenate([cache_win_kv[l][:, ts:], kvw_s[:, :, 4:]], axis=1)
        tail0 = 8 - (M_CONV - 1)
        st_p.append((kv_p.reshape(bp, sp, 4, N_KV, N_HD), win_p.reshape(bp, -1, 2, N_KV, N_HD), tail_p[:, tail0:],
                     c_p, nn_p, m_p[:, 0]))
        st_s.append((kvw_s[:, :, :4], win_s, tail_s[:, tail0:], c_s, nn_s, m_s[:, 0]))

    stack = lambda sts, k, axis=0: jnp.stack([s[k] for s in sts], axis=axis)
    return (y_prompt, y_sample,
            stack(st_p, 0, 2), stack(st_p, 1), stack(st_p, 2), stack(st_p, 3), stack(st_p, 4), stack(st_p, 5),
            stack(st_s, 0, 2), stack(st_s, 1), stack(st_s, 2), stack(st_s, 3), stack(st_s, 4), stack(st_s, 5))
```
